```python
import jax, jax.numpy as jnp
from jax import lax

D_MODEL = 2048
BATCH = 32
SEQ = 256
DEPTH = 4
DEC_BATCH = 2
DEC_SEQ = 1024
PAST_LEN = 512

GRID_W = 64
D_MIX = D_MODEL
N_HEADS = 8
QK_NOPE = 128
QK_ROPE = 64
V_DIM = 128
Q_LORA = 512
KV_LORA = 256
ATTN_WIDTH = N_HEADS * V_DIM
CONV_DIM = D_MIX - ATTN_WIDTH
CONV_GROUPS = 8
CONV_W = 3
IN_COLS = Q_LORA + KV_LORA + QK_ROPE + 3 * CONV_DIM
ROPE_AXIS = QK_ROPE // 2
ROPE_THETA = 10000.0
Q_BLOCK = 128
D_FF = 5632
N_EXPERTS = 8
TOP_K = 2
EXPERT_FF = 5632
N_DENSE = (DEPTH + 1) // 2
N_MOE = DEPTH // 2
N_MOD = 6
EPS = 1e-6
SM_SCALE = (QK_NOPE + QK_ROPE) ** -0.5

kernel_name = "hybrid_mla_shortconv_prefix_diffusion_step"


def rmsnorm(x, g):
    xf = x.astype(jnp.float32)
    y = xf * lax.rsqrt(jnp.mean(xf * xf, axis=-1, keepdims=True) + EPS)
    return (y * g.astype(jnp.float32)).astype(x.dtype)


def modulate(x, g, shift, scale):
    return rmsnorm(x, g) * (1 + scale) + shift


def ada(cond, l, w_ada, b_ada):
    m = jax.nn.silu(cond) @ w_ada[l] + b_ada[l]
    return m.reshape(cond.shape[0], N_MOD, 1, D_MODEL)


def rope_tables(n, dtype):
    rows = n // GRID_W
    t = jnp.arange(rows * GRID_W)
    row = (t // GRID_W).astype(jnp.float32)
    col = (t % GRID_W).astype(jnp.float32)
    inv = 1.0 / (ROPE_THETA ** (jnp.arange(0, ROPE_AXIS, 2, dtype=jnp.float32) / ROPE_AXIS))
    ar = row[:, None] * inv
    ac = col[:, None] * inv
    ang = jnp.concatenate([ar, ar, ac, ac], axis=-1)
    return jnp.cos(ang).astype(dtype), jnp.sin(ang).astype(dtype)


def apply_rope_2d(x, cos, sin):
    xs = x.reshape(x.shape[:-1] + (2, 2, ROPE_AXIS // 2))
    rot = jnp.stack([-xs[..., 1, :], xs[..., 0, :]], axis=-2).reshape(x.shape)
    return x * cos + rot * sin


def attend(q, k, v):
    b, sq, h, dk = q.shape
    nb = sq // Q_BLOCK
    qb = q.reshape(b, nb, Q_BLOCK, h, dk).transpose(1, 0, 2, 3, 4)

    def block(qi):
        s = jnp.einsum('bqhd,bkd->bhqk', qi, k).astype(jnp.float32) * SM_SCALE
        p = jax.nn.softmax(s, axis=-1).astype(v.dtype)
        return jnp.einsum('bhqk,bkc->bqhc', p, v)

    o = lax.map(block, qb)
    return o.transpose(1, 0, 2, 3, 4).reshape(b, sq, h, v.shape[-1])


def short_conv(u, w):
    up = jnp.pad(u, ((0, 0), (1, 1), (0, 0)))
    return up[:, :-2] * w[0] + up[:, 1:-1] * w[1] + up[:, 2:] * w[2]


def project_inputs(h, l, w_in, g_q, g_kv, w_uq, w_uk):
    proj = h @ w_in[l]
    s1 = Q_LORA
    s2 = s1 + KV_LORA
    s3 = s2 + QK_ROPE
    s4 = s3 + CONV_DIM
    s5 = s4 + CONV_DIM
    cq, ckv, krope, gate_b, gate_c, u = jnp.split(proj, [s1, s2, s3, s4, s5], axis=-1)
    q = jnp.einsum('bnr,rhd->bnhd', rmsnorm(cq, g_q[l]), w_uq[l])
    q_nope, q_rope = q[..., :QK_NOPE], q[..., QK_NOPE:]
    q_lat = jnp.einsum('bnhd,chd->bnhc', q_nope, w_uk[l])
    ckv = rmsnorm(ckv, g_kv[l])
    return q_lat, q_rope, ckv, krope, gate_b, gate_c, u


def mixer_output(o_lat, gate_b, gate_c, u, l, w_uv, w_conv, w_o):
    b, n = o_lat.shape[0], o_lat.shape[1]
    attn = jnp.einsum('bnhc,chd->bnhd', o_lat, w_uv[l]).reshape(b, n, ATTN_WIDTH)
    conv = gate_b * short_conv(gate_c * u, w_conv[l])
    return jnp.concatenate([attn, conv], axis=-1) @ w_o[l]


def swiglu(h, wg, wu, wd):
    return (jax.nn.silu(h @ wg) * (h @ wu)) @ wd


def moe(h, w_router, wg, wu, wd):
    logits = (h @ w_router).astype(jnp.float32)
    topv, topi = lax.top_k(logits, TOP_K)
    probs = jax.nn.softmax(topv, axis=-1)
    combine = jnp.sum(jax.nn.one_hot(topi, N_EXPERTS, dtype=jnp.float32) * probs[..., None], axis=-2).astype(h.dtype)
    y = jnp.zeros_like(h)
    for e in range(N_EXPERTS):
        y = y + combine[..., e:e + 1] * swiglu(h, wg[e], wu[e], wd[e])
    return y


def ffn_block(x, mod, l, g_sand, w_ff_gate, w_ff_up, w_ff_down, w_router, w_e_gate, w_e_up, w_e_down):
    h = modulate(x, g_sand[l, 2], mod[:, 3], mod[:, 4])
    if l % 2 == 0:
        i = l // 2
        y = swiglu(h, w_ff_gate[i], w_ff_up[i], w_ff_down[i])
    else:
        i = l // 2
        y = moe(h, w_router[i], w_e_gate[i], w_e_up[i], w_e_down[i])
    return x + mod[:, 5] * rmsnorm(y, g_sand[l, 3])


def setup_inputs(seed: int = 0) -> dict:
    key = jax.random.key(seed)
    ks = jax.random.split(key, 32)
    f32 = jnp.float32

    def nrm(k, shape, scale):
        return jax.random.normal(k, shape, f32) * scale

    def gain(k, shape):
        return 1.0 + 0.05 * jax.random.normal(k, shape, f32)

    return {
        "x_prompt": nrm(ks[0], (BATCH, SEQ, D_MODEL), 1.0),
        "x_sample": nrm(ks[1], (DEC_BATCH, DEC_SEQ, D_MODEL), 1.0),
        "cache_ckv": nrm(ks[2], (DEC_BATCH, DEPTH, PAST_LEN, KV_LORA), 1.0),
        "cache_krope": nrm(ks[3], (DEC_BATCH, DEPTH, PAST_LEN, QK_ROPE), 1.0),
        "c": nrm(ks[4], (DEC_BATCH, D_MODEL), 1.0),
        "c_ctx": nrm(ks[5], (D_MODEL,), 1.0),
        "w_in": nrm(ks[6], (DEPTH, D_MODEL, IN_COLS), D_MODEL ** -0.5),
        "g_q": gain(ks[7], (DEPTH, Q_LORA)),
        "g_kv": gain(ks[8], (DEPTH, KV_LORA)),
        "w_uq": nrm(ks[9], (DEPTH, Q_LORA, N_HEADS, QK_NOPE + QK_ROPE), Q_LORA ** -0.5),
        "w_uk": nrm(ks[10], (DEPTH, KV_LORA, N_HEADS, QK_NOPE), KV_LORA ** -0.5),
        "w_uv": nrm(ks[11], (DEPTH, KV_LORA, N_HEADS, V_DIM), KV_LORA ** -0.5),
        "w_conv": nrm(ks[12], (DEPTH, CONV_W, CONV_DIM), 0.5),
        "w_o": nrm(ks[13], (DEPTH, D_MIX, D_MODEL), D_MIX ** -0.5),
        "w_ada": nrm(ks[14], (DEPTH, D_MODEL, N_MOD * D_MODEL), 0.5 * D_MODEL ** -0.5),
        "b_ada": nrm(ks[15], (DEPTH, N_MOD * D_MODEL), 0.02),
        "g_sand": gain(ks[16], (DEPTH, 4, D_MODEL)),
        "w_ff_gate": nrm(ks[17], (N_DENSE, D_MODEL, D_FF), D_MODEL ** -0.5),
        "w_ff_up": nrm(ks[18], (N_DENSE, D_MODEL, D_FF), D_MODEL ** -0.5),
        "w_ff_down": nrm(ks[19], (N_DENSE, D_FF, D_MODEL), D_FF ** -0.5),
        "w_router": nrm(ks[20], (N_MOE, D_MODEL, N_EXPERTS), D_MODEL ** -0.5),
        "w_e_gate": nrm(ks[21], (N_MOE, N_EXPERTS, D_MODEL, EXPERT_FF), D_MODEL ** -0.5),
        "w_e_up": nrm(ks[22], (N_MOE, N_EXPERTS, D_MODEL, EXPERT_FF), D_MODEL ** -0.5),
        "w_e_down": nrm(ks[23], (N_MOE, N_EXPERTS, EXPERT_FF, D_MODEL), EXPERT_FF ** -0.5),
    }


def reference(x_prompt, x_sample, cache_ckv, cache_krope, c, c_ctx, w_in, g_q, g_kv, w_uq, w_uk, w_uv,
              w_conv, w_o, w_ada, b_ada, g_sand, w_ff_gate, w_ff_up, w_ff_down, w_router,
              w_e_gate, w_e_up, w_e_down):
    cos, sin = rope_tables(x_sample.shape[1], x_sample.dtype)
    xp = x_prompt
    xs = x_sample
    ckv_list = []
    kr_list = []
    for l in range(DEPTH):
        mod_p = ada(c_ctx[None, :], l, w_ada, b_ada)
        mod_s = ada(c, l, w_ada, b_ada)

        h = modulate(xp, g_sand[l, 0], mod_p[:, 0], mod_p[:, 1])
        q_lat, q_rope, ckv, krope, gb, gc, u = project_inputs(h, l, w_in, g_q, g_kv, w_uq, w_uk)
        o = attend(jnp.concatenate([q_lat, q_rope], axis=-1),
                   jnp.concatenate([ckv, krope], axis=-1), ckv)
        xp = xp + mod_p[:, 2] * rmsnorm(mixer_output(o, gb, gc, u, l, w_uv, w_conv, w_o), g_sand[l, 1])
        xp = ffn_block(xp, mod_p, l, g_sand, w_ff_gate, w_ff_up, w_ff_down, w_router, w_e_gate, w_e_up, w_e_down)
        ckv_list.append(ckv)
        kr_list.append(krope)

        h = modulate(xs, g_sand[l, 0], mod_s[:, 0], mod_s[:, 1])
        q_lat, q_rope, ckv_s, krope_s, gb, gc, u = project_inputs(h, l, w_in, g_q, g_kv, w_uq, w_uk)
        q_rope = apply_rope_2d(q_rope, cos[:, None, :], sin[:, None, :])
        krope_s = apply_rope_2d(krope_s, cos, sin)
        ctx_ckv = cache_ckv[:, l]
        ctx_kr = cache_krope[:, l]
        keys = jnp.concatenate([jnp.concatenate([ctx_ckv, ctx_kr], axis=-1),
                                jnp.concatenate([ckv_s, krope_s], axis=-1)], axis=1)
        vals = jnp.concatenate([ctx_ckv, ckv_s], axis=1)
        o = attend(jnp.concatenate([q_lat, q_rope], axis=-1), keys, vals)
        xs = xs + mod_s[:, 2] * rmsnorm(mixer_output(o, gb, gc, u, l, w_uv, w_conv, w_o), g_sand[l, 1])
        xs = ffn_block(xs, mod_s, l, g_sand, w_ff_gate, w_ff_up, w_ff_down, w_router, w_e_gate, w_e_up, w_e_down)

    new_ckv = jnp.stack(ckv_list, axis=1)
    new_krope = jnp.stack(kr_list, axis=1)
    return (xp, xs, new_ckv, new_krope)
```

```python
import functools

import jax
import jax.numpy as jnp
from jax import lax
from jax.experimental import pallas as pl
from jax.experimental.pallas import tpu as pltpu

GRID_W = 64
ROPE_THETA = 10000.0
EPS = 1e-6
N_MOD = 6
TOP_K = 2

LANES = 128
SUBLANES = 8
TM = 512
VMEM_LIMIT = 56 * 1024 * 1024

BF16 = jnp.bfloat16
F32 = jnp.float32


def _params(sem, vmem=VMEM_LIMIT):
    return pltpu.CompilerParams(dimension_semantics=sem, vmem_limit_bytes=vmem)


def _rms(x):
    return x * lax.rsqrt(jnp.mean(x * x, axis=-1, keepdims=True) + EPS)


def _rope(x, cos, sa, sb):
    return x * cos + pltpu.roll(x, LANES - 16, 1) * sa + pltpu.roll(x, 16, 1) * sb


def _ada_kernel(c_ref, w_ref, b_ref, o_ref):
    c = c_ref[...]
    s = c / (1.0 + jnp.exp(-c))
    o_ref[...] = jnp.dot(s.astype(BF16), w_ref[...].astype(BF16),
                         preferred_element_type=F32) + b_ref[...]


def _ada(cond8, w_ada, b_ada):
    L, D, N6 = w_ada.shape
    tn = 1024 if N6 % 1024 == 0 else N6
    return pl.pallas_call(
        _ada_kernel,
        grid=(L, N6 // tn),
        in_specs=[
            pl.BlockSpec((SUBLANES, D), lambda l, n: (0, 0)),
            pl.BlockSpec((None, D, tn), lambda l, n: (l, 0, n)),
            pl.BlockSpec((None, 1, tn), lambda l, n: (l, 0, n)),
        ],
        out_specs=pl.BlockSpec((None, SUBLANES, tn), lambda l, n: (l, 0, n)),
        out_shape=jax.ShapeDtypeStruct((L, SUBLANES, N6), F32),
        compiler_params=_params(("arbitrary", "arbitrary")),
        name="ada_mod",
    )(cond8, w_ada, b_ada.reshape(L, 1, N6))


def _inproj_kernel(x_ref, mod_ref, g_ref, gq_ref, gkv_ref, cos_ref, sa_ref, sb_ref,
                   wa_ref, wb_ref, cqn_ref, ckv_ref, kr_ref, kcat_ref, gates_ref,
                   hb_ref, *, QL, KV, ROPE):
    n = pl.program_id(1)

    @pl.when(n == 0)
    def _():
        h = _rms(x_ref[...]) * g_ref[...]
        h = h * (1.0 + mod_ref[1:2, :]) + mod_ref[0:1, :]
        hb_ref[...] = h.astype(BF16)
        acc = jnp.dot(hb_ref[...], wa_ref[...], preferred_element_type=F32)
        cqn_ref[...] = (_rms(acc[:, :QL]) * gq_ref[...]).astype(BF16)
        ckv = _rms(acc[:, QL:QL + KV]) * gkv_ref[...]
        ckv_ref[...] = ckv
        kr = _rope(acc[:, QL + KV:QL + KV + LANES], cos_ref[...], sa_ref[...], sb_ref[...])
        kr_ref[...] = kr[:, :ROPE]
        kcat_ref[:, :KV] = ckv.astype(BF16)
        kcat_ref[:, KV:] = kr.astype(BF16)

    @pl.when(n > 0)
    def _():
        gates_ref[...] = jnp.dot(hb_ref[...], wb_ref[...], preferred_element_type=F32)


def _q_kernel(cqn_ref, wq_ref, wk_ref, cos_ref, sa_ref, sb_ref, o_ref, *, NOPE, KV, scale):
    q = jnp.dot(cqn_ref[...], wq_ref[...], preferred_element_type=F32)
    q_lat = jnp.dot(q[:, :NOPE].astype(BF16), wk_ref[...], preferred_element_type=F32)
    q_rope = _rope(q[:, NOPE:], cos_ref[...], sa_ref[...], sb_ref[...])
    o_ref[:, :KV] = (q_lat * scale).astype(BF16)
    o_ref[:, KV:] = (q_rope * scale).astype(BF16)


def _attn_kernel(*refs, H, KV, has_cache):
    if has_cache:
        q_ref, kn_ref, kc_ref, o_ref = refs
    else:
        q_ref, kn_ref, o_ref = refs
    DK = KV + LANES
    dn = (((1,), (1,)), ((), ()))
    kn = kn_ref[...]
    vn = kn[:, :KV]
    if has_cache:
        kc = kc_ref[...]
        vc = kc[:, :KV]
    for h in range(H):
        q = q_ref[:, h * DK:(h + 1) * DK]
        s = lax.dot_general(q, kn, dn, preferred_element_type=F32)
        m = jnp.max(s, axis=-1, keepdims=True)
        if has_cache:
            sc = lax.dot_general(q, kc, dn, preferred_element_type=F32)
            m = jnp.maximum(m, jnp.max(sc, axis=-1, keepdims=True))
        p = jnp.exp(s - m)
        l = jnp.sum(p, axis=-1, keepdims=True)
        o = jnp.dot(p.astype(BF16), vn, preferred_element_type=F32)
        if has_cache:
            pc = jnp.exp(sc - m)
            l = l + jnp.sum(pc, axis=-1, keepdims=True)
            o = o + jnp.dot(pc.astype(BF16), vc, preferred_element_type=F32)
        o_ref[:, h * KV:(h + 1) * KV] = (o / l).astype(BF16)


def _mix_kernel(o_ref, gb_ref, gc_ref, u_ref, gcp_ref, up_ref, gcn_ref, un_ref,
                wuv_ref, wc_ref, wo_ref, g_ref, mod_ref, x_ref, out_ref,
                cat_ref, vs_ref, *, H, KV, VD, n_p_tiles, seq_p, seq_s, t_p):
    m = pl.program_id(0)
    tm = x_ref.shape[0]
    AW = H * VD
    for h in range(H):
        a = jnp.dot(o_ref[:, h * KV:(h + 1) * KV], wuv_ref[h], preferred_element_type=F32)
        cat_ref[:, h * VD:(h + 1) * VD] = a.astype(BF16)

    v = gc_ref[...] * u_ref[...]
    vs_ref[SUBLANES:SUBLANES + tm, :] = v
    vs_ref[SUBLANES - 1:SUBLANES, :] = gcp_ref[SUBLANES - 1:SUBLANES, :] * up_ref[SUBLANES - 1:SUBLANES, :]
    vs_ref[SUBLANES + tm:SUBLANES + tm + 1, :] = gcn_ref[0:1, :] * un_ref[0:1, :]
    is_p = m < n_p_tiles
    seq = jnp.where(is_p, seq_p, seq_s)
    row = lax.broadcasted_iota(jnp.int32, (tm, 1), 0) + (m * tm - jnp.where(is_p, 0, t_p))
    pos = jnp.bitwise_and(row, seq - 1)
    vp = jnp.where(pos == 0, 0.0, vs_ref[SUBLANES - 1:SUBLANES - 1 + tm, :])
    vn = jnp.where(pos == seq - 1, 0.0, vs_ref[SUBLANES + 1:SUBLANES + 1 + tm, :])
    conv = vp * wc_ref[0:1, :] + v * wc_ref[1:2, :] + vn * wc_ref[2:3, :]
    cat_ref[:, AW:] = (gb_ref[...] * conv).astype(BF16)

    y = jnp.dot(cat_ref[...], wo_ref[...], preferred_element_type=F32)
    out_ref[...] = x_ref[...] + mod_ref[2:3, :] * (_rms(y) * g_ref[...])


def _modulate_ffn(x_ref, mod_ref, g_ref):
    h = _rms(x_ref[...]) * g_ref[...]
    return h * (1.0 + mod_ref[4:5, :]) + mod_ref[3:4, :]


def _swiglu_step(hb, wg_ref, wu_ref, wd_ref):
    g = jnp.dot(hb, wg_ref[...], preferred_element_type=F32)
    u = jnp.dot(hb, wu_ref[...], preferred_element_type=F32)
    a = (g / (1.0 + jnp.exp(-g))) * u
    return jnp.dot(a.astype(BF16), wd_ref[...], preferred_element_type=F32)


def _ffn_kernel(x_ref, mod_ref, g2_ref, g3_ref, wg_ref, wu_ref, wd_ref, out_ref,
                hb_ref, acc_ref):
    f = pl.program_id(1)
    nf = pl.num_programs(1)

    @pl.when(f == 0)
    def _():
        hb_ref[...] = _modulate_ffn(x_ref, mod_ref, g2_ref).astype(BF16)

    d = _swiglu_step(hb_ref[...], wg_ref, wu_ref, wd_ref)

    @pl.when(f == 0)
    def _():
        acc_ref[...] = d

    @pl.when(f > 0)
    def _():
        acc_ref[...] += d

    @pl.when(f == nf - 1)
    def _():
        out_ref[...] = x_ref[...] + mod_ref[5:6, :] * (_rms(acc_ref[...]) * g3_ref[...])


def _top2_weights(logits, n_experts):
    lane = lax.broadcasted_iota(jnp.int32, logits.shape, 1)
    neg = jnp.float32(-jnp.inf)
    l1 = jnp.where(lane < n_experts, logits, neg)
    m1 = jnp.max(l1, axis=-1, keepdims=True)
    i1 = jnp.min(jnp.where(l1 == m1, lane, LANES), axis=-1, keepdims=True)
    l2 = jnp.where(lane == i1, neg, l1)
    m2 = jnp.max(l2, axis=-1, keepdims=True)
    i2 = jnp.min(jnp.where(l2 == m2, lane, LANES), axis=-1, keepdims=True)
    e2 = jnp.exp(m2 - m1)
    p1 = 1.0 / (1.0 + e2)
    p2 = e2 / (1.0 + e2)
    return jnp.where(lane == i1, p1, 0.0) + jnp.where(lane == i2, p2, 0.0)


def _moe_dense_kernel(x_ref, mod_ref, g2_ref, g3_ref, wr_ref, wg_ref, wu_ref, wd_ref, out_ref,
                      hb_ref, cw_ref, acc_ref, y_ref, *, n_experts):
    e = pl.program_id(1)
    f = pl.program_id(2)
    nf = pl.num_programs(2)

    @pl.when(jnp.logical_and(e == 0, f == 0))
    def _():
        h = _modulate_ffn(x_ref, mod_ref, g2_ref)
        hb_ref[...] = h.astype(BF16)
        logits = jnp.dot(h, wr_ref[...], preferred_element_type=F32,
                         precision=lax.Precision.HIGHEST)
        cw_ref[...] = _top2_weights(logits, n_experts)

    d = _swiglu_step(hb_ref[...], wg_ref, wu_ref, wd_ref)

    @pl.when(f == 0)
    def _():
        acc_ref[...] = d

    @pl.when(f > 0)
    def _():
        acc_ref[...] += d

    @pl.when(f == nf - 1)
    def _():
        lane = lax.broadcasted_iota(jnp.int32, cw_ref.shape, 1)
        cw = jnp.sum(jnp.where(lane == e, cw_ref[...], 0.0), axis=-1, keepdims=True)
        contrib = cw * acc_ref[...]

        @pl.when(e == 0)
        def _():
            y_ref[...] = contrib

        @pl.when(e > 0)
        def _():
            y_ref[...] += contrib

    @pl.when(jnp.logical_and(e == n_experts - 1, f == nf - 1))
    def _():
        out_ref[...] = x_ref[...] + mod_ref[5:6, :] * (_rms(y_ref[...]) * g3_ref[...])


def kernel(x_prompt, x_sample, cache_ckv, cache_krope, c, c_ctx, w_in, g_q, g_kv, w_uq, w_uk, w_uv,
           w_conv, w_o, w_ada, b_ada, g_sand, w_ff_gate, w_ff_up, w_ff_down, w_router,
           w_e_gate, w_e_up, w_e_down):
    B, N, D = x_prompt.shape
    DB, DN, _ = x_sample.shape
    L = w_in.shape[0]
    QL, KV = g_q.shape[1], g_kv.shape[1]
    H, NOPE = w_uk.shape[2], w_uk.shape[3]
    ROPE = w_uq.shape[3] - NOPE
    VD = w_uv.shape[3]
    CONV = w_conv.shape[2]
    AW = H * VD
    PAST = cache_ckv.shape[2]
    FF = w_ff_gate.shape[2]
    NE, EFF = w_e_gate.shape[1], w_e_gate.shape[3]
    T_P, T_S = B * N, DB * DN
    T = T_P + T_S
    DK = KV + LANES
    NA = 1024
    tm = TM
    assert ROPE == 64 and 2 * ROPE == LANES and NOPE % LANES == 0 and KV % LANES == 0 and QL % LANES == 0
    assert QL + KV + LANES <= NA and CONV % LANES == 0 and AW + CONV == w_o.shape[1]
    assert T_P % tm == 0 and DN % tm == 0 and tm % N == 0 and T_P % DN == 0
    assert N & (N - 1) == 0 and DN & (DN - 1) == 0 and DN % GRID_W == 0
    assert NE <= LANES and DB + 1 <= SUBLANES
    n_p_tiles = T_P // tm
    tiles_per_s = DN // tm
    n_tiles = T // tm
    scale = float(NOPE + ROPE) ** -0.5
    tf = 512
    assert FF % tf == 0 and EFF % tf == 0

    def mrow(m):
        return jnp.where(m < n_p_tiles, 0, 1 + (m - n_p_tiles) // tiles_per_s)

    def trow(m):
        return jnp.where(m < n_p_tiles, 0, 1 + (m - n_p_tiles) % tiles_per_s)

    t = jnp.arange(DN)
    inv = 1.0 / (ROPE_THETA ** (jnp.arange(0, ROPE // 2, 2, dtype=F32) / (ROPE // 2)))
    ar = (t // GRID_W).astype(F32)[:, None] * inv
    ac = (t % GRID_W).astype(F32)[:, None] * inv
    ang = jnp.concatenate([ar, ar, ac, ac] * 2, axis=-1)
    first = (jnp.arange(LANES) % (ROPE // 2)) < (ROPE // 4)
    cos_t = jnp.concatenate([jnp.ones((tm, LANES), F32), jnp.cos(ang)], axis=0)
    sin = jnp.sin(ang)
    sa_t = jnp.concatenate([jnp.zeros((tm, LANES), F32), jnp.where(first, -sin, 0.0)], axis=0)
    sb_t = jnp.concatenate([jnp.zeros((tm, LANES), F32), jnp.where(first, 0.0, sin)], axis=0)

    s1, s2 = QL + KV + ROPE, QL + KV + ROPE + 3 * CONV
    w_a = jnp.pad(w_in[:, :, :s1], ((0, 0), (0, 0), (0, NA - s1))).astype(BF16)
    w_b = w_in[:, :, s1:s2].astype(BF16)
    wq = jnp.pad(w_uq.transpose(0, 2, 1, 3), ((0, 0), (0, 0), (0, 0), (0, LANES - ROPE))).astype(BF16)
    wk = w_uk.transpose(0, 2, 3, 1).astype(BF16)
    wuv = w_uv.transpose(0, 2, 1, 3).astype(BF16)
    wo = w_o.astype(BF16)
    wfg, wfu, wfd = w_ff_gate.astype(BF16), w_ff_up.astype(BF16), w_ff_down.astype(BF16)
    weg, weu, wed = w_e_gate.astype(BF16), w_e_up.astype(BF16), w_e_down.astype(BF16)
    wr = jnp.pad(w_router, ((0, 0), (0, 0), (0, LANES - NE)))
    kcache = jnp.concatenate(
        [cache_ckv, cache_krope, jnp.zeros(cache_krope.shape[:-1] + (LANES - ROPE,), F32)],
        axis=-1).astype(BF16)

    cond8 = jnp.concatenate([c_ctx[None, :], c, jnp.zeros((SUBLANES - 1 - DB, D), F32)], axis=0)
    mods = _ada(cond8, w_ada, b_ada)[:, :1 + DB].reshape(L, 1 + DB, N_MOD, D)

    x = jnp.concatenate([x_prompt.reshape(T_P, D), x_sample.reshape(T_S, D)], axis=0)

    mod_spec2 = lambda l: pl.BlockSpec((None, None, N_MOD, D), lambda m, n: (l, mrow(m), 0, 0))
    row_spec2 = lambda w: pl.BlockSpec((tm, w), lambda m, n: (m, 0))
    tab_spec2 = pl.BlockSpec((tm, LANES), lambda m, n: (trow(m), 0))

    ckv_list, kr_list = [], []
    for l in range(L):
        cqn, ckvn, kr, kcat, gates = pl.pallas_call(
            functools.partial(_inproj_kernel, QL=QL, KV=KV, ROPE=ROPE),
            grid=(n_tiles, 1 + 3),
            in_specs=[
                row_spec2(D),
                mod_spec2(l),
                pl.BlockSpec((None, 1, D), lambda m, n: (4 * l, 0, 0)),
                pl.BlockSpec((None, 1, QL), lambda m, n: (l, 0, 0)),
                pl.BlockSpec((None, 1, KV), lambda m, n: (l, 0, 0)),
                tab_spec2, tab_spec2, tab_spec2,
                pl.BlockSpec((None, D, NA), lambda m, n: (l, 0, 0)),
                pl.BlockSpec((None, D, CONV), lambda m, n: (l, 0, jnp.maximum(n - 1, 0))),
            ],
            out_specs=[
                row_spec2(QL), row_spec2(KV), row_spec2(ROPE), row_spec2(DK),
                pl.BlockSpec((tm, CONV), lambda m, n: (m, jnp.maximum(n - 1, 0))),
            ],
            out_shape=[
                jax.ShapeDtypeStruct((T, QL), BF16),
                jax.ShapeDtypeStruct((T, KV), F32),
                jax.ShapeDtypeStruct((T, ROPE), F32),
                jax.ShapeDtypeStruct((T, DK), BF16),
                jax.ShapeDtypeStruct((T, 3 * CONV), F32),
            ],
            scratch_shapes=[pltpu.VMEM((tm, D), BF16)],
            compiler_params=_params(("arbitrary", "arbitrary")),
            name="in_proj",
        )(x, mods, g_sand.reshape(4 * L, 1, D), g_q.reshape(L, 1, QL), g_kv.reshape(L, 1, KV),
          cos_t, sa_t, sb_t, w_a, w_b)
        ckv_list.append(ckvn[:T_P].reshape(B, N, KV))
        kr_list.append(kr[:T_P].reshape(B, N, ROPE))

        qcat = pl.pallas_call(
            functools.partial(_q_kernel, NOPE=NOPE, KV=KV, scale=scale),
            grid=(n_tiles, H),
            in_specs=[
                pl.BlockSpec((tm, QL), lambda m, h: (m, 0)),
                pl.BlockSpec((None, None, QL, NOPE + LANES), lambda m, h: (l, h, 0, 0)),
                pl.BlockSpec((None, None, NOPE, KV), lambda m, h: (l, h, 0, 0)),
                tab_spec2, tab_spec2, tab_spec2,
            ],
            out_specs=pl.BlockSpec((tm, DK), lambda m, h: (m, h)),
            out_shape=jax.ShapeDtypeStruct((T, H * DK), BF16),
            compiler_params=_params(("arbitrary", "arbitrary")),
            name="q_proj",
        )(cqn, wq, wk, cos_t, sa_t, sb_t)

        o_p = pl.pallas_call(
            functools.partial(_attn_kernel, H=H, KV=KV, has_cache=False),
            grid=(B,),
            in_specs=[
                pl.BlockSpec((N, H * DK), lambda b: (b, 0)),
                pl.BlockSpec((N, DK), lambda b: (b, 0)),
            ],
            out_specs=pl.BlockSpec((N, H * KV), lambda b: (b, 0)),
            out_shape=jax.ShapeDtypeStruct((T_P, H * KV), BF16),
            compiler_params=_params(("arbitrary",)),
            name="attn_prompt",
        )(qcat, kcat)
        tq = 256
        qs_per = DN // tq
        o_s = pl.pallas_call(
            functools.partial(_attn_kernel, H=H, KV=KV, has_cache=True),
            grid=(DB, qs_per),
            in_specs=[
                pl.BlockSpec((tq, H * DK), lambda b, i: (T_P // tq + b * qs_per + i, 0)),
                pl.BlockSpec((DN, DK), lambda b, i: (T_P // DN + b, 0)),
                pl.BlockSpec((None, None, PAST, DK), lambda b, i: (b, l, 0, 0)),
            ],
            out_specs=pl.BlockSpec((tq, H * KV), lambda b, i: (b * qs_per + i, 0)),
            out_shape=jax.ShapeDtypeStruct((T_S, H * KV), BF16),
            compiler_params=_params(("arbitrary", "arbitrary")),
            name="attn_sample",
        )(qcat, kcat, kcache)
        o_lat = jnp.concatenate([o_p, o_s], axis=0)

        r8 = tm // SUBLANES
        prev8 = lambda m: jnp.maximum(m * r8 - 1, 0)
        next8 = lambda m: jnp.minimum((m + 1) * r8, T // SUBLANES - 1)
        x = pl.pallas_call(
            functools.partial(_mix_kernel, H=H, KV=KV, VD=VD, n_p_tiles=n_p_tiles,
                              seq_p=N, seq_s=DN, t_p=T_P),
            grid=(n_tiles,),
            in_specs=[
                pl.BlockSpec((tm, H * KV), lambda m: (m, 0)),
                pl.BlockSpec((tm, CONV), lambda m: (m, 0)),
                pl.BlockSpec((tm, CONV), lambda m: (m, 1)),
                pl.BlockSpec((tm, CONV), lambda m: (m, 2)),
                pl.BlockSpec((SUBLANES, CONV), lambda m: (prev8(m), 1)),
                pl.BlockSpec((SUBLANES, CONV), lambda m: (prev8(m), 2)),
                pl.BlockSpec((SUBLANES, CONV), lambda m: (next8(m), 1)),
                pl.BlockSpec((SUBLANES, CONV), lambda m: (next8(m), 2)),
                pl.BlockSpec((None, H, KV, VD), lambda m: (l, 0, 0, 0)),
                pl.BlockSpec((None, 3, CONV), lambda m: (l, 0, 0)),
                pl.BlockSpec((None, AW + CONV, D), lambda m: (l, 0, 0)),
                pl.BlockSpec((None, 1, D), lambda m: (4 * l + 1, 0, 0)),
                pl.BlockSpec((None, None, N_MOD, D), lambda m: (l, mrow(m), 0, 0)),
                pl.BlockSpec((tm, D), lambda m: (m, 0)),
            ],
            out_specs=pl.BlockSpec((tm, D), lambda m: (m, 0)),
            out_shape=jax.ShapeDtypeStruct((T, D), F32),
            scratch_shapes=[pltpu.VMEM((tm, AW + CONV), BF16),
                            pltpu.VMEM((tm + 2 * SUBLANES, CONV), F32)],
            compiler_params=_params(("arbitrary",)),
            name="mixer_out",
        )(o_lat, gates, gates, gates, gates, gates, gates, gates, wuv, w_conv, wo,
          g_sand.reshape(4 * L, 1, D), mods, x)

        i = l // 2
        if l % 2 == 0:
            x = pl.pallas_call(
                _ffn_kernel,
                grid=(n_tiles, FF // tf),
                in_specs=[
                    pl.BlockSpec((tm, D), lambda m, f: (m, 0)),
                    pl.BlockSpec((None, None, N_MOD, D), lambda m, f: (l, mrow(m), 0, 0)),
                    pl.BlockSpec((None, 1, D), lambda m, f: (4 * l + 2, 0, 0)),
                    pl.BlockSpec((None, 1, D), lambda m, f: (4 * l + 3, 0, 0)),
                    pl.BlockSpec((None, D, tf), lambda m, f: (i, 0, f)),
                    pl.BlockSpec((None, D, tf), lambda m, f: (i, 0, f)),
                    pl.BlockSpec((None, tf, D), lambda m, f: (i, f, 0)),
                ],
                out_specs=pl.BlockSpec((tm, D), lambda m, f: (m, 0)),
                out_shape=jax.ShapeDtypeStruct((T, D), F32),
                scratch_shapes=[pltpu.VMEM((tm, D), BF16), pltpu.VMEM((tm, D), F32)],
                compiler_params=_params(("arbitrary", "arbitrary")),
                name="ffn_dense",
            )(x, mods, g_sand.reshape(4 * L, 1, D), g_sand.reshape(4 * L, 1, D), wfg, wfu, wfd)
        else:
            x = pl.pallas_call(
                functools.partial(_moe_dense_kernel, n_experts=NE),
                grid=(n_tiles, NE, EFF // tf),
                in_specs=[
                    pl.BlockSpec((tm, D), lambda m, e, f: (m, 0)),
                    pl.BlockSpec((None, None, N_MOD, D), lambda m, e, f: (l, mrow(m), 0, 0)),
                    pl.BlockSpec((None, 1, D), lambda m, e, f: (4 * l + 2, 0, 0)),
                    pl.BlockSpec((None, 1, D), lambda m, e, f: (4 * l + 3, 0, 0)),
                    pl.BlockSpec((None, D, LANES), lambda m, e, f: (i, 0, 0)),
                    pl.BlockSpec((None, None, D, tf), lambda m, e, f: (i, e, 0, f)),
                    pl.BlockSpec((None, None, D, tf), lambda m, e, f: (i, e, 0, f)),
                    pl.BlockSpec((None, None, tf, D), lambda m, e, f: (i, e, f, 0)),
                ],
                out_specs=pl.BlockSpec((tm, D), lambda m, e, f: (m, 0)),
                out_shape=jax.ShapeDtypeStruct((T, D), F32),
                scratch_shapes=[pltpu.VMEM((tm, D), BF16), pltpu.VMEM((tm, LANES), F32),
                                pltpu.VMEM((tm, D), F32), pltpu.VMEM((tm, D), F32)],
                compiler_params=_params(("arbitrary", "arbitrary", "arbitrary")),
                name="moe_dense",
            )(x, mods, g_sand.reshape(4 * L, 1, D), g_sand.reshape(4 * L, 1, D), wr, weg, weu, wed)

    y_prompt = x[:T_P].reshape(B, N, D)
    y_sample = x[T_P:].reshape(DB, DN, D)
    new_ckv = jnp.stack(ckv_list, axis=1)
    new_krope = jnp.stack(kr_list, axis=1)
    return (y_prompt, y_sample, new_ckv, new_krope)
```

```python
import functools

import jax
import jax.numpy as jnp
from jax import lax
from jax.experimental import pallas as pl
from jax.experimental.pallas import tpu as pltpu

GRID_W = 64
ROPE_THETA = 10000.0
EPS = 1e-6
N_MOD = 6
TOP_K = 2

LANES = 128
SUBLANES = 8
TM = 512
VMEM_LIMIT = 56 * 1024 * 1024

BF16 = jnp.bfloat16
F32 = jnp.float32


def _params(sem, vmem=VMEM_LIMIT):
    return pltpu.CompilerParams(dimension_semantics=sem, vmem_limit_bytes=vmem)


def _rms(x):
    return x * lax.rsqrt(jnp.mean(x * x, axis=-1, keepdims=True) + EPS)


def _rope(x, cos, sa, sb):
    return x * cos + pltpu.roll(x, LANES - 16, 1) * sa + pltpu.roll(x, 16, 1) * sb


def _ada_kernel(c_ref, w_ref, b_ref, o_ref):
    c = c_ref[...]
    s = c / (1.0 + jnp.exp(-c))
    o_ref[...] = jnp.dot(s.astype(BF16), w_ref[...].astype(BF16),
                         preferred_element_type=F32) + b_ref[...]


def _ada(cond8, w_ada, b_ada):
    L, D, N6 = w_ada.shape
    tn = 1024 if N6 % 1024 == 0 else N6
    return pl.pallas_call(
        _ada_kernel,
        grid=(L, N6 // tn),
        in_specs=[
            pl.BlockSpec((SUBLANES, D), lambda l, n: (0, 0)),
            pl.BlockSpec((None, D, tn), lambda l, n: (l, 0, n)),
            pl.BlockSpec((None, 1, tn), lambda l, n: (l, 0, n)),
        ],
        out_specs=pl.BlockSpec((None, SUBLANES, tn), lambda l, n: (l, 0, n)),
        out_shape=jax.ShapeDtypeStruct((L, SUBLANES, N6), F32),
        compiler_params=_params(("arbitrary", "arbitrary")),
        name="ada_mod",
    )(cond8, w_ada, b_ada.reshape(L, 1, N6))


def _inproj_kernel(x_ref, mod_ref, g_ref, gq_ref, gkv_ref, cos_ref, sa_ref, sb_ref,
                   wa_ref, wb_ref, cqn_ref, ckv_ref, kr_ref, kcat_ref, gates_ref,
                   hb_ref, *, QL, KV, ROPE):
    n = pl.program_id(1)

    @pl.when(n == 0)
    def _():
        h = _rms(x_ref[...]) * g_ref[...]
        h = h * (1.0 + mod_ref[1:2, :]) + mod_ref[0:1, :]
        hb_ref[...] = h.astype(BF16)
        acc = jnp.dot(hb_ref[...], wa_ref[...], preferred_element_type=F32)
        cqn_ref[...] = (_rms(acc[:, :QL]) * gq_ref[...]).astype(BF16)
        ckv = _rms(acc[:, QL:QL + KV]) * gkv_ref[...]
        ckv_ref[...] = ckv
        kr = _rope(acc[:, QL + KV:QL + KV + LANES], cos_ref[...], sa_ref[...], sb_ref[...])
        kr_ref[...] = kr[:, :ROPE]
        kcat_ref[:, :KV] = ckv.astype(BF16)
        kcat_ref[:, KV:] = kr.astype(BF16)

    @pl.when(n > 0)
    def _():
        gates_ref[...] = jnp.dot(hb_ref[...], wb_ref[...], preferred_element_type=F32)


def _q_kernel(cqn_ref, wq_ref, wk_ref, cos_ref, sa_ref, sb_ref, o_ref, *, H, NOPE, KV, scale):
    QW = NOPE + LANES
    DK = KV + LANES
    cqn = cqn_ref[...]
    for h in range(H):
        q = jnp.dot(cqn, wq_ref[:, h * QW:(h + 1) * QW], preferred_element_type=F32)
        q_lat = jnp.dot(q[:, :NOPE].astype(BF16), wk_ref[h], preferred_element_type=F32)
        q_rope = _rope(q[:, NOPE:], cos_ref[...], sa_ref[...], sb_ref[...])
        o_ref[:, h * DK:h * DK + KV] = (q_lat * scale).astype(BF16)
        o_ref[:, h * DK + KV:(h + 1) * DK] = (q_rope * scale).astype(BF16)


def _attn_kernel(*refs, H, KV, has_cache):
    if has_cache:
        q_ref, kn_ref, kc_ref, o_ref = refs
    else:
        q_ref, kn_ref, o_ref = refs
    DK = KV + LANES
    dn = (((1,), (1,)), ((), ()))
    kn = kn_ref[...]
    vn = kn[:, :KV]
    if has_cache:
        kc = kc_ref[...]
        vc = kc[:, :KV]
    for h in range(H):
        q = q_ref[:, h * DK:(h + 1) * DK]
        s = lax.dot_general(q, kn, dn, preferred_element_type=F32)
        m = jnp.max(s, axis=-1, keepdims=True)
        if has_cache:
            sc = lax.dot_general(q, kc, dn, preferred_element_type=F32)
            m = jnp.maximum(m, jnp.max(sc, axis=-1, keepdims=True))
        p = jnp.exp(s - m)
        l = jnp.sum(p, axis=-1, keepdims=True)
        o = jnp.dot(p.astype(BF16), vn, preferred_element_type=F32)
        if has_cache:
            pc = jnp.exp(sc - m)
            l = l + jnp.sum(pc, axis=-1, keepdims=True)
            o = o + jnp.dot(pc.astype(BF16), vc, preferred_element_type=F32)
        o_ref[:, h * KV:(h + 1) * KV] = (o / l).astype(BF16)


def _mix_kernel(o_ref, gb_ref, gc_ref, u_ref, gcp_ref, up_ref, gcn_ref, un_ref,
                wuv_ref, wc_ref, wo_ref, g_ref, mod_ref, x_ref, out_ref,
                cat_ref, vs_ref, *, H, KV, VD, n_p_tiles, seq_p, seq_s, t_p):
    m = pl.program_id(0)
    tm = x_ref.shape[0]
    AW = H * VD
    for h in range(H):
        a = jnp.dot(o_ref[:, h * KV:(h + 1) * KV], wuv_ref[h], preferred_element_type=F32)
        cat_ref[:, h * VD:(h + 1) * VD] = a.astype(BF16)

    v = gc_ref[...] * u_ref[...]
    vs_ref[SUBLANES:SUBLANES + tm, :] = v
    vs_ref[SUBLANES - 1:SUBLANES, :] = gcp_ref[SUBLANES - 1:SUBLANES, :] * up_ref[SUBLANES - 1:SUBLANES, :]
    vs_ref[SUBLANES + tm:SUBLANES + tm + 1, :] = gcn_ref[0:1, :] * un_ref[0:1, :]
    is_p = m < n_p_tiles
    seq = jnp.where(is_p, seq_p, seq_s)
    row = lax.broadcasted_iota(jnp.int32, (tm, 1), 0) + (m * tm - jnp.where(is_p, 0, t_p))
    pos = jnp.bitwise_and(row, seq - 1)
    vp = jnp.where(pos == 0, 0.0, vs_ref[SUBLANES - 1:SUBLANES - 1 + tm, :])
    vn = jnp.where(pos == seq - 1, 0.0, vs_ref[SUBLANES + 1:SUBLANES + 1 + tm, :])
    conv = vp * wc_ref[0:1, :] + v * wc_ref[1:2, :] + vn * wc_ref[2:3, :]
    cat_ref[:, AW:] = (gb_ref[...] * conv).astype(BF16)

    y = jnp.dot(cat_ref[...], wo_ref[...], preferred_element_type=F32)
    out_ref[...] = x_ref[...] + mod_ref[2:3, :] * (_rms(y) * g_ref[...])


def _modulate_ffn(x_ref, mod_ref, g_ref):
    h = _rms(x_ref[...]) * g_ref[...]
    return h * (1.0 + mod_ref[4:5, :]) + mod_ref[3:4, :]


def _swiglu_step(hb, wg_ref, wu_ref, wd_ref):
    g = jnp.dot(hb, wg_ref[...], preferred_element_type=F32)
    u = jnp.dot(hb, wu_ref[...], preferred_element_type=F32)
    a = (g / (1.0 + jnp.exp(-g))) * u
    return jnp.dot(a.astype(BF16), wd_ref[...], preferred_element_type=F32)


def _ffn_kernel(x_ref, mod_ref, g2_ref, g3_ref, wg_ref, wu_ref, wd_ref, out_ref,
                hb_ref, acc_ref):
    f = pl.program_id(1)
    nf = pl.num_programs(1)

    @pl.when(f == 0)
    def _():
        hb_ref[...] = _modulate_ffn(x_ref, mod_ref, g2_ref).astype(BF16)

    d = _swiglu_step(hb_ref[...], wg_ref, wu_ref, wd_ref)

    @pl.when(f == 0)
    def _():
        acc_ref[...] = d

    @pl.when(f > 0)
    def _():
        acc_ref[...] += d

    @pl.when(f == nf - 1)
    def _():
        out_ref[...] = x_ref[...] + mod_ref[5:6, :] * (_rms(acc_ref[...]) * g3_ref[...])


def _top2(logits, n_experts):
    lane = lax.broadcasted_iota(jnp.int32, logits.shape, 1)
    neg = jnp.float32(-jnp.inf)
    l1 = jnp.where(lane < n_experts, logits, neg)
    m1 = jnp.max(l1, axis=-1, keepdims=True)
    i1 = jnp.min(jnp.where(l1 == m1, lane, LANES), axis=-1, keepdims=True)
    l2 = jnp.where(lane == i1, neg, l1)
    m2 = jnp.max(l2, axis=-1, keepdims=True)
    i2 = jnp.min(jnp.where(l2 == m2, lane, LANES), axis=-1, keepdims=True)
    e2 = jnp.exp(m2 - m1)
    return lane, i1, i2, 1.0 / (1.0 + e2), e2 / (1.0 + e2)


def _route_kernel(x_ref, mod_ref, g2_ref, wr_ref, h_ref, ri_ref, rp_ref, cnt_ref, carry_ref,
                  *, n_experts):
    m = pl.program_id(0)
    tm = x_ref.shape[0]

    @pl.when(m == 0)
    def _():
        carry_ref[...] = jnp.zeros_like(carry_ref)

    h = _modulate_ffn(x_ref, mod_ref, g2_ref)
    h_ref[...] = h
    logits = jnp.dot(h, wr_ref[...], preferred_element_type=F32, precision=lax.Precision.HIGHEST)
    lane, i1, i2, p1, p2 = _top2(logits, n_experts)
    sel = jnp.logical_or(lane == i1, lane == i2)
    r = lax.broadcasted_iota(jnp.int32, (tm, tm), 0)
    c = lax.broadcasted_iota(jnp.int32, (tm, tm), 1)
    before = jnp.where(r > c, 1.0, 0.0).astype(BF16)
    rank = jnp.dot(before, jnp.where(sel, 1.0, 0.0).astype(BF16), preferred_element_type=F32)
    rank = rank + carry_ref[0:1, :]
    r1 = jnp.sum(jnp.where(lane == i1, rank, 0.0), axis=-1, keepdims=True).astype(jnp.int32)
    r2 = jnp.sum(jnp.where(lane == i2, rank, 0.0), axis=-1, keepdims=True).astype(jnp.int32)
    carry_ref[0:1, :] = carry_ref[0:1, :] + jnp.sum(jnp.where(sel, 1.0, 0.0), axis=0, keepdims=True)
    ri_ref[...] = jnp.where(lane == 0, i1, jnp.where(lane == 1, i2,
                            jnp.where(lane == 2, r1, jnp.where(lane == 3, r2, 0))))
    rp_ref[...] = jnp.where(lane == 0, p1, jnp.where(lane == 1, p2, 0.0))
    cnt_ref[...] = carry_ref[...]


def _row_copy(src_ref, src_row, dst_ref, dst_row, sem):
    return pltpu.make_async_copy(src_ref.at[pl.ds(src_row, 1), :], dst_ref.at[pl.ds(dst_row, 1), :], sem)


def _dispatch_kernel(pos_ref, pad_start_ref, pad_cnt_ref, h_ref, z_ref, hs_ref, sem,
                     *, n_tok, n_experts, window):
    def wait_one():
        _row_copy(h_ref, 0, hs_ref, 0, sem).wait()

    def body(t, carry):
        _row_copy(h_ref, t, hs_ref, pos_ref[2 * t], sem).start()
        _row_copy(h_ref, t, hs_ref, pos_ref[2 * t + 1], sem).start()

        @pl.when(t >= window)
        def _():
            wait_one()
            wait_one()
        return carry

    lax.fori_loop(0, n_tok, body, 0)

    def drain(t, carry):
        wait_one()
        wait_one()
        return carry

    lax.fori_loop(0, min(window, n_tok), drain, 0)

    for e in range(n_experts):
        start, cnt = pad_start_ref[e], pad_cnt_ref[e]

        def pad_body(i, carry):
            _row_copy(z_ref, 0, hs_ref, start + i, sem).start()
            return carry

        def pad_wait(i, carry):
            wait_one()
            return carry

        lax.fori_loop(0, cnt, pad_body, 0)
        lax.fori_loop(0, cnt, pad_wait, 0)


def _cast_kernel(tix_ref, x_ref, o_ref):
    o_ref[...] = x_ref[...].astype(BF16)


def _moe_up_kernel(te_ref, tix_ref, first_ref, nused_ref, hs_ref, wg_ref, wu_ref, a_ref,
                   wgb_ref, wub_ref):
    m = pl.program_id(1)

    @pl.when(first_ref[m] == 1)
    def _():
        wgb_ref[...] = wg_ref[...].astype(BF16)
        wub_ref[...] = wu_ref[...].astype(BF16)

    @pl.when(m < nused_ref[0])
    def _():
        hb = hs_ref[...]
        g = jnp.dot(hb, wgb_ref[...], preferred_element_type=F32)
        u = jnp.dot(hb, wub_ref[...], preferred_element_type=F32)
        a_ref[...] = ((g / (1.0 + jnp.exp(-g))) * u).astype(BF16)


def _moe_down_kernel(te_ref, tix_ref, first_ref, nused_ref, a_ref, wd_ref, ys_ref, wdb_ref):
    m = pl.program_id(1)

    @pl.when(first_ref[m] == 1)
    def _():
        wdb_ref[...] = wd_ref[...].astype(BF16)

    @pl.when(m < nused_ref[0])
    def _():
        ys_ref[...] = jnp.dot(a_ref[...], wdb_ref[...], preferred_element_type=F32)


def _combine_kernel(pos_ref, ys_ref, rp_ref, x_ref, mod_ref, g3_ref, out_ref, buf_ref, sem):
    m = pl.program_id(0)
    tm = x_ref.shape[0]
    base = m * tm

    def issue(r, carry):
        t = base + r
        _row_copy(ys_ref, pos_ref[2 * t], buf_ref, r, sem).start()
        _row_copy(ys_ref, pos_ref[2 * t + 1], buf_ref, tm + r, sem).start()
        return carry

    def wait(r, carry):
        _row_copy(ys_ref, 0, buf_ref, 0, sem).wait()
        return carry

    lax.fori_loop(0, tm, issue, 0)
    lax.fori_loop(0, 2 * tm, wait, 0)
    y = rp_ref[:, 0:1] * buf_ref[0:tm, :] + rp_ref[:, 1:2] * buf_ref[tm:2 * tm, :]
    out_ref[...] = x_ref[...] + mod_ref[5:6, :] * (_rms(y) * g3_ref[...])


def _moe_layer(x, mods, g4, wr, w_e_gate, w_e_up, w_e_down, *, l, i, tm, tf, mrow):
    T, D = x.shape
    NE, EFF = w_e_gate.shape[1], w_e_gate.shape[3]
    n_tiles = T // tm
    n_mt = (TOP_K * T) // tm + NE
    Tp = n_mt * tm
    tn = 512
    assert D % tn == 0 and EFF % tf == 0
    sds = jax.ShapeDtypeStruct

    h, ri, rp, cnt = pl.pallas_call(
        functools.partial(_route_kernel, n_experts=NE),
        grid=(n_tiles,),
        in_specs=[
            pl.BlockSpec((tm, D), lambda m: (m, 0)),
            pl.BlockSpec((None, None, N_MOD, D), lambda m: (l, mrow(m), 0, 0)),
            pl.BlockSpec((None, 1, D), lambda m: (4 * l + 2, 0, 0)),
            pl.BlockSpec((None, D, LANES), lambda m: (i, 0, 0)),
        ],
        out_specs=[
            pl.BlockSpec((tm, D), lambda m: (m, 0)),
            pl.BlockSpec((tm, LANES), lambda m: (m, 0)),
            pl.BlockSpec((tm, LANES), lambda m: (m, 0)),
            pl.BlockSpec((SUBLANES, LANES), lambda m: (0, 0)),
        ],
        out_shape=[sds((T, D), F32), sds((T, LANES), jnp.int32), sds((T, LANES), F32),
                   sds((SUBLANES, LANES), F32)],
        scratch_shapes=[pltpu.VMEM((SUBLANES, LANES), F32)],
        compiler_params=_params(("arbitrary",)),
        name="moe_route",
    )(x, mods, g4, wr)

    cnt = cnt[0, :NE].astype(jnp.int32)
    pc = (cnt + tm - 1) // tm * tm
    ends = jnp.cumsum(pc)
    starts = ends - pc
    pos = jnp.stack([starts[ri[:, 0]] + ri[:, 2], starts[ri[:, 1]] + ri[:, 3]], axis=1).reshape(-1)
    n_used = (ends[-1] // tm).astype(jnp.int32)
    ar = jnp.arange(n_mt, dtype=jnp.int32)
    tix = jnp.minimum(ar, n_used - 1)
    te = jnp.minimum(jnp.searchsorted(ends // tm, tix, side="right"), NE - 1).astype(jnp.int32)
    first = jnp.logical_or(te != jnp.roll(te, 1), ar == 0).astype(jnp.int32)
    nused = n_used.reshape(1)

    hs = pl.pallas_call(
        functools.partial(_dispatch_kernel, n_tok=T, n_experts=NE, window=64),
        grid_spec=pltpu.PrefetchScalarGridSpec(
            num_scalar_prefetch=3, grid=(1,),
            in_specs=[pl.BlockSpec(memory_space=pl.ANY), pl.BlockSpec(memory_space=pl.ANY)],
            out_specs=pl.BlockSpec(memory_space=pl.ANY),
            scratch_shapes=[pltpu.SemaphoreType.DMA(())]),
        out_shape=sds((Tp, D), F32),
        compiler_params=_params(("arbitrary",)),
        name="moe_dispatch",
    )(pos, starts + cnt, pc - cnt, h, jnp.zeros((SUBLANES, D), F32))

    hsb = pl.pallas_call(
        _cast_kernel,
        grid_spec=pltpu.PrefetchScalarGridSpec(
            num_scalar_prefetch=1, grid=(n_mt,),
            in_specs=[pl.BlockSpec((tm, D), lambda m, tix: (tix[m], 0))],
            out_specs=pl.BlockSpec((tm, D), lambda m, tix: (tix[m], 0))),
        out_shape=sds((Tp, D), BF16),
        compiler_params=_params(("arbitrary",)),
        name="moe_cast",
    )(tix, hs)

    a = pl.pallas_call(
        _moe_up_kernel,
        grid_spec=pltpu.PrefetchScalarGridSpec(
            num_scalar_prefetch=4, grid=(EFF // tf, n_mt),
            in_specs=[
                pl.BlockSpec((tm, D), lambda f, m, te, tix, fi, nu: (tix[m], 0)),
                pl.BlockSpec((None, None, D, tf), lambda f, m, te, tix, fi, nu: (i, te[m], 0, f)),
                pl.BlockSpec((None, None, D, tf), lambda f, m, te, tix, fi, nu: (i, te[m], 0, f)),
            ],
            out_specs=pl.BlockSpec((tm, tf), lambda f, m, te, tix, fi, nu: (tix[m], f)),
            scratch_shapes=[pltpu.VMEM((D, tf), BF16), pltpu.VMEM((D, tf), BF16)]),
        out_shape=sds((Tp, EFF), BF16),
        compiler_params=_params(("arbitrary", "arbitrary")),
        name="moe_up",
    )(te, tix, first, nused, hsb, w_e_gate, w_e_up)

    ys = pl.pallas_call(
        _moe_down_kernel,
        grid_spec=pltpu.PrefetchScalarGridSpec(
            num_scalar_prefetch=4, grid=(D // tn, n_mt),
            in_specs=[
                pl.BlockSpec((tm, EFF), lambda n, m, te, tix, fi, nu: (tix[m], 0)),
                pl.BlockSpec((None, None, EFF, tn), lambda n, m, te, tix, fi, nu: (i, te[m], 0, n)),
            ],
            out_specs=pl.BlockSpec((tm, tn), lambda n, m, te, tix, fi, nu: (tix[m], n)),
            scratch_shapes=[pltpu.VMEM((EFF, tn), BF16)]),
        out_shape=sds((Tp, D), F32),
        compiler_params=_params(("arbitrary", "arbitrary")),
        name="moe_down",
    )(te, tix, first, nused, a, w_e_down)

    return pl.pallas_call(
        _combine_kernel,
        grid_spec=pltpu.PrefetchScalarGridSpec(
            num_scalar_prefetch=1, grid=(n_tiles,),
            in_specs=[
                pl.BlockSpec(memory_space=pl.ANY),
                pl.BlockSpec((tm, LANES), lambda m, pos: (m, 0)),
                pl.BlockSpec((tm, D), lambda m, pos: (m, 0)),
                pl.BlockSpec((None, None, N_MOD, D), lambda m, pos: (l, mrow(m), 0, 0)),
                pl.BlockSpec((None, 1, D), lambda m, pos: (4 * l + 3, 0, 0)),
            ],
            out_specs=pl.BlockSpec((tm, D), lambda m, pos: (m, 0)),
            scratch_shapes=[pltpu.VMEM((2 * tm, D), F32), pltpu.SemaphoreType.DMA(())]),
        out_shape=sds((T, D), F32),
        compiler_params=_params(("arbitrary",)),
        name="moe_combine",
    )(pos, ys, rp, x, mods, g4)


def kernel(x_prompt, x_sample, cache_ckv, cache_krope, c, c_ctx, w_in, g_q, g_kv, w_uq, w_uk, w_uv,
           w_conv, w_o, w_ada, b_ada, g_sand, w_ff_gate, w_ff_up, w_ff_down, w_router,
           w_e_gate, w_e_up, w_e_down):
    B, N, D = x_prompt.shape
    DB, DN, _ = x_sample.shape
    L = w_in.shape[0]
    QL, KV = g_q.shape[1], g_kv.shape[1]
    H, NOPE = w_uk.shape[2], w_uk.shape[3]
    ROPE = w_uq.shape[3] - NOPE
    VD = w_uv.shape[3]
    CONV = w_conv.shape[2]
    AW = H * VD
    PAST = cache_ckv.shape[2]
    FF = w_ff_gate.shape[2]
    NE, EFF = w_e_gate.shape[1], w_e_gate.shape[3]
    T_P, T_S = B * N, DB * DN
    T = T_P + T_S
    DK = KV + LANES
    NA = 1024
    tm = TM
    assert ROPE == 64 and 2 * ROPE == LANES and NOPE % LANES == 0 and KV % LANES == 0 and QL % LANES == 0
    assert QL + KV + LANES <= NA and CONV % LANES == 0 and AW + CONV == w_o.shape[1]
    assert T_P % tm == 0 and DN % tm == 0 and tm % N == 0 and T_P % DN == 0
    assert N & (N - 1) == 0 and DN & (DN - 1) == 0 and DN % GRID_W == 0
    assert NE <= LANES and DB + 1 <= SUBLANES
    n_p_tiles = T_P // tm
    tiles_per_s = DN // tm
    n_tiles = T // tm
    scale = float(NOPE + ROPE) ** -0.5
    tf = 512
    assert FF % tf == 0 and EFF % tf == 0

    def mrow(m):
        return jnp.where(m < n_p_tiles, 0, 1 + (m - n_p_tiles) // tiles_per_s)

    def trow(m):
        return jnp.where(m < n_p_tiles, 0, 1 + (m - n_p_tiles) % tiles_per_s)

    t = jnp.arange(DN)
    inv = 1.0 / (ROPE_THETA ** (jnp.arange(0, ROPE // 2, 2, dtype=F32) / (ROPE // 2)))
    ar = (t // GRID_W).astype(F32)[:, None] * inv
    ac = (t % GRID_W).astype(F32)[:, None] * inv
    ang = jnp.concatenate([ar, ar, ac, ac] * 2, axis=-1)
    first = (jnp.arange(LANES) % (ROPE // 2)) < (ROPE // 4)
    cos_t = jnp.concatenate([jnp.ones((tm, LANES), F32), jnp.cos(ang)], axis=0)
    sin = jnp.sin(ang)
    sa_t = jnp.concatenate([jnp.zeros((tm, LANES), F32), jnp.where(first, -sin, 0.0)], axis=0)
    sb_t = jnp.concatenate([jnp.zeros((tm, LANES), F32), jnp.where(first, 0.0, sin)], axis=0)

    s1, s2 = QL + KV + ROPE, QL + KV + ROPE + 3 * CONV
    w_a = jnp.pad(w_in[:, :, :s1], ((0, 0), (0, 0), (0, NA - s1))).astype(BF16)
    w_b = w_in[:, :, s1:s2].astype(BF16)
    wq = jnp.pad(w_uq, ((0, 0), (0, 0), (0, 0), (0, LANES - ROPE))).astype(BF16)
    wq = wq.reshape(L, QL, H * (NOPE + LANES))
    wk = w_uk.transpose(0, 2, 3, 1).astype(BF16)
    wuv = w_uv.transpose(0, 2, 1, 3).astype(BF16)
    wo = w_o.astype(BF16)
    wfg, wfu, wfd = w_ff_gate.astype(BF16), w_ff_up.astype(BF16), w_ff_down.astype(BF16)
    wr = jnp.pad(w_router, ((0, 0), (0, 0), (0, LANES - NE)))
    kcache = jnp.concatenate(
        [cache_ckv, cache_krope, jnp.zeros(cache_krope.shape[:-1] + (LANES - ROPE,), F32)],
        axis=-1).astype(BF16)

    cond8 = jnp.concatenate([c_ctx[None, :], c, jnp.zeros((SUBLANES - 1 - DB, D), F32)], axis=0)
    mods = _ada(cond8, w_ada, b_ada)[:, :1 + DB].reshape(L, 1 + DB, N_MOD, D)

    x = jnp.concatenate([x_prompt.reshape(T_P, D), x_sample.reshape(T_S, D)], axis=0)

    mod_spec2 = lambda l: pl.BlockSpec((None, None, N_MOD, D), lambda m, n: (l, mrow(m), 0, 0))
    row_spec2 = lambda w: pl.BlockSpec((tm, w), lambda m, n: (m, 0))
    tab_spec2 = pl.BlockSpec((tm, LANES), lambda m, n: (trow(m), 0))
    tab_spec1 = pl.BlockSpec((tm, LANES), lambda m: (trow(m), 0))

    ckv_list, kr_list = [], []
    for l in range(L):
        cqn, ckvn, kr, kcat, gates = pl.pallas_call(
            functools.partial(_inproj_kernel, QL=QL, KV=KV, ROPE=ROPE),
            grid=(n_tiles, 1 + 3),
            in_specs=[
                row_spec2(D),
                mod_spec2(l),
                pl.BlockSpec((None, 1, D), lambda m, n: (4 * l, 0, 0)),
                pl.BlockSpec((None, 1, QL), lambda m, n: (l, 0, 0)),
                pl.BlockSpec((None, 1, KV), lambda m, n: (l, 0, 0)),
                tab_spec2, tab_spec2, tab_spec2,
                pl.BlockSpec((None, D, NA), lambda m, n: (l, 0, 0)),
                pl.BlockSpec((None, D, CONV), lambda m, n: (l, 0, jnp.maximum(n - 1, 0))),
            ],
            out_specs=[
                row_spec2(QL), row_spec2(KV), row_spec2(ROPE), row_spec2(DK),
                pl.BlockSpec((tm, CONV), lambda m, n: (m, jnp.maximum(n - 1, 0))),
            ],
            out_shape=[
                jax.ShapeDtypeStruct((T, QL), BF16),
                jax.ShapeDtypeStruct((T, KV), F32),
                jax.ShapeDtypeStruct((T, ROPE), F32),
                jax.ShapeDtypeStruct((T, DK), BF16),
                jax.ShapeDtypeStruct((T, 3 * CONV), F32),
            ],
            scratch_shapes=[pltpu.VMEM((tm, D), BF16)],
            compiler_params=_params(("arbitrary", "arbitrary")),
            name="in_proj",
        )(x, mods, g_sand.reshape(4 * L, 1, D), g_q.reshape(L, 1, QL), g_kv.reshape(L, 1, KV),
          cos_t, sa_t, sb_t, w_a, w_b)
        ckv_list.append(ckvn[:T_P].reshape(B, N, KV))
        kr_list.append(kr[:T_P].reshape(B, N, ROPE))

        qcat = pl.pallas_call(
            functools.partial(_q_kernel, H=H, NOPE=NOPE, KV=KV, scale=scale),
            grid=(n_tiles,),
            in_specs=[
                pl.BlockSpec((tm, QL), lambda m: (m, 0)),
                pl.BlockSpec((None, QL, H * (NOPE + LANES)), lambda m: (l, 0, 0)),
                pl.BlockSpec((None, H, NOPE, KV), lambda m: (l, 0, 0, 0)),
                tab_spec1, tab_spec1, tab_spec1,
            ],
            out_specs=pl.BlockSpec((tm, H * DK), lambda m: (m, 0)),
            out_shape=jax.ShapeDtypeStruct((T, H * DK), BF16),
            compiler_params=_params(("arbitrary",)),
            name="q_proj",
        )(cqn, wq, wk, cos_t, sa_t, sb_t)

        o_p = pl.pallas_call(
            functools.partial(_attn_kernel, H=H, KV=KV, has_cache=False),
            grid=(B,),
            in_specs=[
                pl.BlockSpec((N, H * DK), lambda b: (b, 0)),
                pl.BlockSpec((N, DK), lambda b: (b, 0)),
            ],
            out_specs=pl.BlockSpec((N, H * KV), lambda b: (b, 0)),
            out_shape=jax.ShapeDtypeStruct((T_P, H * KV), BF16),
            compiler_params=_params(("arbitrary",)),
            name="attn_prompt",
        )(qcat, kcat)
        tq = 256
        qs_per = DN // tq
        o_s = pl.pallas_call(
            functools.partial(_attn_kernel, H=H, KV=KV, has_cache=True),
            grid=(DB, qs_per),
            in_specs=[
                pl.BlockSpec((tq, H * DK), lambda b, i: (T_P // tq + b * qs_per + i, 0)),
                pl.BlockSpec((DN, DK), lambda b, i: (T_P // DN + b, 0)),
                pl.BlockSpec((None, None, PAST, DK), lambda b, i: (b, l, 0, 0)),
            ],
            out_specs=pl.BlockSpec((tq, H * KV), lambda b, i: (b * qs_per + i, 0)),
            out_shape=jax.ShapeDtypeStruct((T_S, H * KV), BF16),
            compiler_params=_params(("arbitrary", "arbitrary")),
            name="attn_sample",
        )(qcat, kcat, kcache)
        o_lat = jnp.concatenate([o_p, o_s], axis=0)

        r8 = tm // SUBLANES
        prev8 = lambda m: jnp.maximum(m * r8 - 1, 0)
        next8 = lambda m: jnp.minimum((m + 1) * r8, T // SUBLANES - 1)
        x = pl.pallas_call(
            functools.partial(_mix_kernel, H=H, KV=KV, VD=VD, n_p_tiles=n_p_tiles,
                              seq_p=N, seq_s=DN, t_p=T_P),
            grid=(n_tiles,),
            in_specs=[
                pl.BlockSpec((tm, H * KV), lambda m: (m, 0)),
                pl.BlockSpec((tm, CONV), lambda m: (m, 0)),
                pl.BlockSpec((tm, CONV), lambda m: (m, 1)),
                pl.BlockSpec((tm, CONV), lambda m: (m, 2)),
                pl.BlockSpec((SUBLANES, CONV), lambda m: (prev8(m), 1)),
                pl.BlockSpec((SUBLANES, CONV), lambda m: (prev8(m), 2)),
                pl.BlockSpec((SUBLANES, CONV), lambda m: (next8(m), 1)),
                pl.BlockSpec((SUBLANES, CONV), lambda m: (next8(m), 2)),
                pl.BlockSpec((None, H, KV, VD), lambda m: (l, 0, 0, 0)),
                pl.BlockSpec((None, 3, CONV), lambda m: (l, 0, 0)),
                pl.BlockSpec((None, AW + CONV, D), lambda m: (l, 0, 0)),
                pl.BlockSpec((None, 1, D), lambda m: (4 * l + 1, 0, 0)),
                pl.BlockSpec((None, None, N_MOD, D), lambda m: (l, mrow(m), 0, 0)),
                pl.BlockSpec((tm, D), lambda m: (m, 0)),
            ],
            out_specs=pl.BlockSpec((tm, D), lambda m: (m, 0)),
            out_shape=jax.ShapeDtypeStruct((T, D), F32),
            scratch_shapes=[pltpu.VMEM((tm, AW + CONV), BF16),
                            pltpu.VMEM((tm + 2 * SUBLANES, CONV), F32)],
            compiler_params=_params(("arbitrary",)),
            name="mixer_out",
        )(o_lat, gates, gates, gates, gates, gates, gates, gates, wuv, w_conv, wo,
          g_sand.reshape(4 * L, 1, D), mods, x)

        i = l // 2
        if l % 2 == 0:
            x = pl.pallas_call(
                _ffn_kernel,
                grid=(n_tiles, FF // tf),
                in_specs=[
                    pl.BlockSpec((tm, D), lambda m, f: (m, 0)),
                    pl.BlockSpec((None, None, N_MOD, D), lambda m, f: (l, mrow(m), 0, 0)),
                    pl.BlockSpec((None, 1, D), lambda m, f: (4 * l + 2, 0, 0)),
                    pl.BlockSpec((None, 1, D), lambda m, f: (4 * l + 3, 0, 0)),
                    pl.BlockSpec((None, D, tf), lambda m, f: (i, 0, f)),
                    pl.BlockSpec((None, D, tf), lambda m, f: (i, 0, f)),
                    pl.BlockSpec((None, tf, D), lambda m, f: (i, f, 0)),
                ],
                out_specs=pl.BlockSpec((tm, D), lambda m, f: (m, 0)),
                out_shape=jax.ShapeDtypeStruct((T, D), F32),
                scratch_shapes=[pltpu.VMEM((tm, D), BF16), pltpu.VMEM((tm, D), F32)],
                compiler_params=_params(("arbitrary", "arbitrary")),
                name="ffn_dense",
            )(x, mods, g_sand.reshape(4 * L, 1, D), g_sand.reshape(4 * L, 1, D), wfg, wfu, wfd)
        else:
            x = _moe_layer(x, mods, g_sand.reshape(4 * L, 1, D), wr, w_e_gate, w_e_up, w_e_down,
                           l=l, i=i, tm=tm, tf=tf, mrow=mrow)

    y_prompt = x[:T_P].reshape(B, N, D)
    y_sample = x[T_P:].reshape(DB, DN, D)
    new_ckv = jnp.stack(ckv_list, axis=1)
    new_krope = jnp.stack(kr_list, axis=1)
    return (y_prompt, y_sample, new_ckv, new_krope)
```

```python
import functools

import jax
import jax.numpy as jnp
from jax import lax
from jax.experimental import pallas as pl
from jax.experimental.pallas import tpu as pltpu

GRID_W = 64
ROPE_THETA = 10000.0
EPS = 1e-6
N_MOD = 6
TOP_K = 2

LANES = 128
SUBLANES = 8
TM = 512
TM_FFN = 1024
ROW_CHUNK = 256
VMEM_LIMIT = 56 * 1024 * 1024

BF16 = jnp.bfloat16
F32 = jnp.float32


def _params(sem, vmem=VMEM_LIMIT):
    return pltpu.CompilerParams(dimension_semantics=sem, vmem_limit_bytes=vmem)


def _rms(x):
    return x * lax.rsqrt(jnp.mean(x * x, axis=-1, keepdims=True) + EPS)


def _rope(x, cos, sa, sb):
    return x * cos + pltpu.roll(x, LANES - 16, 1) * sa + pltpu.roll(x, 16, 1) * sb


def _ada_kernel(c_ref, w_ref, b_ref, o_ref):
    c = c_ref[...]
    s = c / (1.0 + jnp.exp(-c))
    o_ref[...] = jnp.dot(s.astype(BF16), w_ref[...].astype(BF16),
                         preferred_element_type=F32) + b_ref[...]


def _ada(cond8, w_ada, b_ada):
    L, D, N6 = w_ada.shape
    tn = 1024 if N6 % 1024 == 0 else N6
    return pl.pallas_call(
        _ada_kernel,
        grid=(L, N6 // tn),
        in_specs=[
            pl.BlockSpec((SUBLANES, D), lambda l, n: (0, 0)),
            pl.BlockSpec((None, D, tn), lambda l, n: (l, 0, n)),
            pl.BlockSpec((None, 1, tn), lambda l, n: (l, 0, n)),
        ],
        out_specs=pl.BlockSpec((None, SUBLANES, tn), lambda l, n: (l, 0, n)),
        out_shape=jax.ShapeDtypeStruct((L, SUBLANES, N6), F32),
        compiler_params=_params(("arbitrary", "arbitrary")),
        name="ada_mod",
    )(cond8, w_ada, b_ada.reshape(L, 1, N6))


def _inproj_kernel(x_ref, mod_ref, g_ref, gq_ref, gkv_ref, cos_ref, sa_ref, sb_ref,
                   wa_ref, wb_ref, cqn_ref, ckv_ref, kr_ref, kcat_ref, gates_ref,
                   hb_ref, *, QL, KV, ROPE):
    n = pl.program_id(1)

    @pl.when(n == 0)
    def _():
        h = _rms(x_ref[...]) * g_ref[...]
        h = h * (1.0 + mod_ref[1:2, :]) + mod_ref[0:1, :]
        hb_ref[...] = h.astype(BF16)
        acc = jnp.dot(hb_ref[...], wa_ref[...], preferred_element_type=F32)
        cqn_ref[...] = (_rms(acc[:, :QL]) * gq_ref[...]).astype(BF16)
        ckv = _rms(acc[:, QL:QL + KV]) * gkv_ref[...]
        ckv_ref[...] = ckv
        kr = _rope(acc[:, QL + KV:QL + KV + LANES], cos_ref[...], sa_ref[...], sb_ref[...])
        kr_ref[...] = kr[:, :ROPE]
        kcat_ref[:, :KV] = ckv.astype(BF16)
        kcat_ref[:, KV:] = kr.astype(BF16)

    @pl.when(n > 0)
    def _():
        gates_ref[...] = jnp.dot(hb_ref[...], wb_ref[...], preferred_element_type=F32)


def _q_kernel(cqn_ref, wq_ref, wk_ref, cos_ref, sa_ref, sb_ref, o_ref, *, H, NOPE, KV, scale):
    QW = NOPE + LANES
    DK = KV + LANES
    cqn = cqn_ref[...]
    for h in range(H):
        q = jnp.dot(cqn, wq_ref[:, h * QW:(h + 1) * QW], preferred_element_type=F32)
        q_lat = jnp.dot(q[:, :NOPE].astype(BF16), wk_ref[h], preferred_element_type=F32)
        q_rope = _rope(q[:, NOPE:], cos_ref[...], sa_ref[...], sb_ref[...])
        o_ref[:, h * DK:h * DK + KV] = (q_lat * scale).astype(BF16)
        o_ref[:, h * DK + KV:(h + 1) * DK] = (q_rope * scale).astype(BF16)


def _attn_kernel(*refs, H, KV, has_cache):
    if has_cache:
        q_ref, kn_ref, kc_ref, o_ref = refs
    else:
        q_ref, kn_ref, o_ref = refs
    DK = KV + LANES
    dn = (((1,), (1,)), ((), ()))
    kn = kn_ref[...]
    vn = kn[:, :KV]
    if has_cache:
        kc = kc_ref[...]
        vc = kc[:, :KV]
    for h in range(H):
        q = q_ref[:, h * DK:(h + 1) * DK]
        s = lax.dot_general(q, kn, dn, preferred_element_type=F32)
        m = jnp.max(s, axis=-1, keepdims=True)
        if has_cache:
            sc = lax.dot_general(q, kc, dn, preferred_element_type=F32)
            m = jnp.maximum(m, jnp.max(sc, axis=-1, keepdims=True))
        p = jnp.exp(s - m)
        l = jnp.sum(p, axis=-1, keepdims=True)
        o = jnp.dot(p.astype(BF16), vn, preferred_element_type=F32)
        if has_cache:
            pc = jnp.exp(sc - m)
            l = l + jnp.sum(pc, axis=-1, keepdims=True)
            o = o + jnp.dot(pc.astype(BF16), vc, preferred_element_type=F32)
        o_ref[:, h * KV:(h + 1) * KV] = (o / l).astype(BF16)


def _mix_kernel(op_ref, os_ref, gb_ref, gc_ref, u_ref, gcp_ref, up_ref, gcn_ref, un_ref,
                wuv_ref, wc_ref, wo_ref, g_ref, mod_ref, x_ref, out_ref,
                cat_ref, vs_ref, *, H, KV, VD, n_p_tiles, seq_p, seq_s, t_p):
    m = pl.program_id(0)
    tm = x_ref.shape[0]
    AW = H * VD

    def value_up(o_ref):
        for h in range(H):
            a = jnp.dot(o_ref[:, h * KV:(h + 1) * KV], wuv_ref[h], preferred_element_type=F32)
            cat_ref[:, h * VD:(h + 1) * VD] = a.astype(BF16)

    @pl.when(m < n_p_tiles)
    def _():
        value_up(op_ref)

    @pl.when(m >= n_p_tiles)
    def _():
        value_up(os_ref)

    v = gc_ref[...] * u_ref[...]
    vs_ref[SUBLANES:SUBLANES + tm, :] = v
    vs_ref[SUBLANES - 1:SUBLANES, :] = gcp_ref[SUBLANES - 1:SUBLANES, :] * up_ref[SUBLANES - 1:SUBLANES, :]
    vs_ref[SUBLANES + tm:SUBLANES + tm + 1, :] = gcn_ref[0:1, :] * un_ref[0:1, :]
    is_p = m < n_p_tiles
    seq = jnp.where(is_p, seq_p, seq_s)
    row = lax.broadcasted_iota(jnp.int32, (tm, 1), 0) + (m * tm - jnp.where(is_p, 0, t_p))
    pos = jnp.bitwise_and(row, seq - 1)
    vp = jnp.where(pos == 0, 0.0, vs_ref[SUBLANES - 1:SUBLANES - 1 + tm, :])
    vn = jnp.where(pos == seq - 1, 0.0, vs_ref[SUBLANES + 1:SUBLANES + 1 + tm, :])
    conv = vp * wc_ref[0:1, :] + v * wc_ref[1:2, :] + vn * wc_ref[2:3, :]
    cat_ref[:, AW:] = (gb_ref[...] * conv).astype(BF16)

    y = jnp.dot(cat_ref[...], wo_ref[...], preferred_element_type=F32)
    out_ref[...] = x_ref[...] + mod_ref[2:3, :] * (_rms(y) * g_ref[...])


def _modulate_ffn(x_ref, mod_ref, g_ref):
    h = _rms(x_ref[...]) * g_ref[...]
    return h * (1.0 + mod_ref[4:5, :]) + mod_ref[3:4, :]


def _swiglu_step(hb, wg_ref, wu_ref, wd_ref):
    g = jnp.dot(hb, wg_ref[...], preferred_element_type=F32)
    u = jnp.dot(hb, wu_ref[...], preferred_element_type=F32)
    a = (g / (1.0 + jnp.exp(-g))) * u
    return jnp.dot(a.astype(BF16), wd_ref[...], preferred_element_type=F32)


def _for_row_chunks(n_rows, chunk, fn):
    def body(c, carry):
        fn(pl.ds(pl.multiple_of(c * chunk, chunk), chunk))
        return carry
    lax.fori_loop(0, n_rows // chunk, body, 0)


def _ffn_kernel(x_ref, mod_ref, g2_ref, g3_ref, wg_ref, wu_ref, wd_ref, out_ref, hb_ref):
    f = pl.program_id(1)
    nf = pl.num_programs(1)
    tm = x_ref.shape[0]

    @pl.when(f == 0)
    def _():
        def prologue(rows):
            h = _rms(x_ref[rows, :]) * g2_ref[...]
            hb_ref[rows, :] = (h * (1.0 + mod_ref[4:5, :]) + mod_ref[3:4, :]).astype(BF16)
        _for_row_chunks(tm, ROW_CHUNK, prologue)

    d = _swiglu_step(hb_ref[...], wg_ref, wu_ref, wd_ref)

    @pl.when(f == 0)
    def _():
        out_ref[...] = d

    @pl.when(f > 0)
    def _():
        out_ref[...] += d

    @pl.when(f == nf - 1)
    def _():
        def epilogue(rows):
            y = _rms(out_ref[rows, :]) * g3_ref[...]
            out_ref[rows, :] = x_ref[rows, :] + mod_ref[5:6, :] * y
        _for_row_chunks(tm, ROW_CHUNK, epilogue)


def _top2(logits, n_experts):
    lane = lax.broadcasted_iota(jnp.int32, logits.shape, 1)
    neg = jnp.float32(-jnp.inf)
    l1 = jnp.where(lane < n_experts, logits, neg)
    m1 = jnp.max(l1, axis=-1, keepdims=True)
    i1 = jnp.min(jnp.where(l1 == m1, lane, LANES), axis=-1, keepdims=True)
    l2 = jnp.where(lane == i1, neg, l1)
    m2 = jnp.max(l2, axis=-1, keepdims=True)
    i2 = jnp.min(jnp.where(l2 == m2, lane, LANES), axis=-1, keepdims=True)
    e2 = jnp.exp(m2 - m1)
    return lane, i1, i2, 1.0 / (1.0 + e2), e2 / (1.0 + e2)


def _route_kernel(x_ref, mod_ref, g2_ref, wr_ref, h_ref, ri_ref, rp_ref, cnt_ref, carry_ref,
                  *, n_experts):
    m = pl.program_id(0)
    tm = x_ref.shape[0]

    @pl.when(m == 0)
    def _():
        carry_ref[...] = jnp.zeros_like(carry_ref)

    h = _modulate_ffn(x_ref, mod_ref, g2_ref)
    h_ref[...] = h
    logits = jnp.dot(h, wr_ref[...], preferred_element_type=F32, precision=lax.Precision.HIGHEST)
    lane, i1, i2, p1, p2 = _top2(logits, n_experts)
    sel = jnp.logical_or(lane == i1, lane == i2)
    r = lax.broadcasted_iota(jnp.int32, (tm, tm), 0)
    c = lax.broadcasted_iota(jnp.int32, (tm, tm), 1)
    before = jnp.where(r > c, 1.0, 0.0).astype(BF16)
    rank = jnp.dot(before, jnp.where(sel, 1.0, 0.0).astype(BF16), preferred_element_type=F32)
    rank = rank + carry_ref[0:1, :]
    r1 = jnp.sum(jnp.where(lane == i1, rank, 0.0), axis=-1, keepdims=True).astype(jnp.int32)
    r2 = jnp.sum(jnp.where(lane == i2, rank, 0.0), axis=-1, keepdims=True).astype(jnp.int32)
    carry_ref[0:1, :] = carry_ref[0:1, :] + jnp.sum(jnp.where(sel, 1.0, 0.0), axis=0, keepdims=True)
    ri_ref[...] = jnp.where(lane == 0, i1, jnp.where(lane == 1, i2,
                            jnp.where(lane == 2, r1, jnp.where(lane == 3, r2, 0))))
    rp_ref[...] = jnp.where(lane == 0, p1, jnp.where(lane == 1, p2, 0.0))
    cnt_ref[...] = carry_ref[...]


def _row_copy(src_ref, src_row, dst_ref, dst_row, sem):
    return pltpu.make_async_copy(src_ref.at[pl.ds(src_row, 1), :], dst_ref.at[pl.ds(dst_row, 1), :], sem)


def _dispatch_kernel(pos_ref, pad_start_ref, pad_cnt_ref, h_ref, hs_ref, zero_ref, sem,
                     *, n_experts):
    m = pl.program_id(0)
    tm = h_ref.shape[0]
    base = m * tm

    def wait_one(i, carry):
        _row_copy(h_ref, 0, hs_ref, 0, sem).wait()
        return carry

    def issue(r, carry):
        t = base + r
        _row_copy(h_ref, r, hs_ref, pos_ref[2 * t], sem).start()
        _row_copy(h_ref, r, hs_ref, pos_ref[2 * t + 1], sem).start()
        return carry

    lax.fori_loop(0, tm, issue, 0, unroll=8)
    lax.fori_loop(0, 2 * tm, wait_one, 0, unroll=8)

    @pl.when(m == 0)
    def _():
        zero_ref[...] = jnp.zeros_like(zero_ref)
        for e in range(n_experts):
            start, cnt = pad_start_ref[e], pad_cnt_ref[e]

            def pad_issue(i, carry):
                _row_copy(zero_ref, 0, hs_ref, start + i, sem).start()
                return carry

            lax.fori_loop(0, cnt, pad_issue, 0)
            lax.fori_loop(0, cnt, wait_one, 0)


def _cast_kernel(tix_ref, x_ref, o_ref):
    o_ref[...] = x_ref[...].astype(BF16)


def _moe_up_kernel(te_ref, tix_ref, first_ref, nused_ref, hs_ref, wg_ref, wu_ref, a_ref,
                   wgb_ref, wub_ref):
    m = pl.program_id(1)

    @pl.when(first_ref[m] == 1)
    def _():
        wgb_ref[...] = wg_ref[...].astype(BF16)
        wub_ref[...] = wu_ref[...].astype(BF16)

    @pl.when(m < nused_ref[0])
    def _():
        hb = hs_ref[...]
        g = jnp.dot(hb, wgb_ref[...], preferred_element_type=F32)
        u = jnp.dot(hb, wub_ref[...], preferred_element_type=F32)
        a_ref[...] = ((g / (1.0 + jnp.exp(-g))) * u).astype(BF16)


def _moe_down_kernel(te_ref, tix_ref, first_ref, nused_ref, a_ref, wd_ref, ys_ref, wdb_ref):
    m = pl.program_id(1)

    @pl.when(first_ref[m] == 1)
    def _():
        wdb_ref[...] = wd_ref[...].astype(BF16)

    @pl.when(m < nused_ref[0])
    def _():
        ys_ref[...] = jnp.dot(a_ref[...], wdb_ref[...], preferred_element_type=F32)


def _combine_kernel(pos_ref, ys_ref, rp_ref, x_ref, mod_ref, g3_ref, *rest, n_split):
    if n_split is None:
        out_ref, buf_ref, sem = rest
    else:
        out_a_ref, out_b_ref, buf_ref, sem = rest
    m = pl.program_id(0)
    n = pl.num_programs(0)
    tm = x_ref.shape[0]

    def issue_tile(tile, slot):
        def issue(r, carry):
            t = tile * tm + r
            _row_copy(ys_ref, pos_ref[2 * t], buf_ref.at[slot], r, sem.at[slot]).start()
            _row_copy(ys_ref, pos_ref[2 * t + 1], buf_ref.at[slot], tm + r, sem.at[slot]).start()
            return carry
        lax.fori_loop(0, tm, issue, 0, unroll=8)

    @pl.when(m == 0)
    def _():
        issue_tile(0, 0)

    @pl.when(m + 1 < n)
    def _():
        issue_tile(m + 1, (m + 1) % 2)

    slot = m % 2

    def wait(r, carry):
        _row_copy(ys_ref, 0, buf_ref.at[slot], 0, sem.at[slot]).wait()
        return carry

    lax.fori_loop(0, 2 * tm, wait, 0, unroll=8)

    def result(rows, second):
        y = rp_ref[rows, 0:1] * buf_ref[slot, rows, :] + rp_ref[rows, 1:2] * buf_ref[slot, second, :]
        return x_ref[rows, :] + mod_ref[5:6, :] * (_rms(y) * g3_ref[...])

    def chunked(dst_ref):
        def body(c, carry):
            r0 = pl.multiple_of(c * ROW_CHUNK, ROW_CHUNK)
            dst_ref[pl.ds(r0, ROW_CHUNK), :] = result(pl.ds(r0, ROW_CHUNK), pl.ds(tm + r0, ROW_CHUNK))
            return carry
        lax.fori_loop(0, tm // ROW_CHUNK, body, 0)

    if n_split is None:
        chunked(out_ref)
    else:
        @pl.when(m < n_split)
        def _():
            chunked(out_a_ref)

        @pl.when(m >= n_split)
        def _():
            chunked(out_b_ref)


def _moe_layer(x, mods, g4, wr, w_e_gate, w_e_up, w_e_down, *, l, i, tm, tf, mrow, n_split=None):
    T, D = x.shape
    NE, EFF = w_e_gate.shape[1], w_e_gate.shape[3]
    n_tiles = T // tm
    n_mt = (TOP_K * T) // tm + NE
    Tp = n_mt * tm
    tn = 512
    assert D % tn == 0 and EFF % tf == 0
    sds = jax.ShapeDtypeStruct

    h, ri, rp, cnt = pl.pallas_call(
        functools.partial(_route_kernel, n_experts=NE),
        grid=(n_tiles,),
        in_specs=[
            pl.BlockSpec((tm, D), lambda m: (m, 0)),
            pl.BlockSpec((None, None, N_MOD, D), lambda m: (l, mrow(m), 0, 0)),
            pl.BlockSpec((None, 1, D), lambda m: (4 * l + 2, 0, 0)),
            pl.BlockSpec((None, D, LANES), lambda m: (i, 0, 0)),
        ],
        out_specs=[
            pl.BlockSpec((tm, D), lambda m: (m, 0)),
            pl.BlockSpec((tm, LANES), lambda m: (m, 0)),
            pl.BlockSpec((tm, LANES), lambda m: (m, 0)),
            pl.BlockSpec((SUBLANES, LANES), lambda m: (0, 0)),
        ],
        out_shape=[sds((T, D), F32), sds((T, LANES), jnp.int32), sds((T, LANES), F32),
                   sds((SUBLANES, LANES), F32)],
        scratch_shapes=[pltpu.VMEM((SUBLANES, LANES), F32)],
        compiler_params=_params(("arbitrary",)),
        name="moe_route",
    )(x, mods, g4, wr)

    cnt = cnt[0, :NE].astype(jnp.int32)
    pc = (cnt + tm - 1) // tm * tm
    ends = jnp.cumsum(pc)
    starts = ends - pc
    pos = jnp.stack([starts[ri[:, 0]] + ri[:, 2], starts[ri[:, 1]] + ri[:, 3]], axis=1).reshape(-1)
    n_used = (ends[-1] // tm).astype(jnp.int32)
    ar = jnp.arange(n_mt, dtype=jnp.int32)
    tix = jnp.minimum(ar, n_used - 1)
    te = jnp.minimum(jnp.searchsorted(ends // tm, tix, side="right"), NE - 1).astype(jnp.int32)
    first = jnp.logical_or(te != jnp.roll(te, 1), ar == 0).astype(jnp.int32)
    nused = n_used.reshape(1)

    hs = pl.pallas_call(
        functools.partial(_dispatch_kernel, n_experts=NE),
        grid_spec=pltpu.PrefetchScalarGridSpec(
            num_scalar_prefetch=3, grid=(n_tiles,),
            in_specs=[pl.BlockSpec((tm, D), lambda m, pos, ps, pn: (m, 0))],
            out_specs=pl.BlockSpec(memory_space=pl.ANY),
            scratch_shapes=[pltpu.VMEM((SUBLANES, D), F32), pltpu.SemaphoreType.DMA(())]),
        out_shape=sds((Tp, D), F32),
        compiler_params=_params(("arbitrary",)),
        name="moe_dispatch",
    )(pos, starts + cnt, pc - cnt, h)

    hsb = pl.pallas_call(
        _cast_kernel,
        grid_spec=pltpu.PrefetchScalarGridSpec(
            num_scalar_prefetch=1, grid=(n_mt,),
            in_specs=[pl.BlockSpec((tm, D), lambda m, tix: (tix[m], 0))],
            out_specs=pl.BlockSpec((tm, D), lambda m, tix: (tix[m], 0))),
        out_shape=sds((Tp, D), BF16),
        compiler_params=_params(("arbitrary",)),
        name="moe_cast",
    )(tix, hs)

    a = pl.pallas_call(
        _moe_up_kernel,
        grid_spec=pltpu.PrefetchScalarGridSpec(
            num_scalar_prefetch=4, grid=(EFF // tf, n_mt),
            in_specs=[
                pl.BlockSpec((tm, D), lambda f, m, te, tix, fi, nu: (tix[m], 0)),
                pl.BlockSpec((None, None, D, tf), lambda f, m, te, tix, fi, nu: (i, te[m], 0, f)),
                pl.BlockSpec((None, None, D, tf), lambda f, m, te, tix, fi, nu: (i, te[m], 0, f)),
            ],
            out_specs=pl.BlockSpec((tm, tf), lambda f, m, te, tix, fi, nu: (tix[m], f)),
            scratch_shapes=[pltpu.VMEM((D, tf), BF16), pltpu.VMEM((D, tf), BF16)]),
        out_shape=sds((Tp, EFF), BF16),
        compiler_params=_params(("arbitrary", "arbitrary")),
        name="moe_up",
    )(te, tix, first, nused, hsb, w_e_gate, w_e_up)

    ys = pl.pallas_call(
        _moe_down_kernel,
        grid_spec=pltpu.PrefetchScalarGridSpec(
            num_scalar_prefetch=4, grid=(D // tn, n_mt),
            in_specs=[
                pl.BlockSpec((tm, EFF), lambda n, m, te, tix, fi, nu: (tix[m], 0)),
                pl.BlockSpec((None, None, EFF, tn), lambda n, m, te, tix, fi, nu: (i, te[m], 0, n)),
            ],
            out_specs=pl.BlockSpec((tm, tn), lambda n, m, te, tix, fi, nu: (tix[m], n)),
            scratch_shapes=[pltpu.VMEM((EFF, tn), BF16)]),
        out_shape=sds((Tp, D), F32),
        compiler_params=_params(("arbitrary", "arbitrary")),
        name="moe_down",
    )(te, tix, first, nused, a, w_e_down)

    if n_split is None:
        out_specs = pl.BlockSpec((tm, D), lambda m, pos: (m, 0))
        out_shape = sds((T, D), F32)
    else:
        out_specs = [pl.BlockSpec((tm, D), lambda m, pos: (jnp.minimum(m, n_split - 1), 0)),
                     pl.BlockSpec((tm, D), lambda m, pos: (jnp.maximum(m - n_split, 0), 0))]
        out_shape = [sds((n_split * tm, D), F32), sds((T - n_split * tm, D), F32)]
    return pl.pallas_call(
        functools.partial(_combine_kernel, n_split=n_split),
        grid_spec=pltpu.PrefetchScalarGridSpec(
            num_scalar_prefetch=1, grid=(n_tiles,),
            in_specs=[
                pl.BlockSpec(memory_space=pl.ANY),
                pl.BlockSpec((tm, LANES), lambda m, pos: (m, 0)),
                pl.BlockSpec((tm, D), lambda m, pos: (m, 0)),
                pl.BlockSpec((None, None, N_MOD, D), lambda m, pos: (l, mrow(m), 0, 0)),
                pl.BlockSpec((None, 1, D), lambda m, pos: (4 * l + 3, 0, 0)),
            ],
            out_specs=out_specs,
            scratch_shapes=[pltpu.VMEM((2, 2 * tm, D), F32), pltpu.SemaphoreType.DMA((2,))]),
        out_shape=out_shape,
        compiler_params=_params(("arbitrary",)),
        name="moe_combine",
    )(pos, ys, rp, x, mods, g4)


def kernel(x_prompt, x_sample, cache_ckv, cache_krope, c, c_ctx, w_in, g_q, g_kv, w_uq, w_uk, w_uv,
           w_conv, w_o, w_ada, b_ada, g_sand, w_ff_gate, w_ff_up, w_ff_down, w_router,
           w_e_gate, w_e_up, w_e_down):
    B, N, D = x_prompt.shape
    DB, DN, _ = x_sample.shape
    L = w_in.shape[0]
    QL, KV = g_q.shape[1], g_kv.shape[1]
    H, NOPE = w_uk.shape[2], w_uk.shape[3]
    ROPE = w_uq.shape[3] - NOPE
    VD = w_uv.shape[3]
    CONV = w_conv.shape[2]
    AW = H * VD
    PAST = cache_ckv.shape[2]
    FF = w_ff_gate.shape[2]
    NE, EFF = w_e_gate.shape[1], w_e_gate.shape[3]
    T_P, T_S = B * N, DB * DN
    T = T_P + T_S
    DK = KV + LANES
    NA = 1024
    tm = TM
    assert ROPE == 64 and 2 * ROPE == LANES and NOPE % LANES == 0 and KV % LANES == 0 and QL % LANES == 0
    assert QL + KV + LANES <= NA and CONV % LANES == 0 and AW + CONV == w_o.shape[1]
    assert T_P % tm == 0 and DN % tm == 0 and tm % N == 0 and T_P % DN == 0
    assert N & (N - 1) == 0 and DN & (DN - 1) == 0 and DN % GRID_W == 0
    assert NE <= LANES and DB + 1 <= SUBLANES
    n_p_tiles = T_P // tm
    tiles_per_s = DN // tm
    n_tiles = T // tm
    scale = float(NOPE + ROPE) ** -0.5
    tf = 512
    assert FF % tf == 0 and EFF % tf == 0

    def mrow(m):
        return jnp.where(m < n_p_tiles, 0, 1 + (m - n_p_tiles) // tiles_per_s)

    def trow(m):
        return jnp.where(m < n_p_tiles, 0, 1 + (m - n_p_tiles) % tiles_per_s)

    t = jnp.arange(DN)
    inv = 1.0 / (ROPE_THETA ** (jnp.arange(0, ROPE // 2, 2, dtype=F32) / (ROPE // 2)))
    ar = (t // GRID_W).astype(F32)[:, None] * inv
    ac = (t % GRID_W).astype(F32)[:, None] * inv
    ang = jnp.concatenate([ar, ar, ac, ac] * 2, axis=-1)
    first = (jnp.arange(LANES) % (ROPE // 2)) < (ROPE // 4)
    cos_t = jnp.concatenate([jnp.ones((tm, LANES), F32), jnp.cos(ang)], axis=0)
    sin = jnp.sin(ang)
    sa_t = jnp.concatenate([jnp.zeros((tm, LANES), F32), jnp.where(first, -sin, 0.0)], axis=0)
    sb_t = jnp.concatenate([jnp.zeros((tm, LANES), F32), jnp.where(first, 0.0, sin)], axis=0)

    s1, s2 = QL + KV + ROPE, QL + KV + ROPE + 3 * CONV
    w_a = jnp.pad(w_in[:, :, :s1], ((0, 0), (0, 0), (0, NA - s1))).astype(BF16)
    w_b = w_in[:, :, s1:s2].astype(BF16)
    wq = jnp.pad(w_uq, ((0, 0), (0, 0), (0, 0), (0, LANES - ROPE))).astype(BF16)
    wq = wq.reshape(L, QL, H * (NOPE + LANES))
    wk = w_uk.transpose(0, 2, 3, 1).astype(BF16)
    wuv = w_uv.transpose(0, 2, 1, 3).astype(BF16)
    wo = w_o.astype(BF16)
    wfg, wfu, wfd = w_ff_gate.astype(BF16), w_ff_up.astype(BF16), w_ff_down.astype(BF16)
    wr = jnp.pad(w_router, ((0, 0), (0, 0), (0, LANES - NE)))
    kcache = jnp.concatenate(
        [cache_ckv, cache_krope, jnp.zeros(cache_krope.shape[:-1] + (LANES - ROPE,), F32)],
        axis=-1).astype(BF16)

    cond8 = jnp.concatenate([c_ctx[None, :], c, jnp.zeros((SUBLANES - 1 - DB, D), F32)], axis=0)
    mods = _ada(cond8, w_ada, b_ada)[:, :1 + DB].reshape(L, 1 + DB, N_MOD, D)

    x = jnp.concatenate([x_prompt.reshape(T_P, D), x_sample.reshape(T_S, D)], axis=0)

    mod_spec2 = lambda l: pl.BlockSpec((None, None, N_MOD, D), lambda m, n: (l, mrow(m), 0, 0))
    row_spec2 = lambda w: pl.BlockSpec((tm, w), lambda m, n: (m, 0))
    tab_spec2 = pl.BlockSpec((tm, LANES), lambda m, n: (trow(m), 0))
    tab_spec1 = pl.BlockSpec((tm, LANES), lambda m: (trow(m), 0))

    ckv_list, kr_list = [], []
    for l in range(L):
        cqn, ckvn, kr, kcat, gates = pl.pallas_call(
            functools.partial(_inproj_kernel, QL=QL, KV=KV, ROPE=ROPE),
            grid=(n_tiles, 1 + 3),
            in_specs=[
                row_spec2(D),
                mod_spec2(l),
                pl.BlockSpec((None, 1, D), lambda m, n: (4 * l, 0, 0)),
                pl.BlockSpec((None, 1, QL), lambda m, n: (l, 0, 0)),
                pl.BlockSpec((None, 1, KV), lambda m, n: (l, 0, 0)),
                tab_spec2, tab_spec2, tab_spec2,
                pl.BlockSpec((None, D, NA), lambda m, n: (l, 0, 0)),
                pl.BlockSpec((None, D, CONV), lambda m, n: (l, 0, jnp.maximum(n - 1, 0))),
            ],
            out_specs=[
                row_spec2(QL), row_spec2(KV), row_spec2(ROPE), row_spec2(DK),
                pl.BlockSpec((tm, CONV), lambda m, n: (m, jnp.maximum(n - 1, 0))),
            ],
            out_shape=[
                jax.ShapeDtypeStruct((T, QL), BF16),
                jax.ShapeDtypeStruct((T, KV), F32),
                jax.ShapeDtypeStruct((T, ROPE), F32),
                jax.ShapeDtypeStruct((T, DK), BF16),
                jax.ShapeDtypeStruct((T, 3 * CONV), F32),
            ],
            scratch_shapes=[pltpu.VMEM((tm, D), BF16)],
            compiler_params=_params(("arbitrary", "arbitrary")),
            name="in_proj",
        )(x, mods, g_sand.reshape(4 * L, 1, D), g_q.reshape(L, 1, QL), g_kv.reshape(L, 1, KV),
          cos_t, sa_t, sb_t, w_a, w_b)
        ckv_list.append(ckvn[:T_P].reshape(B, N, KV))
        kr_list.append(kr[:T_P].reshape(B, N, ROPE))

        qcat = pl.pallas_call(
            functools.partial(_q_kernel, H=H, NOPE=NOPE, KV=KV, scale=scale),
            grid=(n_tiles,),
            in_specs=[
                pl.BlockSpec((tm, QL), lambda m: (m, 0)),
                pl.BlockSpec((None, QL, H * (NOPE + LANES)), lambda m: (l, 0, 0)),
                pl.BlockSpec((None, H, NOPE, KV), lambda m: (l, 0, 0, 0)),
                tab_spec1, tab_spec1, tab_spec1,
            ],
            out_specs=pl.BlockSpec((tm, H * DK), lambda m: (m, 0)),
            out_shape=jax.ShapeDtypeStruct((T, H * DK), BF16),
            compiler_params=_params(("arbitrary",)),
            name="q_proj",
        )(cqn, wq, wk, cos_t, sa_t, sb_t)

        o_p = pl.pallas_call(
            functools.partial(_attn_kernel, H=H, KV=KV, has_cache=False),
            grid=(B,),
            in_specs=[
                pl.BlockSpec((N, H * DK), lambda b: (b, 0)),
                pl.BlockSpec((N, DK), lambda b: (b, 0)),
            ],
            out_specs=pl.BlockSpec((N, H * KV), lambda b: (b, 0)),
            out_shape=jax.ShapeDtypeStruct((T_P, H * KV), BF16),
            compiler_params=_params(("arbitrary",)),
            name="attn_prompt",
        )(qcat, kcat)
        tq = 256
        qs_per = DN // tq
        o_s = pl.pallas_call(
            functools.partial(_attn_kernel, H=H, KV=KV, has_cache=True),
            grid=(DB, qs_per),
            in_specs=[
                pl.BlockSpec((tq, H * DK), lambda b, i: (T_P // tq + b * qs_per + i, 0)),
                pl.BlockSpec((DN, DK), lambda b, i: (T_P // DN + b, 0)),
                pl.BlockSpec((None, None, PAST, DK), lambda b, i: (b, l, 0, 0)),
            ],
            out_specs=pl.BlockSpec((tq, H * KV), lambda b, i: (b * qs_per + i, 0)),
            out_shape=jax.ShapeDtypeStruct((T_S, H * KV), BF16),
            compiler_params=_params(("arbitrary", "arbitrary")),
            name="attn_sample",
        )(qcat, kcat, kcache)

        r8 = tm // SUBLANES
        prev8 = lambda m: jnp.maximum(m * r8 - 1, 0)
        next8 = lambda m: jnp.minimum((m + 1) * r8, T // SUBLANES - 1)
        x = pl.pallas_call(
            functools.partial(_mix_kernel, H=H, KV=KV, VD=VD, n_p_tiles=n_p_tiles,
                              seq_p=N, seq_s=DN, t_p=T_P),
            grid=(n_tiles,),
            in_specs=[
                pl.BlockSpec((tm, H * KV), lambda m: (jnp.minimum(m, n_p_tiles - 1), 0)),
                pl.BlockSpec((tm, H * KV), lambda m: (jnp.maximum(m - n_p_tiles, 0), 0)),
                pl.BlockSpec((tm, CONV), lambda m: (m, 0)),
                pl.BlockSpec((tm, CONV), lambda m: (m, 1)),
                pl.BlockSpec((tm, CONV), lambda m: (m, 2)),
                pl.BlockSpec((SUBLANES, CONV), lambda m: (prev8(m), 1)),
                pl.BlockSpec((SUBLANES, CONV), lambda m: (prev8(m), 2)),
                pl.BlockSpec((SUBLANES, CONV), lambda m: (next8(m), 1)),
                pl.BlockSpec((SUBLANES, CONV), lambda m: (next8(m), 2)),
                pl.BlockSpec((None, H, KV, VD), lambda m: (l, 0, 0, 0)),
                pl.BlockSpec((None, 3, CONV), lambda m: (l, 0, 0)),
                pl.BlockSpec((None, AW + CONV, D), lambda m: (l, 0, 0)),
                pl.BlockSpec((None, 1, D), lambda m: (4 * l + 1, 0, 0)),
                pl.BlockSpec((None, None, N_MOD, D), lambda m: (l, mrow(m), 0, 0)),
                pl.BlockSpec((tm, D), lambda m: (m, 0)),
            ],
            out_specs=pl.BlockSpec((tm, D), lambda m: (m, 0)),
            out_shape=jax.ShapeDtypeStruct((T, D), F32),
            scratch_shapes=[pltpu.VMEM((tm, AW + CONV), BF16),
                            pltpu.VMEM((tm + 2 * SUBLANES, CONV), F32)],
            compiler_params=_params(("arbitrary",)),
            name="mixer_out",
        )(o_p, o_s, gates, gates, gates, gates, gates, gates, gates, wuv, w_conv, wo,
          g_sand.reshape(4 * L, 1, D), mods, x)

        i = l // 2
        if l % 2 == 0:
            tmf = TM_FFN if (T_P % TM_FFN == 0 and DN % TM_FFN == 0) else tm
            mrow_f = lambda m: jnp.where(m < T_P // tmf, 0, 1 + (m - T_P // tmf) // (DN // tmf))
            x = pl.pallas_call(
                _ffn_kernel,
                grid=(T // tmf, FF // tf),
                in_specs=[
                    pl.BlockSpec((tmf, D), lambda m, f: (m, 0), pipeline_mode=pl.Buffered(1)),
                    pl.BlockSpec((None, None, N_MOD, D), lambda m, f: (l, mrow_f(m), 0, 0)),
                    pl.BlockSpec((None, 1, D), lambda m, f: (4 * l + 2, 0, 0)),
                    pl.BlockSpec((None, 1, D), lambda m, f: (4 * l + 3, 0, 0)),
                    pl.BlockSpec((None, D, tf), lambda m, f: (i, 0, f)),
                    pl.BlockSpec((None, D, tf), lambda m, f: (i, 0, f)),
                    pl.BlockSpec((None, tf, D), lambda m, f: (i, f, 0)),
                ],
                out_specs=pl.BlockSpec((tmf, D), lambda m, f: (m, 0)),
                out_shape=jax.ShapeDtypeStruct((T, D), F32),
                scratch_shapes=[pltpu.VMEM((tmf, D), BF16)],
                compiler_params=_params(("arbitrary", "arbitrary")),
                name="ffn_dense",
            )(x, mods, g_sand.reshape(4 * L, 1, D), g_sand.reshape(4 * L, 1, D), wfg, wfu, wfd)
        else:
            x = _moe_layer(x, mods, g_sand.reshape(4 * L, 1, D), wr, w_e_gate, w_e_up, w_e_down,
                           l=l, i=i, tm=tm, tf=tf, mrow=mrow,
                           n_split=n_p_tiles if l == L - 1 else None)

    x_p, x_s = x if isinstance(x, (list, tuple)) else (x[:T_P], x[T_P:])
    y_prompt = x_p.reshape(B, N, D)
    y_sample = x_s.reshape(DB, DN, D)
    new_ckv = jnp.stack(ckv_list, axis=1)
    new_krope = jnp.stack(kr_list, axis=1)
    return (y_prompt, y_sample, new_ckv, new_krope)
```

```python
import functools

import jax
import jax.numpy as jnp
from jax import lax
from jax.experimental import pallas as pl
from jax.experimental.pallas import tpu as pltpu

GRID_W = 64
ROPE_THETA = 10000.0
EPS = 1e-6
N_MOD = 6
TOP_K = 2

LANES = 128
SUBLANES = 8
TM = 512
TM_FFN = 1024
ROW_CHUNK = 256
MOE_SUB = 256
VMEM_LIMIT = 56 * 1024 * 1024

BF16 = jnp.bfloat16
F32 = jnp.float32


def _params(sem, vmem=VMEM_LIMIT):
    return pltpu.CompilerParams(dimension_semantics=sem, vmem_limit_bytes=vmem)


def _rms(x):
    return x * lax.rsqrt(jnp.mean(x * x, axis=-1, keepdims=True) + EPS)


def _rope(x, cos, sa, sb):
    return x * cos + pltpu.roll(x, LANES - 16, 1) * sa + pltpu.roll(x, 16, 1) * sb


def _ada_kernel(c_ref, w_ref, b_ref, o_ref):
    c = c_ref[...]
    s = c / (1.0 + jnp.exp(-c))
    o_ref[...] = jnp.dot(s.astype(BF16), w_ref[...].astype(BF16),
                         preferred_element_type=F32) + b_ref[...]


def _ada(cond8, w_ada, b_ada):
    L, D, N6 = w_ada.shape
    tn = 1024 if N6 % 1024 == 0 else N6
    return pl.pallas_call(
        _ada_kernel,
        grid=(L, N6 // tn),
        in_specs=[
            pl.BlockSpec((SUBLANES, D), lambda l, n: (0, 0)),
            pl.BlockSpec((None, D, tn), lambda l, n: (l, 0, n)),
            pl.BlockSpec((None, 1, tn), lambda l, n: (l, 0, n)),
        ],
        out_specs=pl.BlockSpec((None, SUBLANES, tn), lambda l, n: (l, 0, n)),
        out_shape=jax.ShapeDtypeStruct((L, SUBLANES, N6), F32),
        compiler_params=_params(("arbitrary", "arbitrary")),
        name="ada_mod",
    )(cond8, w_ada, b_ada.reshape(L, 1, N6))


def _inproj_kernel(x_ref, mod_ref, g_ref, gq_ref, gkv_ref, cos_ref, sa_ref, sb_ref,
                   wa_ref, wb_ref, cqn_ref, ckv_ref, kr_ref, kcat_ref, gates_ref,
                   hb_ref, *, QL, KV, ROPE):
    n = pl.program_id(1)

    @pl.when(n == 0)
    def _():
        h = _rms(x_ref[...]) * g_ref[...]
        h = h * (1.0 + mod_ref[1:2, :]) + mod_ref[0:1, :]
        hb_ref[...] = h.astype(BF16)
        acc = jnp.dot(hb_ref[...], wa_ref[...], preferred_element_type=F32)
        cqn_ref[...] = (_rms(acc[:, :QL]) * gq_ref[...]).astype(BF16)
        ckv = _rms(acc[:, QL:QL + KV]) * gkv_ref[...]
        ckv_ref[...] = ckv
        kr = _rope(acc[:, QL + KV:QL + KV + LANES], cos_ref[...], sa_ref[...], sb_ref[...])
        kr_ref[...] = kr[:, :ROPE]
        kcat_ref[:, :KV] = ckv.astype(BF16)
        kcat_ref[:, KV:] = kr.astype(BF16)

    @pl.when(n > 0)
    def _():
        gates_ref[...] = jnp.dot(hb_ref[...], wb_ref[...], preferred_element_type=F32)


def _q_kernel(cqn_ref, wq_ref, wk_ref, cos_ref, sa_ref, sb_ref, o_ref, *, H, NOPE, KV, scale):
    QW = NOPE + LANES
    DK = KV + LANES
    cqn = cqn_ref[...]
    for h in range(H):
        q = jnp.dot(cqn, wq_ref[:, h * QW:(h + 1) * QW], preferred_element_type=F32)
        q_lat = jnp.dot(q[:, :NOPE].astype(BF16), wk_ref[h], preferred_element_type=F32)
        q_rope = _rope(q[:, NOPE:], cos_ref[...], sa_ref[...], sb_ref[...])
        o_ref[:, h * DK:h * DK + KV] = (q_lat * scale).astype(BF16)
        o_ref[:, h * DK + KV:(h + 1) * DK] = (q_rope * scale).astype(BF16)


def _attn_kernel(*refs, H, KV, has_cache):
    if has_cache:
        q_ref, kn_ref, kc_ref, o_ref = refs
    else:
        q_ref, kn_ref, o_ref = refs
    DK = KV + LANES
    dn = (((1,), (1,)), ((), ()))
    kn = kn_ref[...]
    vn = kn[:, :KV]
    if has_cache:
        kc = kc_ref[...]
        vc = kc[:, :KV]
    for h in range(H):
        q = q_ref[:, h * DK:(h + 1) * DK]
        s = lax.dot_general(q, kn, dn, preferred_element_type=F32)
        m = jnp.max(s, axis=-1, keepdims=True)
        if has_cache:
            sc = lax.dot_general(q, kc, dn, preferred_element_type=F32)
            m = jnp.maximum(m, jnp.max(sc, axis=-1, keepdims=True))
        p = jnp.exp(s - m)
        l = jnp.sum(p, axis=-1, keepdims=True)
        o = jnp.dot(p.astype(BF16), vn, preferred_element_type=F32)
        if has_cache:
            pc = jnp.exp(sc - m)
            l = l + jnp.sum(pc, axis=-1, keepdims=True)
            o = o + jnp.dot(pc.astype(BF16), vc, preferred_element_type=F32)
        o_ref[:, h * KV:(h + 1) * KV] = (o / l).astype(BF16)


def _mix_kernel(op_ref, os_ref, gb_ref, gc_ref, u_ref, gcp_ref, up_ref, gcn_ref, un_ref,
                wuv_ref, wc_ref, wo_ref, g_ref, mod_ref, x_ref, out_ref,
                cat_ref, vs_ref, *, H, KV, VD, n_p_tiles, seq_p, seq_s, t_p):
    m = pl.program_id(0)
    tm = x_ref.shape[0]
    AW = H * VD
    is_p = m < n_p_tiles
    seq = jnp.where(is_p, seq_p, seq_s)
    row0 = m * tm - jnp.where(is_p, 0, t_p)

    vs_ref[SUBLANES:SUBLANES + tm, :] = gc_ref[...] * u_ref[...]
    vs_ref[SUBLANES - 1:SUBLANES, :] = gcp_ref[SUBLANES - 1:SUBLANES, :] * up_ref[SUBLANES - 1:SUBLANES, :]
    vs_ref[SUBLANES + tm:SUBLANES + tm + 1, :] = gcn_ref[0:1, :] * un_ref[0:1, :]

    for r0 in range(0, tm, ROW_CHUNK):
        rows = slice(r0, r0 + ROW_CHUNK)
        for h in range(H):
            cols = slice(h * KV, (h + 1) * KV)
            o = jnp.where(is_p, op_ref[rows, cols], os_ref[rows, cols])
            a = jnp.dot(o, wuv_ref[h], preferred_element_type=F32)
            cat_ref[rows, h * VD:(h + 1) * VD] = a.astype(BF16)
        pos = jnp.bitwise_and(lax.broadcasted_iota(jnp.int32, (ROW_CHUNK, 1), 0) + (row0 + r0), seq - 1)
        v = vs_ref[SUBLANES + r0:SUBLANES + r0 + ROW_CHUNK, :]
        vp = jnp.where(pos == 0, 0.0, vs_ref[SUBLANES - 1 + r0:SUBLANES - 1 + r0 + ROW_CHUNK, :])
        vn = jnp.where(pos == seq - 1, 0.0, vs_ref[SUBLANES + 1 + r0:SUBLANES + 1 + r0 + ROW_CHUNK, :])
        conv = vp * wc_ref[0:1, :] + v * wc_ref[1:2, :] + vn * wc_ref[2:3, :]
        cat_ref[rows, AW:] = (gb_ref[rows, :] * conv).astype(BF16)
        y = jnp.dot(cat_ref[rows, :], wo_ref[...], preferred_element_type=F32)
        out_ref[rows, :] = x_ref[rows, :] + mod_ref[2:3, :] * (_rms(y) * g_ref[...])


def _modulate_ffn(x_ref, mod_ref, g_ref):
    h = _rms(x_ref[...]) * g_ref[...]
    return h * (1.0 + mod_ref[4:5, :]) + mod_ref[3:4, :]


def _swiglu_step(hb, wg_ref, wu_ref, wd_ref):
    g = jnp.dot(hb, wg_ref[...], preferred_element_type=F32)
    u = jnp.dot(hb, wu_ref[...], preferred_element_type=F32)
    a = (g / (1.0 + jnp.exp(-g))) * u
    return jnp.dot(a.astype(BF16), wd_ref[...], preferred_element_type=F32)


def _for_row_chunks(n_rows, chunk, fn):
    def body(c, carry):
        fn(pl.ds(pl.multiple_of(c * chunk, chunk), chunk))
        return carry
    lax.fori_loop(0, n_rows // chunk, body, 0)


def _ffn_kernel(x_ref, mod_ref, g2_ref, g3_ref, wg_ref, wu_ref, wd_ref, out_ref, hb_ref):
    f = pl.program_id(1)
    nf = pl.num_programs(1)
    tm = x_ref.shape[0]

    @pl.when(f == 0)
    def _():
        def prologue(rows):
            h = _rms(x_ref[rows, :]) * g2_ref[...]
            hb_ref[rows, :] = (h * (1.0 + mod_ref[4:5, :]) + mod_ref[3:4, :]).astype(BF16)
            out_ref[rows, :] = jnp.zeros((ROW_CHUNK, out_ref.shape[1]), F32)
        _for_row_chunks(tm, ROW_CHUNK, prologue)

    out_ref[...] += _swiglu_step(hb_ref[...], wg_ref, wu_ref, wd_ref)

    @pl.when(f == nf - 1)
    def _():
        def epilogue(rows):
            y = _rms(out_ref[rows, :]) * g3_ref[...]
            out_ref[rows, :] = x_ref[rows, :] + mod_ref[5:6, :] * y
        _for_row_chunks(tm, ROW_CHUNK, epilogue)


def _top2(logits, n_experts):
    lane = lax.broadcasted_iota(jnp.int32, logits.shape, 1)
    neg = jnp.float32(-jnp.inf)
    l1 = jnp.where(lane < n_experts, logits, neg)
    m1 = jnp.max(l1, axis=-1, keepdims=True)
    i1 = jnp.min(jnp.where(l1 == m1, lane, LANES), axis=-1, keepdims=True)
    l2 = jnp.where(lane == i1, neg, l1)
    m2 = jnp.max(l2, axis=-1, keepdims=True)
    i2 = jnp.min(jnp.where(l2 == m2, lane, LANES), axis=-1, keepdims=True)
    e2 = jnp.exp(m2 - m1)
    return lane, i1, i2, 1.0 / (1.0 + e2), e2 / (1.0 + e2)


def _route_kernel(x_ref, mod_ref, g2_ref, wr_ref, h_ref, ri_ref, rp_ref, cnt_ref, carry_ref,
                  *, n_experts):
    m = pl.program_id(0)
    tm = x_ref.shape[0]

    @pl.when(m == 0)
    def _():
        carry_ref[...] = jnp.zeros_like(carry_ref)

    h = _modulate_ffn(x_ref, mod_ref, g2_ref)
    h_ref[...] = h
    logits = jnp.dot(h, wr_ref[...], preferred_element_type=F32, precision=lax.Precision.HIGHEST)
    lane, i1, i2, p1, p2 = _top2(logits, n_experts)
    sel = jnp.logical_or(lane == i1, lane == i2)
    r = lax.broadcasted_iota(jnp.int32, (tm, tm), 0)
    c = lax.broadcasted_iota(jnp.int32, (tm, tm), 1)
    before = jnp.where(r > c, 1.0, 0.0).astype(BF16)
    rank = jnp.dot(before, jnp.where(sel, 1.0, 0.0).astype(BF16), preferred_element_type=F32)
    rank = rank + carry_ref[0:1, :]
    r1 = jnp.sum(jnp.where(lane == i1, rank, 0.0), axis=-1, keepdims=True).astype(jnp.int32)
    r2 = jnp.sum(jnp.where(lane == i2, rank, 0.0), axis=-1, keepdims=True).astype(jnp.int32)
    carry_ref[0:1, :] = carry_ref[0:1, :] + jnp.sum(jnp.where(sel, 1.0, 0.0), axis=0, keepdims=True)
    ri_ref[...] = jnp.where(lane == 0, i1, jnp.where(lane == 1, i2,
                            jnp.where(lane == 2, r1, jnp.where(lane == 3, r2, 0))))
    rp_ref[...] = jnp.where(lane == 0, p1, jnp.where(lane == 1, p2, 0.0))
    cnt_ref[...] = carry_ref[...]


def _row_copy(src_ref, src_row, dst_ref, dst_row, sem):
    return pltpu.make_async_copy(src_ref.at[pl.ds(src_row, 1), :], dst_ref.at[pl.ds(dst_row, 1), :], sem)


def _dispatch_kernel(pos_ref, pad_start_ref, pad_cnt_ref, h_ref, hs_ref, zero_ref, sem,
                     *, n_experts):
    m = pl.program_id(0)
    tm = h_ref.shape[0]
    base = m * tm

    def wait_one(i, carry):
        _row_copy(h_ref, 0, hs_ref, 0, sem).wait()
        return carry

    def issue(r, carry):
        t = base + r
        _row_copy(h_ref, r, hs_ref, pos_ref[2 * t], sem).start()
        _row_copy(h_ref, r, hs_ref, pos_ref[2 * t + 1], sem).start()
        return carry

    lax.fori_loop(0, tm, issue, 0, unroll=8)
    lax.fori_loop(0, 2 * tm, wait_one, 0, unroll=8)

    @pl.when(m == 0)
    def _():
        zero_ref[...] = jnp.zeros_like(zero_ref)
        for e in range(n_experts):
            start, cnt = pad_start_ref[e], pad_cnt_ref[e]

            def pad_issue(i, carry):
                _row_copy(zero_ref, 0, hs_ref, start + i, sem).start()
                return carry

            lax.fori_loop(0, cnt, pad_issue, 0)
            lax.fori_loop(0, cnt, wait_one, 0)


def _for_live_rows(nrows, tm, fn):
    for n in range(MOE_SUB, tm + 1, MOE_SUB):
        pl.when(nrows == n)(functools.partial(fn, n))


def _cast_kernel(tix_ref, nrows_ref, x_ref, o_ref):
    def cast(n):
        o_ref[0:n, :] = x_ref[0:n, :].astype(BF16)
    _for_live_rows(nrows_ref[pl.program_id(0)], x_ref.shape[0], cast)


def _moe_up_kernel(te_ref, tix_ref, first_ref, nrows_ref, hs_ref, wg_ref, wu_ref, a_ref,
                   wgb_ref, wub_ref):
    m = pl.program_id(1)

    @pl.when(first_ref[m] == 1)
    def _():
        wgb_ref[...] = wg_ref[...].astype(BF16)
        wub_ref[...] = wu_ref[...].astype(BF16)

    def up(n):
        hb = hs_ref[0:n, :]
        g = jnp.dot(hb, wgb_ref[...], preferred_element_type=F32)
        u = jnp.dot(hb, wub_ref[...], preferred_element_type=F32)
        a_ref[0:n, :] = ((g / (1.0 + jnp.exp(-g))) * u).astype(BF16)
    _for_live_rows(nrows_ref[m], hs_ref.shape[0], up)


def _moe_down_kernel(te_ref, tix_ref, first_ref, nrows_ref, a_ref, wd_ref, ys_ref, wdb_ref):
    m = pl.program_id(1)

    @pl.when(first_ref[m] == 1)
    def _():
        wdb_ref[...] = wd_ref[...].astype(BF16)

    def down(n):
        ys_ref[0:n, :] = jnp.dot(a_ref[0:n, :], wdb_ref[...], preferred_element_type=F32)
    _for_live_rows(nrows_ref[m], a_ref.shape[0], down)


def _combine_kernel(pos_ref, ys_ref, rp_ref, x_ref, mod_ref, g3_ref, *rest, n_split):
    if n_split is None:
        out_ref, buf_ref, sem = rest
    else:
        out_a_ref, out_b_ref, buf_ref, sem = rest
    m = pl.program_id(0)
    n = pl.num_programs(0)
    tm = x_ref.shape[0]

    def issue_tile(tile, slot):
        def issue(r, carry):
            t = tile * tm + r
            _row_copy(ys_ref, pos_ref[2 * t], buf_ref.at[slot], r, sem.at[slot]).start()
            _row_copy(ys_ref, pos_ref[2 * t + 1], buf_ref.at[slot], tm + r, sem.at[slot]).start()
            return carry
        lax.fori_loop(0, tm, issue, 0, unroll=8)

    @pl.when(m == 0)
    def _():
        issue_tile(0, 0)

    @pl.when(m + 1 < n)
    def _():
        issue_tile(m + 1, (m + 1) % 2)

    slot = m % 2

    def wait(r, carry):
        _row_copy(ys_ref, 0, buf_ref.at[slot], 0, sem.at[slot]).wait()
        return carry

    lax.fori_loop(0, 2 * tm, wait, 0, unroll=8)

    def result(rows, second):
        y = rp_ref[rows, 0:1] * buf_ref[slot, rows, :] + rp_ref[rows, 1:2] * buf_ref[slot, second, :]
        return x_ref[rows, :] + mod_ref[5:6, :] * (_rms(y) * g3_ref[...])

    def chunked(dst_ref):
        def body(c, carry):
            r0 = pl.multiple_of(c * ROW_CHUNK, ROW_CHUNK)
            dst_ref[pl.ds(r0, ROW_CHUNK), :] = result(pl.ds(r0, ROW_CHUNK), pl.ds(tm + r0, ROW_CHUNK))
            return carry
        lax.fori_loop(0, tm // ROW_CHUNK, body, 0)

    if n_split is None:
        chunked(out_ref)
    else:
        @pl.when(m < n_split)
        def _():
            chunked(out_a_ref)

        @pl.when(m >= n_split)
        def _():
            chunked(out_b_ref)


def _moe_layer(x, mods, g4, wr, w_e_gate, w_e_up, w_e_down, *, l, i, tm, tf, mrow, n_split=None):
    T, D = x.shape
    NE, EFF = w_e_gate.shape[1], w_e_gate.shape[3]
    n_tiles = T // tm
    n_mt = (TOP_K * T) // tm + NE
    Tp = n_mt * tm
    tn = 512
    assert D % tn == 0 and EFF % tf == 0
    sds = jax.ShapeDtypeStruct

    h, ri, rp, cnt = pl.pallas_call(
        functools.partial(_route_kernel, n_experts=NE),
        grid=(n_tiles,),
        in_specs=[
            pl.BlockSpec((tm, D), lambda m: (m, 0)),
            pl.BlockSpec((None, None, N_MOD, D), lambda m: (l, mrow(m), 0, 0)),
            pl.BlockSpec((None, 1, D), lambda m: (4 * l + 2, 0, 0)),
            pl.BlockSpec((None, D, LANES), lambda m: (i, 0, 0)),
        ],
        out_specs=[
            pl.BlockSpec((tm, D), lambda m: (m, 0)),
            pl.BlockSpec((tm, LANES), lambda m: (m, 0)),
            pl.BlockSpec((tm, LANES), lambda m: (m, 0)),
            pl.BlockSpec((SUBLANES, LANES), lambda m: (0, 0)),
        ],
        out_shape=[sds((T, D), F32), sds((T, LANES), jnp.int32), sds((T, LANES), F32),
                   sds((SUBLANES, LANES), F32)],
        scratch_shapes=[pltpu.VMEM((SUBLANES, LANES), F32)],
        compiler_params=_params(("arbitrary",)),
        name="moe_route",
    )(x, mods, g4, wr)

    cnt = cnt[0, :NE].astype(jnp.int32)
    pc = (cnt + tm - 1) // tm * tm
    ends = jnp.cumsum(pc)
    starts = ends - pc
    pos = jnp.stack([starts[ri[:, 0]] + ri[:, 2], starts[ri[:, 1]] + ri[:, 3]], axis=1).reshape(-1)
    n_used = (ends[-1] // tm).astype(jnp.int32)
    ar = jnp.arange(n_mt, dtype=jnp.int32)
    tix = jnp.minimum(ar, n_used - 1)
    te = jnp.minimum(jnp.sum(tix[:, None] >= (ends // tm)[None, :], axis=1), NE - 1).astype(jnp.int32)
    first = jnp.logical_or(te != jnp.roll(te, 1), ar == 0).astype(jnp.int32)
    live = jnp.clip(cnt[te] - (tix * tm - starts[te]), 0, tm)
    nrows = jnp.where(ar < n_used, (live + MOE_SUB - 1) // MOE_SUB * MOE_SUB, 0).astype(jnp.int32)
    zero_fill = (-cnt) % MOE_SUB

    hs = pl.pallas_call(
        functools.partial(_dispatch_kernel, n_experts=NE),
        grid_spec=pltpu.PrefetchScalarGridSpec(
            num_scalar_prefetch=3, grid=(n_tiles,),
            in_specs=[pl.BlockSpec((tm, D), lambda m, pos, ps, pn: (m, 0))],
            out_specs=pl.BlockSpec(memory_space=pl.ANY),
            scratch_shapes=[pltpu.VMEM((SUBLANES, D), F32), pltpu.SemaphoreType.DMA(())]),
        out_shape=sds((Tp, D), F32),
        compiler_params=_params(("arbitrary",)),
        name="moe_dispatch",
    )(pos, starts + cnt, zero_fill, h)

    hsb = pl.pallas_call(
        _cast_kernel,
        grid_spec=pltpu.PrefetchScalarGridSpec(
            num_scalar_prefetch=2, grid=(n_mt,),
            in_specs=[pl.BlockSpec((tm, D), lambda m, tix, nr: (tix[m], 0))],
            out_specs=pl.BlockSpec((tm, D), lambda m, tix, nr: (tix[m], 0))),
        out_shape=sds((Tp, D), BF16),
        compiler_params=_params(("arbitrary",)),
        name="moe_cast",
    )(tix, nrows, hs)

    a = pl.pallas_call(
        _moe_up_kernel,
        grid_spec=pltpu.PrefetchScalarGridSpec(
            num_scalar_prefetch=4, grid=(EFF // tf, n_mt),
            in_specs=[
                pl.BlockSpec((tm, D), lambda f, m, te, tix, fi, nu: (tix[m], 0)),
                pl.BlockSpec((None, None, D, tf), lambda f, m, te, tix, fi, nu: (i, te[m], 0, f)),
                pl.BlockSpec((None, None, D, tf), lambda f, m, te, tix, fi, nu: (i, te[m], 0, f)),
            ],
            out_specs=pl.BlockSpec((tm, tf), lambda f, m, te, tix, fi, nu: (tix[m], f)),
            scratch_shapes=[pltpu.VMEM((D, tf), BF16), pltpu.VMEM((D, tf), BF16)]),
        out_shape=sds((Tp, EFF), BF16),
        compiler_params=_params(("arbitrary", "arbitrary")),
        name="moe_up",
    )(te, tix, first, nrows, hsb, w_e_gate, w_e_up)

    ys = pl.pallas_call(
        _moe_down_kernel,
        grid_spec=pltpu.PrefetchScalarGridSpec(
            num_scalar_prefetch=4, grid=(D // tn, n_mt),
            in_specs=[
                pl.BlockSpec((tm, EFF), lambda n, m, te, tix, fi, nu: (tix[m], 0)),
                pl.BlockSpec((None, None, EFF, tn), lambda n, m, te, tix, fi, nu: (i, te[m], 0, n)),
            ],
            out_specs=pl.BlockSpec((tm, tn), lambda n, m, te, tix, fi, nu: (tix[m], n)),
            scratch_shapes=[pltpu.VMEM((EFF, tn), BF16)]),
        out_shape=sds((Tp, D), F32),
        compiler_params=_params(("arbitrary", "arbitrary")),
        name="moe_down",
    )(te, tix, first, nrows, a, w_e_down)

    if n_split is None:
        out_specs = pl.BlockSpec((tm, D), lambda m, pos: (m, 0))
        out_shape = sds((T, D), F32)
    else:
        out_specs = [pl.BlockSpec((tm, D), lambda m, pos: (jnp.minimum(m, n_split - 1), 0)),
                     pl.BlockSpec((tm, D), lambda m, pos: (jnp.maximum(m - n_split, 0), 0))]
        out_shape = [sds((n_split * tm, D), F32), sds((T - n_split * tm, D), F32)]
    return pl.pallas_call(
        functools.partial(_combine_kernel, n_split=n_split),
        grid_spec=pltpu.PrefetchScalarGridSpec(
            num_scalar_prefetch=1, grid=(n_tiles,),
            in_specs=[
                pl.BlockSpec(memory_space=pl.ANY),
                pl.BlockSpec((tm, LANES), lambda m, pos: (m, 0)),
                pl.BlockSpec((tm, D), lambda m, pos: (m, 0)),
                pl.BlockSpec((None, None, N_MOD, D), lambda m, pos: (l, mrow(m), 0, 0)),
                pl.BlockSpec((None, 1, D), lambda m, pos: (4 * l + 3, 0, 0)),
            ],
            out_specs=out_specs,
            scratch_shapes=[pltpu.VMEM((2, 2 * tm, D), F32), pltpu.SemaphoreType.DMA((2,))]),
        out_shape=out_shape,
        compiler_params=_params(("arbitrary",)),
        name="moe_combine",
    )(pos, ys, rp, x, mods, g4)


def kernel(x_prompt, x_sample, cache_ckv, cache_krope, c, c_ctx, w_in, g_q, g_kv, w_uq, w_uk, w_uv,
           w_conv, w_o, w_ada, b_ada, g_sand, w_ff_gate, w_ff_up, w_ff_down, w_router,
           w_e_gate, w_e_up, w_e_down):
    B, N, D = x_prompt.shape
    DB, DN, _ = x_sample.shape
    L = w_in.shape[0]
    QL, KV = g_q.shape[1], g_kv.shape[1]
    H, NOPE = w_uk.shape[2], w_uk.shape[3]
    ROPE = w_uq.shape[3] - NOPE
    VD = w_uv.shape[3]
    CONV = w_conv.shape[2]
    AW = H * VD
    PAST = cache_ckv.shape[2]
    FF = w_ff_gate.shape[2]
    NE, EFF = w_e_gate.shape[1], w_e_gate.shape[3]
    T_P, T_S = B * N, DB * DN
    T = T_P + T_S
    DK = KV + LANES
    NA = 1024
    tm = TM
    assert ROPE == 64 and 2 * ROPE == LANES and NOPE % LANES == 0 and KV % LANES == 0 and QL % LANES == 0
    assert QL + KV + LANES <= NA and CONV % LANES == 0 and AW + CONV == w_o.shape[1]
    assert T_P % tm == 0 and DN % tm == 0 and tm % N == 0 and T_P % DN == 0
    assert N & (N - 1) == 0 and DN & (DN - 1) == 0 and DN % GRID_W == 0
    assert NE <= LANES and DB + 1 <= SUBLANES
    n_p_tiles = T_P // tm
    tiles_per_s = DN // tm
    n_tiles = T // tm
    scale = float(NOPE + ROPE) ** -0.5
    tf = 512
    assert FF % tf == 0 and EFF % tf == 0

    def mrow(m):
        return jnp.where(m < n_p_tiles, 0, 1 + (m - n_p_tiles) // tiles_per_s)

    def trow(m):
        return jnp.where(m < n_p_tiles, 0, 1 + (m - n_p_tiles) % tiles_per_s)

    t = jnp.arange(DN)
    inv = 1.0 / (ROPE_THETA ** (jnp.arange(0, ROPE // 2, 2, dtype=F32) / (ROPE // 2)))
    ar = (t // GRID_W).astype(F32)[:, None] * inv
    ac = (t % GRID_W).astype(F32)[:, None] * inv
    ang = jnp.concatenate([ar, ar, ac, ac] * 2, axis=-1)
    first = (jnp.arange(LANES) % (ROPE // 2)) < (ROPE // 4)
    cos_t = jnp.concatenate([jnp.ones((tm, LANES), F32), jnp.cos(ang)], axis=0)
    sin = jnp.sin(ang)
    sa_t = jnp.concatenate([jnp.zeros((tm, LANES), F32), jnp.where(first, -sin, 0.0)], axis=0)
    sb_t = jnp.concatenate([jnp.zeros((tm, LANES), F32), jnp.where(first, 0.0, sin)], axis=0)

    s1, s2 = QL + KV + ROPE, QL + KV + ROPE + 3 * CONV
    w_in_b = w_in.astype(BF16)
    w_a = jnp.pad(w_in_b[:, :, :s1], ((0, 0), (0, 0), (0, NA - s1)))
    w_b = w_in_b[:, :, s1:s2]
    wq = jnp.pad(w_uq, ((0, 0), (0, 0), (0, 0), (0, LANES - ROPE))).astype(BF16)
    wq = wq.reshape(L, QL, H * (NOPE + LANES))
    wk = w_uk.transpose(0, 2, 3, 1).astype(BF16)
    wuv = w_uv.transpose(0, 2, 1, 3).astype(BF16)
    wo = w_o.astype(BF16)
    wfg, wfu, wfd = w_ff_gate.astype(BF16), w_ff_up.astype(BF16), w_ff_down.astype(BF16)
    wr = jnp.pad(w_router, ((0, 0), (0, 0), (0, LANES - NE)))
    kcache = jnp.concatenate(
        [cache_ckv, cache_krope, jnp.zeros(cache_krope.shape[:-1] + (LANES - ROPE,), F32)],
        axis=-1).astype(BF16)

    cond8 = jnp.concatenate([c_ctx[None, :], c, jnp.zeros((SUBLANES - 1 - DB, D), F32)], axis=0)
    mods = _ada(cond8, w_ada, b_ada)[:, :1 + DB].reshape(L, 1 + DB, N_MOD, D)

    x = jnp.concatenate([x_prompt.reshape(T_P, D), x_sample.reshape(T_S, D)], axis=0)

    mod_spec2 = lambda l: pl.BlockSpec((None, None, N_MOD, D), lambda m, n: (l, mrow(m), 0, 0))
    row_spec2 = lambda w: pl.BlockSpec((tm, w), lambda m, n: (m, 0))
    tab_spec2 = pl.BlockSpec((tm, LANES), lambda m, n: (trow(m), 0))
    tab_spec1 = pl.BlockSpec((tm, LANES), lambda m: (trow(m), 0))

    ckv_list, kr_list = [], []
    for l in range(L):
        cqn, ckvn, kr, kcat, gates = pl.pallas_call(
            functools.partial(_inproj_kernel, QL=QL, KV=KV, ROPE=ROPE),
            grid=(n_tiles, 1 + 3),
            in_specs=[
                row_spec2(D),
                mod_spec2(l),
                pl.BlockSpec((None, 1, D), lambda m, n: (4 * l, 0, 0)),
                pl.BlockSpec((None, 1, QL), lambda m, n: (l, 0, 0)),
                pl.BlockSpec((None, 1, KV), lambda m, n: (l, 0, 0)),
                tab_spec2, tab_spec2, tab_spec2,
                pl.BlockSpec((None, D, NA), lambda m, n: (l, 0, 0)),
                pl.BlockSpec((None, D, CONV), lambda m, n: (l, 0, jnp.maximum(n - 1, 0))),
            ],
            out_specs=[
                row_spec2(QL), row_spec2(KV), row_spec2(ROPE), row_spec2(DK),
                pl.BlockSpec((tm, CONV), lambda m, n: (m, jnp.maximum(n - 1, 0))),
            ],
            out_shape=[
                jax.ShapeDtypeStruct((T, QL), BF16),
                jax.ShapeDtypeStruct((T, KV), F32),
                jax.ShapeDtypeStruct((T, ROPE), F32),
                jax.ShapeDtypeStruct((T, DK), BF16),
                jax.ShapeDtypeStruct((T, 3 * CONV), F32),
            ],
            scratch_shapes=[pltpu.VMEM((tm, D), BF16)],
            compiler_params=_params(("arbitrary", "arbitrary")),
            name="in_proj",
        )(x, mods, g_sand.reshape(4 * L, 1, D), g_q.reshape(L, 1, QL), g_kv.reshape(L, 1, KV),
          cos_t, sa_t, sb_t, w_a, w_b)
        ckv_list.append(ckvn[:T_P].reshape(B, N, KV))
        kr_list.append(kr[:T_P].reshape(B, N, ROPE))

        qcat = pl.pallas_call(
            functools.partial(_q_kernel, H=H, NOPE=NOPE, KV=KV, scale=scale),
            grid=(n_tiles,),
            in_specs=[
                pl.BlockSpec((tm, QL), lambda m: (m, 0)),
                pl.BlockSpec((None, QL, H * (NOPE + LANES)), lambda m: (l, 0, 0)),
                pl.BlockSpec((None, H, NOPE, KV), lambda m: (l, 0, 0, 0)),
                tab_spec1, tab_spec1, tab_spec1,
            ],
            out_specs=pl.BlockSpec((tm, H * DK), lambda m: (m, 0)),
            out_shape=jax.ShapeDtypeStruct((T, H * DK), BF16),
            compiler_params=_params(("arbitrary",)),
            name="q_proj",
        )(cqn, wq, wk, cos_t, sa_t, sb_t)

        o_p = pl.pallas_call(
            functools.partial(_attn_kernel, H=H, KV=KV, has_cache=False),
            grid=(B,),
            in_specs=[
                pl.BlockSpec((N, H * DK), lambda b: (b, 0)),
                pl.BlockSpec((N, DK), lambda b: (b, 0)),
            ],
            out_specs=pl.BlockSpec((N, H * KV), lambda b: (b, 0)),
            out_shape=jax.ShapeDtypeStruct((T_P, H * KV), BF16),
            compiler_params=_params(("arbitrary",)),
            name="attn_prompt",
        )(qcat, kcat)
        tq = 256
        qs_per = DN // tq
        o_s = pl.pallas_call(
            functools.partial(_attn_kernel, H=H, KV=KV, has_cache=True),
            grid=(DB, qs_per),
            in_specs=[
                pl.BlockSpec((tq, H * DK), lambda b, i: (T_P // tq + b * qs_per + i, 0)),
                pl.BlockSpec((DN, DK), lambda b, i: (T_P // DN + b, 0)),
                pl.BlockSpec((None, None, PAST, DK), lambda b, i: (b, l, 0, 0)),
            ],
            out_specs=pl.BlockSpec((tq, H * KV), lambda b, i: (b * qs_per + i, 0)),
            out_shape=jax.ShapeDtypeStruct((T_S, H * KV), BF16),
            compiler_params=_params(("arbitrary", "arbitrary")),
            name="attn_sample",
        )(qcat, kcat, kcache)

        r8 = tm // SUBLANES
        prev8 = lambda m: jnp.maximum(m * r8 - 1, 0)
        next8 = lambda m: jnp.minimum((m + 1) * r8, T // SUBLANES - 1)
        x = pl.pallas_call(
            functools.partial(_mix_kernel, H=H, KV=KV, VD=VD, n_p_tiles=n_p_tiles,
                              seq_p=N, seq_s=DN, t_p=T_P),
            grid=(n_tiles,),
            in_specs=[
                pl.BlockSpec((tm, H * KV), lambda m: (jnp.minimum(m, n_p_tiles - 1), 0)),
                pl.BlockSpec((tm, H * KV), lambda m: (jnp.maximum(m - n_p_tiles, 0), 0)),
                pl.BlockSpec((tm, CONV), lambda m: (m, 0)),
                pl.BlockSpec((tm, CONV), lambda m: (m, 1)),
                pl.BlockSpec((tm, CONV), lambda m: (m, 2)),
                pl.BlockSpec((SUBLANES, CONV), lambda m: (prev8(m), 1)),
                pl.BlockSpec((SUBLANES, CONV), lambda m: (prev8(m), 2)),
                pl.BlockSpec((SUBLANES, CONV), lambda m: (next8(m), 1)),
                pl.BlockSpec((SUBLANES, CONV), lambda m: (next8(m), 2)),
                pl.BlockSpec((None, H, KV, VD), lambda m: (l, 0, 0, 0)),
                pl.BlockSpec((None, 3, CONV), lambda m: (l, 0, 0)),
                pl.BlockSpec((None, AW + CONV, D), lambda m: (l, 0, 0)),
                pl.BlockSpec((None, 1, D), lambda m: (4 * l + 1, 0, 0)),
                pl.BlockSpec((None, None, N_MOD, D), lambda m: (l, mrow(m), 0, 0)),
                pl.BlockSpec((tm, D), lambda m: (m, 0)),
            ],
            out_specs=pl.BlockSpec((tm, D), lambda m: (m, 0)),
            out_shape=jax.ShapeDtypeStruct((T, D), F32),
            scratch_shapes=[pltpu.VMEM((tm, AW + CONV), BF16),
                            pltpu.VMEM((tm + 2 * SUBLANES, CONV), F32)],
            compiler_params=_params(("arbitrary",)),
            name="mixer_out",
        )(o_p, o_s, gates, gates, gates, gates, gates, gates, gates, wuv, w_conv, wo,
          g_sand.reshape(4 * L, 1, D), mods, x)

        i = l // 2
        if l % 2 == 0:
            tmf = TM_FFN if (T_P % TM_FFN == 0 and DN % TM_FFN == 0) else tm
            mrow_f = lambda m: jnp.where(m < T_P // tmf, 0, 1 + (m - T_P // tmf) // (DN // tmf))
            x = pl.pallas_call(
                _ffn_kernel,
                grid=(T // tmf, FF // tf),
                in_specs=[
                    pl.BlockSpec((tmf, D), lambda m, f: (m, 0), pipeline_mode=pl.Buffered(1)),
                    pl.BlockSpec((None, None, N_MOD, D), lambda m, f: (l, mrow_f(m), 0, 0)),
                    pl.BlockSpec((None, 1, D), lambda m, f: (4 * l + 2, 0, 0)),
                    pl.BlockSpec((None, 1, D), lambda m, f: (4 * l + 3, 0, 0)),
                    pl.BlockSpec((None, D, tf), lambda m, f: (i, 0, f)),
                    pl.BlockSpec((None, D, tf), lambda m, f: (i, 0, f)),
                    pl.BlockSpec((None, tf, D), lambda m, f: (i, f, 0)),
                ],
                out_specs=pl.BlockSpec((tmf, D), lambda m, f: (m, 0)),
                out_shape=jax.ShapeDtypeStruct((T, D), F32),
                scratch_shapes=[pltpu.VMEM((tmf, D), BF16)],
                compiler_params=_params(("arbitrary", "arbitrary")),
                name="ffn_dense",
            )(x, mods, g_sand.reshape(4 * L, 1, D), g_sand.reshape(4 * L, 1, D), wfg, wfu, wfd)
        else:
            x = _moe_layer(x, mods, g_sand.reshape(4 * L, 1, D), wr, w_e_gate, w_e_up, w_e_down,
                           l=l, i=i, tm=tm, tf=tf, mrow=mrow,
                           n_split=n_p_tiles if l == L - 1 else None)

    x_p, x_s = x if isinstance(x, (list, tuple)) else (x[:T_P], x[T_P:])
    y_prompt = x_p.reshape(B, N, D)
    y_sample = x_s.reshape(DB, DN, D)
    new_ckv = jnp.stack(ckv_list, axis=1)
    new_krope = jnp.stack(kr_list, axis=1)
    return (y_prompt, y_sample, new_ckv, new_krope)
```

```python
import functools

import jax
import jax.numpy as jnp
from jax import lax
from jax.experimental import pallas as pl
from jax.experimental.pallas import tpu as pltpu

GRID_W = 64
ROPE_THETA = 10000.0
EPS = 1e-6
N_MOD = 6
TOP_K = 2

LANES = 128
SUBLANES = 8
TM = 512
TM_FFN = 1024
ROW_CHUNK = 256
HALO_ROWS = 16
MOE_SUB = 256
VMEM_LIMIT = 56 * 1024 * 1024

BF16 = jnp.bfloat16
F32 = jnp.float32


def _params(sem, vmem=VMEM_LIMIT):
    return pltpu.CompilerParams(dimension_semantics=sem, vmem_limit_bytes=vmem)


def _rms(x):
    return x * lax.rsqrt(jnp.mean(x * x, axis=-1, keepdims=True) + EPS)


def _rope(x, cos, sa, sb):
    return x * cos + pltpu.roll(x, LANES - 16, 1) * sa + pltpu.roll(x, 16, 1) * sb


def _ada_kernel(c_ref, w_ref, b_ref, o_ref):
    c = c_ref[...]
    s = c / (1.0 + jnp.exp(-c))
    o_ref[...] = jnp.dot(s.astype(BF16), w_ref[...].astype(BF16),
                         preferred_element_type=F32) + b_ref[...]


def _ada(cond8, w_ada, b_ada):
    L, D, N6 = w_ada.shape
    tn = 1024 if N6 % 1024 == 0 else N6
    return pl.pallas_call(
        _ada_kernel,
        grid=(L, N6 // tn),
        in_specs=[
            pl.BlockSpec((SUBLANES, D), lambda l, n: (0, 0)),
            pl.BlockSpec((None, D, tn), lambda l, n: (l, 0, n)),
            pl.BlockSpec((None, 1, tn), lambda l, n: (l, 0, n)),
        ],
        out_specs=pl.BlockSpec((None, SUBLANES, tn), lambda l, n: (l, 0, n)),
        out_shape=jax.ShapeDtypeStruct((L, SUBLANES, N6), F32),
        compiler_params=_params(("arbitrary", "arbitrary")),
        name="ada_mod",
    )(cond8, w_ada, b_ada.reshape(L, 1, N6))


def _inproj_kernel(x_ref, mod_ref, g_ref, gq_ref, gkv_ref, cos_ref, sa_ref, sb_ref,
                   wa_ref, wb_ref, cqn_ref, ckv_ref, kr_ref, kcat_ref, gates_ref,
                   hb_ref, *, QL, KV, ROPE):
    n = pl.program_id(1)

    @pl.when(n == 0)
    def _():
        h = _rms(x_ref[...]) * g_ref[...]
        h = h * (1.0 + mod_ref[1:2, :]) + mod_ref[0:1, :]
        hb_ref[...] = h.astype(BF16)
        acc = jnp.dot(hb_ref[...], wa_ref[...], preferred_element_type=F32)
        cqn_ref[...] = (_rms(acc[:, :QL]) * gq_ref[...]).astype(BF16)
        ckv = _rms(acc[:, QL:QL + KV]) * gkv_ref[...]
        ckv_ref[...] = ckv
        kr = _rope(acc[:, QL + KV:QL + KV + LANES], cos_ref[...], sa_ref[...], sb_ref[...])
        kr_ref[...] = kr[:, :ROPE]
        kcat_ref[:, :KV] = ckv.astype(BF16)
        kcat_ref[:, KV:] = kr.astype(BF16)

    @pl.when(n > 0)
    def _():
        gates_ref[...] = jnp.dot(hb_ref[...], wb_ref[...], preferred_element_type=F32).astype(BF16)


def _q_kernel(cqn_ref, wq_ref, wk_ref, cos_ref, sa_ref, sb_ref, o_ref, *, H, NOPE, KV, scale):
    QW = NOPE + LANES
    DK = KV + LANES
    cqn = cqn_ref[...]
    for h in range(H):
        q = jnp.dot(cqn, wq_ref[:, h * QW:(h + 1) * QW], preferred_element_type=F32)
        q_lat = jnp.dot(q[:, :NOPE].astype(BF16), wk_ref[h], preferred_element_type=F32)
        q_rope = _rope(q[:, NOPE:], cos_ref[...], sa_ref[...], sb_ref[...])
        o_ref[:, h * DK:h * DK + KV] = (q_lat * scale).astype(BF16)
        o_ref[:, h * DK + KV:(h + 1) * DK] = (q_rope * scale).astype(BF16)


def _attn_kernel(*refs, H, KV, has_cache):
    if has_cache:
        q_ref, kn_ref, kc_ref, o_ref = refs
    else:
        q_ref, kn_ref, o_ref = refs
    DK = KV + LANES
    dn = (((1,), (1,)), ((), ()))
    kn = kn_ref[...]
    vn = kn[:, :KV]
    if has_cache:
        kc = kc_ref[...]
        vc = kc[:, :KV]
    for h in range(H):
        q = q_ref[:, h * DK:(h + 1) * DK]
        s = lax.dot_general(q, kn, dn, preferred_element_type=F32)
        m = jnp.max(s, axis=-1, keepdims=True)
        if has_cache:
            sc = lax.dot_general(q, kc, dn, preferred_element_type=F32)
            m = jnp.maximum(m, jnp.max(sc, axis=-1, keepdims=True))
        p = jnp.exp(s - m)
        l = jnp.sum(p, axis=-1, keepdims=True)
        o = jnp.dot(p.astype(BF16), vn, preferred_element_type=F32)
        if has_cache:
            pc = jnp.exp(sc - m)
            l = l + jnp.sum(pc, axis=-1, keepdims=True)
            o = o + jnp.dot(pc.astype(BF16), vc, preferred_element_type=F32)
        o_ref[:, h * KV:(h + 1) * KV] = (o / l).astype(BF16)


def _mix_kernel(op_ref, os_ref, gb_ref, gc_ref, u_ref, gcp_ref, up_ref, gcn_ref, un_ref,
                wuv_ref, wc_ref, wo_ref, g_ref, mod_ref, x_ref, out_ref,
                cat_ref, vs_ref, *, H, KV, VD, n_p_tiles, seq_p, seq_s, t_p):
    m = pl.program_id(0)
    tm = x_ref.shape[0]
    AW = H * VD
    is_p = m < n_p_tiles
    seq = jnp.where(is_p, seq_p, seq_s)
    row0 = m * tm - jnp.where(is_p, 0, t_p)

    hr = HALO_ROWS
    vs_ref[SUBLANES:SUBLANES + tm, :] = gc_ref[...].astype(F32) * u_ref[...].astype(F32)
    vs_ref[SUBLANES - 1:SUBLANES, :] = gcp_ref[hr - 1:hr, :].astype(F32) * up_ref[hr - 1:hr, :].astype(F32)
    vs_ref[SUBLANES + tm:SUBLANES + tm + 1, :] = gcn_ref[0:1, :].astype(F32) * un_ref[0:1, :].astype(F32)

    for r0 in range(0, tm, ROW_CHUNK):
        rows = slice(r0, r0 + ROW_CHUNK)
        for h in range(H):
            cols = slice(h * KV, (h + 1) * KV)
            o = jnp.where(is_p, op_ref[rows, cols], os_ref[rows, cols])
            a = jnp.dot(o, wuv_ref[h], preferred_element_type=F32)
            cat_ref[rows, h * VD:(h + 1) * VD] = a.astype(BF16)
        pos = jnp.bitwise_and(lax.broadcasted_iota(jnp.int32, (ROW_CHUNK, 1), 0) + (row0 + r0), seq - 1)
        v = vs_ref[SUBLANES + r0:SUBLANES + r0 + ROW_CHUNK, :]
        vp = jnp.where(pos == 0, 0.0, vs_ref[SUBLANES - 1 + r0:SUBLANES - 1 + r0 + ROW_CHUNK, :])
        vn = jnp.where(pos == seq - 1, 0.0, vs_ref[SUBLANES + 1 + r0:SUBLANES + 1 + r0 + ROW_CHUNK, :])
        conv = vp * wc_ref[0:1, :] + v * wc_ref[1:2, :] + vn * wc_ref[2:3, :]
        cat_ref[rows, AW:] = (gb_ref[rows, :].astype(F32) * conv).astype(BF16)
        y = jnp.dot(cat_ref[rows, :], wo_ref[...], preferred_element_type=F32)
        out_ref[rows, :] = x_ref[rows, :] + mod_ref[2:3, :] * (_rms(y) * g_ref[...])


def _modulate_ffn(x_ref, mod_ref, g_ref):
    h = _rms(x_ref[...]) * g_ref[...]
    return h * (1.0 + mod_ref[4:5, :]) + mod_ref[3:4, :]


def _swiglu_step(hb, wg_ref, wu_ref, wd_ref):
    g = jnp.dot(hb, wg_ref[...], preferred_element_type=F32)
    u = jnp.dot(hb, wu_ref[...], preferred_element_type=F32)
    a = (g / (1.0 + jnp.exp(-g))) * u
    return jnp.dot(a.astype(BF16), wd_ref[...], preferred_element_type=F32)


def _for_row_chunks(n_rows, chunk, fn):
    def body(c, carry):
        fn(pl.ds(pl.multiple_of(c * chunk, chunk), chunk))
        return carry
    lax.fori_loop(0, n_rows // chunk, body, 0)


def _ffn_kernel(x_ref, mod_ref, g2_ref, g3_ref, wg_ref, wu_ref, wd_ref, out_ref, hb_ref):
    f = pl.program_id(1)
    nf = pl.num_programs(1)
    tm = x_ref.shape[0]

    @pl.when(f == 0)
    def _():
        def prologue(rows):
            h = _rms(x_ref[rows, :]) * g2_ref[...]
            hb_ref[rows, :] = (h * (1.0 + mod_ref[4:5, :]) + mod_ref[3:4, :]).astype(BF16)
            out_ref[rows, :] = jnp.zeros((ROW_CHUNK, out_ref.shape[1]), F32)
        _for_row_chunks(tm, ROW_CHUNK, prologue)

    out_ref[...] += _swiglu_step(hb_ref[...], wg_ref, wu_ref, wd_ref)

    @pl.when(f == nf - 1)
    def _():
        def epilogue(rows):
            y = _rms(out_ref[rows, :]) * g3_ref[...]
            out_ref[rows, :] = x_ref[rows, :] + mod_ref[5:6, :] * y
        _for_row_chunks(tm, ROW_CHUNK, epilogue)


def _top2(logits, n_experts):
    lane = lax.broadcasted_iota(jnp.int32, logits.shape, 1)
    neg = jnp.float32(-jnp.inf)
    l1 = jnp.where(lane < n_experts, logits, neg)
    m1 = jnp.max(l1, axis=-1, keepdims=True)
    i1 = jnp.min(jnp.where(l1 == m1, lane, LANES), axis=-1, keepdims=True)
    l2 = jnp.where(lane == i1, neg, l1)
    m2 = jnp.max(l2, axis=-1, keepdims=True)
    i2 = jnp.min(jnp.where(l2 == m2, lane, LANES), axis=-1, keepdims=True)
    e2 = jnp.exp(m2 - m1)
    return lane, i1, i2, 1.0 / (1.0 + e2), e2 / (1.0 + e2)


def _route_kernel(x_ref, mod_ref, g2_ref, wr_ref, h_ref, ri_ref, rp_ref, cnt_ref, carry_ref,
                  *, n_experts):
    m = pl.program_id(0)
    tm = x_ref.shape[0]

    @pl.when(m == 0)
    def _():
        carry_ref[...] = jnp.zeros_like(carry_ref)

    h = _modulate_ffn(x_ref, mod_ref, g2_ref)
    h_ref[...] = h
    logits = jnp.dot(h, wr_ref[...], preferred_element_type=F32, precision=lax.Precision.HIGHEST)
    lane, i1, i2, p1, p2 = _top2(logits, n_experts)
    sel = jnp.logical_or(lane == i1, lane == i2)
    r = lax.broadcasted_iota(jnp.int32, (tm, tm), 0)
    c = lax.broadcasted_iota(jnp.int32, (tm, tm), 1)
    before = jnp.where(r > c, 1.0, 0.0).astype(BF16)
    rank = jnp.dot(before, jnp.where(sel, 1.0, 0.0).astype(BF16), preferred_element_type=F32)
    rank = rank + carry_ref[0:1, :]
    r1 = jnp.sum(jnp.where(lane == i1, rank, 0.0), axis=-1, keepdims=True).astype(jnp.int32)
    r2 = jnp.sum(jnp.where(lane == i2, rank, 0.0), axis=-1, keepdims=True).astype(jnp.int32)
    carry_ref[0:1, :] = carry_ref[0:1, :] + jnp.sum(jnp.where(sel, 1.0, 0.0), axis=0, keepdims=True)
    ri_ref[...] = jnp.where(lane == 0, i1, jnp.where(lane == 1, i2,
                            jnp.where(lane == 2, r1, jnp.where(lane == 3, r2, 0))))
    rp_ref[...] = jnp.where(lane == 0, p1, jnp.where(lane == 1, p2, 0.0))
    cnt_ref[...] = carry_ref[...]


def _row_copy(src_ref, src_row, dst_ref, dst_row, sem):
    return pltpu.make_async_copy(src_ref.at[pl.ds(src_row, 1), :], dst_ref.at[pl.ds(dst_row, 1), :], sem)


def _dispatch_kernel(pos_ref, pad_start_ref, pad_cnt_ref, h_ref, hs_ref, zero_ref, sem,
                     *, n_experts):
    m = pl.program_id(0)
    tm = h_ref.shape[0]
    base = m * tm

    def wait_one(i, carry):
        _row_copy(h_ref, 0, hs_ref, 0, sem).wait()
        return carry

    def issue(r, carry):
        t = base + r
        _row_copy(h_ref, r, hs_ref, pos_ref[2 * t], sem).start()
        _row_copy(h_ref, r, hs_ref, pos_ref[2 * t + 1], sem).start()
        return carry

    lax.fori_loop(0, tm, issue, 0, unroll=8)
    lax.fori_loop(0, 2 * tm, wait_one, 0, unroll=8)

    @pl.when(m == 0)
    def _():
        zero_ref[...] = jnp.zeros_like(zero_ref)
        for e in range(n_experts):
            start, cnt = pad_start_ref[e], pad_cnt_ref[e]

            def pad_issue(i, carry):
                _row_copy(zero_ref, 0, hs_ref, start + i, sem).start()
                return carry

            lax.fori_loop(0, cnt, pad_issue, 0)
            lax.fori_loop(0, cnt, wait_one, 0)


def _for_live_rows(nrows, tm, fn):
    for n in range(MOE_SUB, tm + 1, MOE_SUB):
        pl.when(nrows == n)(functools.partial(fn, n))


def _cast_kernel(tix_ref, nrows_ref, x_ref, o_ref):
    def cast(n):
        o_ref[0:n, :] = x_ref[0:n, :].astype(BF16)
    _for_live_rows(nrows_ref[pl.program_id(0)], x_ref.shape[0], cast)


def _moe_up_kernel(te_ref, tix_ref, first_ref, nrows_ref, hs_ref, wg_ref, wu_ref, a_ref,
                   wgb_ref, wub_ref):
    m = pl.program_id(1)

    @pl.when(first_ref[m] == 1)
    def _():
        wgb_ref[...] = wg_ref[...].astype(BF16)
        wub_ref[...] = wu_ref[...].astype(BF16)

    def up(n):
        hb = hs_ref[0:n, :]
        g = jnp.dot(hb, wgb_ref[...], preferred_element_type=F32)
        u = jnp.dot(hb, wub_ref[...], preferred_element_type=F32)
        a_ref[0:n, :] = ((g / (1.0 + jnp.exp(-g))) * u).astype(BF16)
    _for_live_rows(nrows_ref[m], hs_ref.shape[0], up)


def _moe_down_kernel(te_ref, tix_ref, first_ref, nrows_ref, a_ref, wd_ref, ys_ref, wdb_ref):
    m = pl.program_id(1)

    @pl.when(first_ref[m] == 1)
    def _():
        wdb_ref[...] = wd_ref[...].astype(BF16)

    def down(n):
        ys_ref[0:n, :] = jnp.dot(a_ref[0:n, :], wdb_ref[...], preferred_element_type=F32)
    _for_live_rows(nrows_ref[m], a_ref.shape[0], down)


def _combine_kernel(pos_ref, ys_ref, rp_ref, x_ref, mod_ref, g3_ref, *rest, n_split):
    if n_split is None:
        out_ref, buf_ref, sem = rest
    else:
        out_a_ref, out_b_ref, buf_ref, sem = rest
    m = pl.program_id(0)
    n = pl.num_programs(0)
    tm = x_ref.shape[0]

    def issue_tile(tile, slot):
        def issue(r, carry):
            t = tile * tm + r
            _row_copy(ys_ref, pos_ref[2 * t], buf_ref.at[slot], r, sem.at[slot]).start()
            _row_copy(ys_ref, pos_ref[2 * t + 1], buf_ref.at[slot], tm + r, sem.at[slot]).start()
            return carry
        lax.fori_loop(0, tm, issue, 0, unroll=8)

    @pl.when(m == 0)
    def _():
        issue_tile(0, 0)

    @pl.when(m + 1 < n)
    def _():
        issue_tile(m + 1, (m + 1) % 2)

    slot = m % 2

    def wait(r, carry):
        _row_copy(ys_ref, 0, buf_ref.at[slot], 0, sem.at[slot]).wait()
        return carry

    lax.fori_loop(0, 2 * tm, wait, 0, unroll=8)

    def result(rows, second):
        y = rp_ref[rows, 0:1] * buf_ref[slot, rows, :] + rp_ref[rows, 1:2] * buf_ref[slot, second, :]
        return x_ref[rows, :] + mod_ref[5:6, :] * (_rms(y) * g3_ref[...])

    def chunked(dst_ref):
        def body(c, carry):
            r0 = pl.multiple_of(c * ROW_CHUNK, ROW_CHUNK)
            dst_ref[pl.ds(r0, ROW_CHUNK), :] = result(pl.ds(r0, ROW_CHUNK), pl.ds(tm + r0, ROW_CHUNK))
            return carry
        lax.fori_loop(0, tm // ROW_CHUNK, body, 0)

    if n_split is None:
        chunked(out_ref)
    else:
        @pl.when(m < n_split)
        def _():
            chunked(out_a_ref)

        @pl.when(m >= n_split)
        def _():
            chunked(out_b_ref)


def _moe_layer(x, mods, g4, wr, w_e_gate, w_e_up, w_e_down, *, l, i, tm, tf, mrow, n_split=None):
    T, D = x.shape
    NE, EFF = w_e_gate.shape[1], w_e_gate.shape[3]
    n_tiles = T // tm
    tmu = 2 * tm
    n_mt = (TOP_K * T) // tmu + NE
    Tp = n_mt * tmu
    tn = 512
    assert D % tn == 0 and EFF % tf == 0 and (TOP_K * T) % tmu == 0
    sds = jax.ShapeDtypeStruct

    h, ri, rp, cnt = pl.pallas_call(
        functools.partial(_route_kernel, n_experts=NE),
        grid=(n_tiles,),
        in_specs=[
            pl.BlockSpec((tm, D), lambda m: (m, 0)),
            pl.BlockSpec((None, None, N_MOD, D), lambda m: (l, mrow(m), 0, 0)),
            pl.BlockSpec((None, 1, D), lambda m: (4 * l + 2, 0, 0)),
            pl.BlockSpec((None, D, LANES), lambda m: (i, 0, 0)),
        ],
        out_specs=[
            pl.BlockSpec((tm, D), lambda m: (m, 0)),
            pl.BlockSpec((tm, LANES), lambda m: (m, 0)),
            pl.BlockSpec((tm, LANES), lambda m: (m, 0)),
            pl.BlockSpec((SUBLANES, LANES), lambda m: (0, 0)),
        ],
        out_shape=[sds((T, D), F32), sds((T, LANES), jnp.int32), sds((T, LANES), F32),
                   sds((SUBLANES, LANES), F32)],
        scratch_shapes=[pltpu.VMEM((SUBLANES, LANES), F32)],
        compiler_params=_params(("arbitrary",)),
        name="moe_route",
    )(x, mods, g4, wr)

    cnt = cnt[0, :NE].astype(jnp.int32)
    pc = (cnt + tmu - 1) // tmu * tmu
    starts = jnp.cumsum(pc) - pc
    pos = jnp.stack([starts[ri[:, 0]] + ri[:, 2], starts[ri[:, 1]] + ri[:, 3]], axis=1).reshape(-1)
    zero_fill = (-cnt) % MOE_SUB

    def tile_list(t, n_max):
        per = (cnt + t - 1) // t
        upto = jnp.cumsum(per)
        ar = jnp.arange(n_max, dtype=jnp.int32)
        k = jnp.minimum(ar, upto[-1] - 1)
        te = jnp.minimum(jnp.sum(k[:, None] >= upto[None, :], axis=1), NE - 1).astype(jnp.int32)
        off = k - (upto[te] - per[te])
        tix = (starts[te] // t + off).astype(jnp.int32)
        first = jnp.logical_or(te != jnp.roll(te, 1), ar == 0).astype(jnp.int32)
        live = jnp.clip(cnt[te] - off * t, 0, t)
        nrows = jnp.where(ar < upto[-1], (live + MOE_SUB - 1) // MOE_SUB * MOE_SUB, 0).astype(jnp.int32)
        return te, tix, first, nrows

    te_u, tix_u, first_u, nrows_u = tile_list(tmu, n_mt)
    n_mt_d = (TOP_K * T) // tm + NE
    te_d, tix_d, first_d, nrows_d = tile_list(tm, n_mt_d)

    hs = pl.pallas_call(
        functools.partial(_dispatch_kernel, n_experts=NE),
        grid_spec=pltpu.PrefetchScalarGridSpec(
            num_scalar_prefetch=3, grid=(n_tiles,),
            in_specs=[pl.BlockSpec((tm, D), lambda m, pos, ps, pn: (m, 0))],
            out_specs=pl.BlockSpec(memory_space=pl.ANY),
            scratch_shapes=[pltpu.VMEM((SUBLANES, D), F32), pltpu.SemaphoreType.DMA(())]),
        out_shape=sds((Tp, D), F32),
        compiler_params=_params(("arbitrary",)),
        name="moe_dispatch",
    )(pos, starts + cnt, zero_fill, h)

    hsb = pl.pallas_call(
        _cast_kernel,
        grid_spec=pltpu.PrefetchScalarGridSpec(
            num_scalar_prefetch=2, grid=(n_mt_d,),
            in_specs=[pl.BlockSpec((tm, D), lambda m, tix, nr: (tix[m], 0))],
            out_specs=pl.BlockSpec((tm, D), lambda m, tix, nr: (tix[m], 0))),
        out_shape=sds((Tp, D), BF16),
        compiler_params=_params(("arbitrary",)),
        name="moe_cast",
    )(tix_d, nrows_d, hs)

    a = pl.pallas_call(
        _moe_up_kernel,
        grid_spec=pltpu.PrefetchScalarGridSpec(
            num_scalar_prefetch=4, grid=(EFF // tf, n_mt),
            in_specs=[
                pl.BlockSpec((tmu, D), lambda f, m, te, tix, fi, nu: (tix[m], 0)),
                pl.BlockSpec((None, None, D, tf), lambda f, m, te, tix, fi, nu: (i, te[m], 0, f)),
                pl.BlockSpec((None, None, D, tf), lambda f, m, te, tix, fi, nu: (i, te[m], 0, f)),
            ],
            out_specs=pl.BlockSpec((tmu, tf), lambda f, m, te, tix, fi, nu: (tix[m], f)),
            scratch_shapes=[pltpu.VMEM((D, tf), BF16), pltpu.VMEM((D, tf), BF16)]),
        out_shape=sds((Tp, EFF), BF16),
        compiler_params=_params(("arbitrary", "arbitrary")),
        name="moe_up",
    )(te_u, tix_u, first_u, nrows_u, hsb, w_e_gate, w_e_up)

    ys = pl.pallas_call(
        _moe_down_kernel,
        grid_spec=pltpu.PrefetchScalarGridSpec(
            num_scalar_prefetch=4, grid=(D // tn, n_mt_d),
            in_specs=[
                pl.BlockSpec((tm, EFF), lambda n, m, te, tix, fi, nu: (tix[m], 0)),
                pl.BlockSpec((None, None, EFF, tn), lambda n, m, te, tix, fi, nu: (i, te[m], 0, n)),
            ],
            out_specs=pl.BlockSpec((tm, tn), lambda n, m, te, tix, fi, nu: (tix[m], n)),
            scratch_shapes=[pltpu.VMEM((EFF, tn), BF16)]),
        out_shape=sds((Tp, D), F32),
        compiler_params=_params(("arbitrary", "arbitrary")),
        name="moe_down",
    )(te_d, tix_d, first_d, nrows_d, a, w_e_down)

    if n_split is None:
        out_specs = pl.BlockSpec((tm, D), lambda m, pos: (m, 0))
        out_shape = sds((T, D), F32)
    else:
        out_specs = [pl.BlockSpec((tm, D), lambda m, pos: (jnp.minimum(m, n_split - 1), 0)),
                     pl.BlockSpec((tm, D), lambda m, pos: (jnp.maximum(m - n_split, 0), 0))]
        out_shape = [sds((n_split * tm, D), F32), sds((T - n_split * tm, D), F32)]
    return pl.pallas_call(
        functools.partial(_combine_kernel, n_split=n_split),
        grid_spec=pltpu.PrefetchScalarGridSpec(
            num_scalar_prefetch=1, grid=(n_tiles,),
            in_specs=[
                pl.BlockSpec(memory_space=pl.ANY),
                pl.BlockSpec((tm, LANES), lambda m, pos: (m, 0)),
                pl.BlockSpec((tm, D), lambda m, pos: (m, 0)),
                pl.BlockSpec((None, None, N_MOD, D), lambda m, pos: (l, mrow(m), 0, 0)),
                pl.BlockSpec((None, 1, D), lambda m, pos: (4 * l + 3, 0, 0)),
            ],
            out_specs=out_specs,
            scratch_shapes=[pltpu.VMEM((2, 2 * tm, D), F32), pltpu.SemaphoreType.DMA((2,))]),
        out_shape=out_shape,
        compiler_params=_params(("arbitrary",)),
        name="moe_combine",
    )(pos, ys, rp, x, mods, g4)


def kernel(x_prompt, x_sample, cache_ckv, cache_krope, c, c_ctx, w_in, g_q, g_kv, w_uq, w_uk, w_uv,
           w_conv, w_o, w_ada, b_ada, g_sand, w_ff_gate, w_ff_up, w_ff_down, w_router,
           w_e_gate, w_e_up, w_e_down):
    B, N, D = x_prompt.shape
    DB, DN, _ = x_sample.shape
    L = w_in.shape[0]
    QL, KV = g_q.shape[1], g_kv.shape[1]
    H, NOPE = w_uk.shape[2], w_uk.shape[3]
    ROPE = w_uq.shape[3] - NOPE
    VD = w_uv.shape[3]
    CONV = w_conv.shape[2]
    AW = H * VD
    PAST = cache_ckv.shape[2]
    FF = w_ff_gate.shape[2]
    NE, EFF = w_e_gate.shape[1], w_e_gate.shape[3]
    T_P, T_S = B * N, DB * DN
    T = T_P + T_S
    DK = KV + LANES
    NA = 1024
    tm = TM
    assert ROPE == 64 and 2 * ROPE == LANES and NOPE % LANES == 0 and KV % LANES == 0 and QL % LANES == 0
    assert QL + KV + LANES <= NA and CONV % LANES == 0 and AW + CONV == w_o.shape[1]
    assert T_P % tm == 0 and DN % tm == 0 and tm % N == 0 and T_P % DN == 0
    assert N & (N - 1) == 0 and DN & (DN - 1) == 0 and DN % GRID_W == 0
    assert NE <= LANES and DB + 1 <= SUBLANES
    n_p_tiles = T_P // tm
    tiles_per_s = DN // tm
    n_tiles = T // tm
    scale = float(NOPE + ROPE) ** -0.5
    tf = 512
    assert FF % tf == 0 and EFF % tf == 0

    def mrow(m):
        return jnp.where(m < n_p_tiles, 0, 1 + (m - n_p_tiles) // tiles_per_s)

    def trow(m):
        return jnp.where(m < n_p_tiles, 0, 1 + (m - n_p_tiles) % tiles_per_s)

    t = jnp.arange(DN)
    inv = 1.0 / (ROPE_THETA ** (jnp.arange(0, ROPE // 2, 2, dtype=F32) / (ROPE // 2)))
    ar = (t // GRID_W).astype(F32)[:, None] * inv
    ac = (t % GRID_W).astype(F32)[:, None] * inv
    ang = jnp.concatenate([ar, ar, ac, ac] * 2, axis=-1)
    first = (jnp.arange(LANES) % (ROPE // 2)) < (ROPE // 4)
    cos_t = jnp.concatenate([jnp.ones((tm, LANES), F32), jnp.cos(ang)], axis=0)
    sin = jnp.sin(ang)
    sa_t = jnp.concatenate([jnp.zeros((tm, LANES), F32), jnp.where(first, -sin, 0.0)], axis=0)
    sb_t = jnp.concatenate([jnp.zeros((tm, LANES), F32), jnp.where(first, 0.0, sin)], axis=0)

    s1, s2 = QL + KV + ROPE, QL + KV + ROPE + 3 * CONV
    w_in_b = w_in.astype(BF16)
    w_a = jnp.pad(w_in_b[:, :, :s1], ((0, 0), (0, 0), (0, NA - s1)))
    w_b = w_in_b[:, :, s1:s2]
    wq = jnp.pad(w_uq, ((0, 0), (0, 0), (0, 0), (0, LANES - ROPE))).astype(BF16)
    wq = wq.reshape(L, QL, H * (NOPE + LANES))
    wk = w_uk.transpose(0, 2, 3, 1).astype(BF16)
    wuv = w_uv.transpose(0, 2, 1, 3).astype(BF16)
    wo = w_o.astype(BF16)
    wfg, wfu, wfd = w_ff_gate.astype(BF16), w_ff_up.astype(BF16), w_ff_down.astype(BF16)
    wr = jnp.pad(w_router, ((0, 0), (0, 0), (0, LANES - NE)))
    kcache = jnp.concatenate(
        [cache_ckv, cache_krope, jnp.zeros(cache_krope.shape[:-1] + (LANES - ROPE,), F32)],
        axis=-1).astype(BF16)

    cond8 = jnp.concatenate([c_ctx[None, :], c, jnp.zeros((SUBLANES - 1 - DB, D), F32)], axis=0)
    mods = _ada(cond8, w_ada, b_ada)[:, :1 + DB].reshape(L, 1 + DB, N_MOD, D)

    x = jnp.concatenate([x_prompt.reshape(T_P, D), x_sample.reshape(T_S, D)], axis=0)

    mod_spec2 = lambda l: pl.BlockSpec((None, None, N_MOD, D), lambda m, n: (l, mrow(m), 0, 0))
    row_spec2 = lambda w: pl.BlockSpec((tm, w), lambda m, n: (m, 0))
    tab_spec2 = pl.BlockSpec((tm, LANES), lambda m, n: (trow(m), 0))
    tab_spec1 = pl.BlockSpec((tm, LANES), lambda m: (trow(m), 0))

    ckv_list, kr_list = [], []
    for l in range(L):
        cqn, ckvn, kr, kcat, gates = pl.pallas_call(
            functools.partial(_inproj_kernel, QL=QL, KV=KV, ROPE=ROPE),
            grid=(n_tiles, 1 + 3),
            in_specs=[
                row_spec2(D),
                mod_spec2(l),
                pl.BlockSpec((None, 1, D), lambda m, n: (4 * l, 0, 0)),
                pl.BlockSpec((None, 1, QL), lambda m, n: (l, 0, 0)),
                pl.BlockSpec((None, 1, KV), lambda m, n: (l, 0, 0)),
                tab_spec2, tab_spec2, tab_spec2,
                pl.BlockSpec((None, D, NA), lambda m, n: (l, 0, 0)),
                pl.BlockSpec((None, D, CONV), lambda m, n: (l, 0, jnp.maximum(n - 1, 0))),
            ],
            out_specs=[
                row_spec2(QL), row_spec2(KV), row_spec2(ROPE), row_spec2(DK),
                pl.BlockSpec((tm, CONV), lambda m, n: (m, jnp.maximum(n - 1, 0))),
            ],
            out_shape=[
                jax.ShapeDtypeStruct((T, QL), BF16),
                jax.ShapeDtypeStruct((T, KV), F32),
                jax.ShapeDtypeStruct((T, ROPE), F32),
                jax.ShapeDtypeStruct((T, DK), BF16),
                jax.ShapeDtypeStruct((T, 3 * CONV), BF16),
            ],
            scratch_shapes=[pltpu.VMEM((tm, D), BF16)],
            compiler_params=_params(("arbitrary", "arbitrary")),
            name="in_proj",
        )(x, mods, g_sand.reshape(4 * L, 1, D), g_q.reshape(L, 1, QL), g_kv.reshape(L, 1, KV),
          cos_t, sa_t, sb_t, w_a, w_b)
        ckv_list.append(ckvn[:T_P].reshape(B, N, KV))
        kr_list.append(kr[:T_P].reshape(B, N, ROPE))

        qcat = pl.pallas_call(
            functools.partial(_q_kernel, H=H, NOPE=NOPE, KV=KV, scale=scale),
            grid=(n_tiles,),
            in_specs=[
                pl.BlockSpec((tm, QL), lambda m: (m, 0)),
                pl.BlockSpec((None, QL, H * (NOPE + LANES)), lambda m: (l, 0, 0)),
                pl.BlockSpec((None, H, NOPE, KV), lambda m: (l, 0, 0, 0)),
                tab_spec1, tab_spec1, tab_spec1,
            ],
            out_specs=pl.BlockSpec((tm, H * DK), lambda m: (m, 0)),
            out_shape=jax.ShapeDtypeStruct((T, H * DK), BF16),
            compiler_params=_params(("arbitrary",)),
            name="q_proj",
        )(cqn, wq, wk, cos_t, sa_t, sb_t)

        o_p = pl.pallas_call(
            functools.partial(_attn_kernel, H=H, KV=KV, has_cache=False),
            grid=(B,),
            in_specs=[
                pl.BlockSpec((N, H * DK), lambda b: (b, 0)),
                pl.BlockSpec((N, DK), lambda b: (b, 0)),
            ],
            out_specs=pl.BlockSpec((N, H * KV), lambda b: (b, 0)),
            out_shape=jax.ShapeDtypeStruct((T_P, H * KV), BF16),
            compiler_params=_params(("arbitrary",)),
            name="attn_prompt",
        )(qcat, kcat)
        tq = 256
        qs_per = DN // tq
        o_s = pl.pallas_call(
            functools.partial(_attn_kernel, H=H, KV=KV, has_cache=True),
            grid=(DB, qs_per),
            in_specs=[
                pl.BlockSpec((tq, H * DK), lambda b, i: (T_P // tq + b * qs_per + i, 0)),
                pl.BlockSpec((DN, DK), lambda b, i: (T_P // DN + b, 0)),
                pl.BlockSpec((None, None, PAST, DK), lambda b, i: (b, l, 0, 0)),
            ],
            out_specs=pl.BlockSpec((tq, H * KV), lambda b, i: (b * qs_per + i, 0)),
            out_shape=jax.ShapeDtypeStruct((T_S, H * KV), BF16),
            compiler_params=_params(("arbitrary", "arbitrary")),
            name="attn_sample",
        )(qcat, kcat, kcache)

        r8 = tm // HALO_ROWS
        prev8 = lambda m: jnp.maximum(m * r8 - 1, 0)
        next8 = lambda m: jnp.minimum((m + 1) * r8, T // HALO_ROWS - 1)
        x = pl.pallas_call(
            functools.partial(_mix_kernel, H=H, KV=KV, VD=VD, n_p_tiles=n_p_tiles,
                              seq_p=N, seq_s=DN, t_p=T_P),
            grid=(n_tiles,),
            in_specs=[
                pl.BlockSpec((tm, H * KV), lambda m: (jnp.minimum(m, n_p_tiles - 1), 0)),
                pl.BlockSpec((tm, H * KV), lambda m: (jnp.maximum(m - n_p_tiles, 0), 0)),
                pl.BlockSpec((tm, CONV), lambda m: (m, 0)),
                pl.BlockSpec((tm, CONV), lambda m: (m, 1)),
                pl.BlockSpec((tm, CONV), lambda m: (m, 2)),
                pl.BlockSpec((HALO_ROWS, CONV), lambda m: (prev8(m), 1)),
                pl.BlockSpec((HALO_ROWS, CONV), lambda m: (prev8(m), 2)),
                pl.BlockSpec((HALO_ROWS, CONV), lambda m: (next8(m), 1)),
                pl.BlockSpec((HALO_ROWS, CONV), lambda m: (next8(m), 2)),
                pl.BlockSpec((None, H, KV, VD), lambda m: (l, 0, 0, 0)),
                pl.BlockSpec((None, 3, CONV), lambda m: (l, 0, 0)),
                pl.BlockSpec((None, AW + CONV, D), lambda m: (l, 0, 0)),
                pl.BlockSpec((None, 1, D), lambda m: (4 * l + 1, 0, 0)),
                pl.BlockSpec((None, None, N_MOD, D), lambda m: (l, mrow(m), 0, 0)),
                pl.BlockSpec((tm, D), lambda m: (m, 0)),
            ],
            out_specs=pl.BlockSpec((tm, D), lambda m: (m, 0)),
            out_shape=jax.ShapeDtypeStruct((T, D), F32),
            scratch_shapes=[pltpu.VMEM((tm, AW + CONV), BF16),
                            pltpu.VMEM((tm + 2 * SUBLANES, CONV), F32)],
            compiler_params=_params(("arbitrary",)),
            name="mixer_out",
        )(o_p, o_s, gates, gates, gates, gates, gates, gates, gates, wuv, w_conv, wo,
          g_sand.reshape(4 * L, 1, D), mods, x)

        i = l // 2
        if l % 2 == 0:
            tmf = TM_FFN if (T_P % TM_FFN == 0 and DN % TM_FFN == 0) else tm
            mrow_f = lambda m: jnp.where(m < T_P // tmf, 0, 1 + (m - T_P // tmf) // (DN // tmf))
            x = pl.pallas_call(
                _ffn_kernel,
                grid=(T // tmf, FF // tf),
                in_specs=[
                    pl.BlockSpec((tmf, D), lambda m, f: (m, 0), pipeline_mode=pl.Buffered(1)),
                    pl.BlockSpec((None, None, N_MOD, D), lambda m, f: (l, mrow_f(m), 0, 0)),
                    pl.BlockSpec((None, 1, D), lambda m, f: (4 * l + 2, 0, 0)),
                    pl.BlockSpec((None, 1, D), lambda m, f: (4 * l + 3, 0, 0)),
                    pl.BlockSpec((None, D, tf), lambda m, f: (i, 0, f)),
                    pl.BlockSpec((None, D, tf), lambda m, f: (i, 0, f)),
                    pl.BlockSpec((None, tf, D), lambda m, f: (i, f, 0)),
                ],
                out_specs=pl.BlockSpec((tmf, D), lambda m, f: (m, 0)),
                out_shape=jax.ShapeDtypeStruct((T, D), F32),
                scratch_shapes=[pltpu.VMEM((tmf, D), BF16)],
                compiler_params=_params(("arbitrary", "arbitrary")),
                name="ffn_dense",
            )(x, mods, g_sand.reshape(4 * L, 1, D), g_sand.reshape(4 * L, 1, D), wfg, wfu, wfd)
        else:
            x = _moe_layer(x, mods, g_sand.reshape(4 * L, 1, D), wr, w_e_gate, w_e_up, w_e_down,
                           l=l, i=i, tm=tm, tf=tf, mrow=mrow,
                           n_split=n_p_tiles if l == L - 1 else None)

    x_p, x_s = x if isinstance(x, (list, tuple)) else (x[:T_P], x[T_P:])
    y_prompt = x_p.reshape(B, N, D)
    y_sample = x_s.reshape(DB, DN, D)
    new_ckv = jnp.stack(ckv_list, axis=1)
    new_krope = jnp.stack(kr_list, axis=1)
    return (y_prompt, y_sample, new_ckv, new_krope)
```

```python
import functools

import jax
import jax.numpy as jnp
from jax import lax
from jax.experimental import pallas as pl
from jax.experimental.pallas import tpu as pltpu

GRID_W = 64
ROPE_THETA = 10000.0
EPS = 1e-6
N_MOD = 6
TOP_K = 2

LANES = 128
SUBLANES = 8
TM = 512
TM_FFN = 1024
ROW_CHUNK = 256
HALO_ROWS = 16
MOE_SUB = 256
VMEM_LIMIT = 56 * 1024 * 1024

BF16 = jnp.bfloat16
F32 = jnp.float32


def _params(sem, vmem=VMEM_LIMIT):
    return pltpu.CompilerParams(dimension_semantics=sem, vmem_limit_bytes=vmem)


def _rms(x):
    return x * lax.rsqrt(jnp.mean(x * x, axis=-1, keepdims=True) + EPS)


def _rope(x, cos, sa, sb):
    return x * cos + pltpu.roll(x, LANES - 16, 1) * sa + pltpu.roll(x, 16, 1) * sb


def _ada_kernel(c_ref, w_ref, b_ref, o_ref):
    c = c_ref[...]
    s = c / (1.0 + jnp.exp(-c))
    o_ref[...] = jnp.dot(s.astype(BF16), w_ref[...].astype(BF16),
                         preferred_element_type=F32) + b_ref[...]


def _ada(cond8, w_ada, b_ada):
    L, D, N6 = w_ada.shape
    tn = 1024 if N6 % 1024 == 0 else N6
    return pl.pallas_call(
        _ada_kernel,
        grid=(L, N6 // tn),
        in_specs=[
            pl.BlockSpec((SUBLANES, D), lambda l, n: (0, 0)),
            pl.BlockSpec((None, D, tn), lambda l, n: (l, 0, n)),
            pl.BlockSpec((None, 1, tn), lambda l, n: (l, 0, n)),
        ],
        out_specs=pl.BlockSpec((None, SUBLANES, tn), lambda l, n: (l, 0, n)),
        out_shape=jax.ShapeDtypeStruct((L, SUBLANES, N6), F32),
        compiler_params=_params(("arbitrary", "arbitrary")),
        name="ada_mod",
    )(cond8, w_ada, b_ada.reshape(L, 1, N6))


def _inproj_kernel(x_ref, mod_ref, g_ref, gq_ref, gkv_ref, cos_ref, sa_ref, sb_ref,
                   wa_ref, wb_ref, cqn_ref, ckv_ref, kr_ref, kcat_ref, gates_ref,
                   hb_ref, *, QL, KV, ROPE):
    n = pl.program_id(1)

    @pl.when(n == 0)
    def _():
        h = _rms(x_ref[...]) * g_ref[...]
        h = h * (1.0 + mod_ref[1:2, :]) + mod_ref[0:1, :]
        hb_ref[...] = h.astype(BF16)
        acc = jnp.dot(hb_ref[...], wa_ref[...], preferred_element_type=F32)
        cqn_ref[...] = (_rms(acc[:, :QL]) * gq_ref[...]).astype(BF16)
        ckv = _rms(acc[:, QL:QL + KV]) * gkv_ref[...]
        ckv_ref[...] = ckv
        kr = _rope(acc[:, QL + KV:QL + KV + LANES], cos_ref[...], sa_ref[...], sb_ref[...])
        kr_ref[...] = kr[:, :ROPE]
        kcat_ref[:, :KV] = ckv.astype(BF16)
        kcat_ref[:, KV:] = kr.astype(BF16)

    @pl.when(n > 0)
    def _():
        gates_ref[...] = jnp.dot(hb_ref[...], wb_ref[...], preferred_element_type=F32).astype(BF16)


def _q_kernel(cqn_ref, wq_ref, wk_ref, cos_ref, sa_ref, sb_ref, o_ref, *, H, NOPE, KV, scale):
    QW = NOPE + LANES
    DK = KV + LANES
    cqn = cqn_ref[...]
    for h in range(H):
        q = jnp.dot(cqn, wq_ref[:, h * QW:(h + 1) * QW], preferred_element_type=F32)
        q_lat = jnp.dot(q[:, :NOPE].astype(BF16), wk_ref[h], preferred_element_type=F32)
        q_rope = _rope(q[:, NOPE:], cos_ref[...], sa_ref[...], sb_ref[...])
        o_ref[:, h * DK:h * DK + KV] = (q_lat * scale).astype(BF16)
        o_ref[:, h * DK + KV:(h + 1) * DK] = (q_rope * scale).astype(BF16)


def _attn_kernel(*refs, H, KV, has_cache):
    if has_cache:
        q_ref, kn_ref, kc_ref, o_ref = refs
    else:
        q_ref, kn_ref, o_ref = refs
    DK = KV + LANES
    dn = (((1,), (1,)), ((), ()))
    kn = kn_ref[...]
    vn = kn[:, :KV]
    if has_cache:
        kc = kc_ref[...]
        vc = kc[:, :KV]
    for h in range(H):
        q = q_ref[:, h * DK:(h + 1) * DK]
        s = lax.dot_general(q, kn, dn, preferred_element_type=F32)
        m = jnp.max(s, axis=-1, keepdims=True)
        if has_cache:
            sc = lax.dot_general(q, kc, dn, preferred_element_type=F32)
            m = jnp.maximum(m, jnp.max(sc, axis=-1, keepdims=True))
        p = jnp.exp(s - m)
        l = jnp.sum(p, axis=-1, keepdims=True)
        o = jnp.dot(p.astype(BF16), vn, preferred_element_type=F32)
        if has_cache:
            pc = jnp.exp(sc - m)
            l = l + jnp.sum(pc, axis=-1, keepdims=True)
            o = o + jnp.dot(pc.astype(BF16), vc, preferred_element_type=F32)
        o_ref[:, h * KV:(h + 1) * KV] = (o / l).astype(BF16)


def _mix_kernel(op_ref, os_ref, gb_ref, gc_ref, u_ref, gcp_ref, up_ref, gcn_ref, un_ref,
                wuv_ref, wc_ref, wo_ref, g_ref, mod_ref, x_ref, out_ref,
                cat_ref, vs_ref, *, H, KV, VD, n_p_tiles, seq_p, seq_s, t_p):
    m = pl.program_id(0)
    tm = x_ref.shape[0]
    AW = H * VD
    is_p = m < n_p_tiles
    seq = jnp.where(is_p, seq_p, seq_s)
    row0 = m * tm - jnp.where(is_p, 0, t_p)

    hr = HALO_ROWS
    vs_ref[SUBLANES:SUBLANES + tm, :] = gc_ref[...].astype(F32) * u_ref[...].astype(F32)
    vs_ref[SUBLANES - 1:SUBLANES, :] = gcp_ref[hr - 1:hr, :].astype(F32) * up_ref[hr - 1:hr, :].astype(F32)
    vs_ref[SUBLANES + tm:SUBLANES + tm + 1, :] = gcn_ref[0:1, :].astype(F32) * un_ref[0:1, :].astype(F32)

    for r0 in range(0, tm, ROW_CHUNK):
        rows = slice(r0, r0 + ROW_CHUNK)
        for h in range(H):
            cols = slice(h * KV, (h + 1) * KV)
            o = jnp.where(is_p, op_ref[rows, cols], os_ref[rows, cols])
            a = jnp.dot(o, wuv_ref[h], preferred_element_type=F32)
            cat_ref[rows, h * VD:(h + 1) * VD] = a.astype(BF16)
        pos = jnp.bitwise_and(lax.broadcasted_iota(jnp.int32, (ROW_CHUNK, 1), 0) + (row0 + r0), seq - 1)
        v = vs_ref[SUBLANES + r0:SUBLANES + r0 + ROW_CHUNK, :]
        vp = jnp.where(pos == 0, 0.0, vs_ref[SUBLANES - 1 + r0:SUBLANES - 1 + r0 + ROW_CHUNK, :])
        vn = jnp.where(pos == seq - 1, 0.0, vs_ref[SUBLANES + 1 + r0:SUBLANES + 1 + r0 + ROW_CHUNK, :])
        conv = vp * wc_ref[0:1, :] + v * wc_ref[1:2, :] + vn * wc_ref[2:3, :]
        cat_ref[rows, AW:] = (gb_ref[rows, :].astype(F32) * conv).astype(BF16)
        y = jnp.dot(cat_ref[rows, :], wo_ref[...], preferred_element_type=F32)
        out_ref[rows, :] = x_ref[rows, :] + mod_ref[2:3, :] * (_rms(y) * g_ref[...])


def _modulate_ffn(x_ref, mod_ref, g_ref):
    h = _rms(x_ref[...]) * g_ref[...]
    return h * (1.0 + mod_ref[4:5, :]) + mod_ref[3:4, :]


def _swiglu_step(hb, wg_ref, wu_ref, wd_ref):
    g = jnp.dot(hb, wg_ref[...], preferred_element_type=F32)
    u = jnp.dot(hb, wu_ref[...], preferred_element_type=F32)
    a = (g / (1.0 + jnp.exp(-g))) * u
    return jnp.dot(a.astype(BF16), wd_ref[...], preferred_element_type=F32)


def _for_row_chunks(n_rows, chunk, fn):
    def body(c, carry):
        fn(pl.ds(pl.multiple_of(c * chunk, chunk), chunk))
        return carry
    lax.fori_loop(0, n_rows // chunk, body, 0)


def _ffn_kernel(x_ref, mod_ref, g2_ref, g3_ref, wg_ref, wu_ref, wd_ref, out_ref, hb_ref):
    f = pl.program_id(1)
    nf = pl.num_programs(1)
    tm = x_ref.shape[0]

    @pl.when(f == 0)
    def _():
        def prologue(rows):
            h = _rms(x_ref[rows, :]) * g2_ref[...]
            hb_ref[rows, :] = (h * (1.0 + mod_ref[4:5, :]) + mod_ref[3:4, :]).astype(BF16)
            out_ref[rows, :] = jnp.zeros((ROW_CHUNK, out_ref.shape[1]), F32)
        _for_row_chunks(tm, ROW_CHUNK, prologue)

    out_ref[...] += _swiglu_step(hb_ref[...], wg_ref, wu_ref, wd_ref)

    @pl.when(f == nf - 1)
    def _():
        def epilogue(rows):
            y = _rms(out_ref[rows, :]) * g3_ref[...]
            out_ref[rows, :] = x_ref[rows, :] + mod_ref[5:6, :] * y
        _for_row_chunks(tm, ROW_CHUNK, epilogue)


def _top2(logits, n_experts):
    lane = lax.broadcasted_iota(jnp.int32, logits.shape, 1)
    neg = jnp.float32(-jnp.inf)
    l1 = jnp.where(lane < n_experts, logits, neg)
    m1 = jnp.max(l1, axis=-1, keepdims=True)
    i1 = jnp.min(jnp.where(l1 == m1, lane, LANES), axis=-1, keepdims=True)
    l2 = jnp.where(lane == i1, neg, l1)
    m2 = jnp.max(l2, axis=-1, keepdims=True)
    i2 = jnp.min(jnp.where(l2 == m2, lane, LANES), axis=-1, keepdims=True)
    e2 = jnp.exp(m2 - m1)
    return lane, i1, i2, 1.0 / (1.0 + e2), e2 / (1.0 + e2)


def _route_kernel(x_ref, mod_ref, g2_ref, wr_ref, ri_ref, rp_ref, cnt_ref, carry_ref,
                  *, n_experts):
    m = pl.program_id(0)
    tm = x_ref.shape[0]

    @pl.when(m == 0)
    def _():
        carry_ref[...] = jnp.zeros_like(carry_ref)

    h = _modulate_ffn(x_ref, mod_ref, g2_ref)
    logits = jnp.dot(h, wr_ref[...], preferred_element_type=F32, precision=lax.Precision.HIGHEST)
    lane, i1, i2, p1, p2 = _top2(logits, n_experts)
    sel = jnp.logical_or(lane == i1, lane == i2)
    r = lax.broadcasted_iota(jnp.int32, (tm, tm), 0)
    c = lax.broadcasted_iota(jnp.int32, (tm, tm), 1)
    before = jnp.where(r > c, 1.0, 0.0).astype(BF16)
    rank = jnp.dot(before, jnp.where(sel, 1.0, 0.0).astype(BF16), preferred_element_type=F32)
    rank = rank + carry_ref[0:1, :]
    r1 = jnp.sum(jnp.where(lane == i1, rank, 0.0), axis=-1, keepdims=True).astype(jnp.int32)
    r2 = jnp.sum(jnp.where(lane == i2, rank, 0.0), axis=-1, keepdims=True).astype(jnp.int32)
    carry_ref[0:1, :] = carry_ref[0:1, :] + jnp.sum(jnp.where(sel, 1.0, 0.0), axis=0, keepdims=True)
    ri_ref[...] = jnp.where(lane == 0, i1, jnp.where(lane == 1, i2,
                            jnp.where(lane == 2, r1, jnp.where(lane == 3, r2, 0))))
    rp_ref[...] = jnp.where(lane == 0, p1, jnp.where(lane == 1, p2, 0.0))
    cnt_ref[...] = carry_ref[...]


def _row_copy(src_ref, src_row, dst_ref, dst_row, sem):
    return pltpu.make_async_copy(src_ref.at[pl.ds(src_row, 1), :], dst_ref.at[pl.ds(dst_row, 1), :], sem)


def _rows_wait(any_ref, n_rows, sem):
    pltpu.make_async_copy(any_ref.at[pl.ds(0, n_rows), :], any_ref.at[pl.ds(0, n_rows), :], sem).wait()


def _dispatch_kernel(pos_ref, pad_start_ref, pad_cnt_ref, x_ref, mod_ref, g2_ref, hs_ref,
                     pk_ref, zero_ref, sem, pad_sem, *, n_experts):
    m = pl.program_id(0)
    n = pl.num_programs(0)
    tm, half = x_ref.shape[0], x_ref.shape[1] // 2
    slot = m % 2

    @pl.when(m >= 2)
    def _():
        _rows_wait(hs_ref, 2 * tm, sem.at[slot])

    def pack(rows):
        h = _rms(x_ref[rows, :]) * g2_ref[...]
        h = (h * (1.0 + mod_ref[4:5, :]) + mod_ref[3:4, :]).astype(BF16).astype(F32)
        bits = lax.bitcast_convert_type(h, jnp.uint32)
        pk_ref[slot, rows, :] = jnp.bitwise_or(bits[:, half:], jnp.right_shift(bits[:, :half], 16))
    _for_row_chunks(tm, ROW_CHUNK, pack)

    def issue(r, carry):
        t = m * tm + r
        _row_copy(pk_ref.at[slot], r, hs_ref, pos_ref[2 * t], sem.at[slot]).start()
        _row_copy(pk_ref.at[slot], r, hs_ref, pos_ref[2 * t + 1], sem.at[slot]).start()
        return carry

    lax.fori_loop(0, tm, issue, 0, unroll=8)

    @pl.when(m == 0)
    def _():
        zero_ref[...] = jnp.zeros_like(zero_ref)
        for e in range(n_experts):
            start, cnt = pad_start_ref[e], pad_cnt_ref[e]

            def pad_issue(i, carry):
                _row_copy(zero_ref, 0, hs_ref, start + i, pad_sem).start()
                return carry

            def pad_wait(i, carry):
                _row_copy(zero_ref, 0, hs_ref, 0, pad_sem).wait()
                return carry

            lax.fori_loop(0, cnt, pad_issue, 0)
            lax.fori_loop(0, cnt, pad_wait, 0)

    @pl.when(m == n - 1)
    def _():
        _rows_wait(hs_ref, 2 * tm, sem.at[slot])

        @pl.when(n >= 2)
        def _():
            _rows_wait(hs_ref, 2 * tm, sem.at[1 - slot])


def _for_live_rows(nrows, tm, fn):
    for n in range(MOE_SUB, tm + 1, MOE_SUB):
        pl.when(nrows == n)(functools.partial(fn, n))


def _moe_up_kernel(te_ref, tix_ref, first_ref, nrows_ref, hs_ref, wg_ref, wu_ref, a_ref,
                   wgb_ref, wub_ref):
    m = pl.program_id(1)

    @pl.when(first_ref[m] == 1)
    def _():
        wgb_ref[...] = wg_ref[...].astype(BF16)
        wub_ref[...] = wu_ref[...].astype(BF16)

    half = hs_ref.shape[1]

    def up(n):
        w = hs_ref[0:n, :]
        lo = lax.bitcast_convert_type(jnp.left_shift(w, 16), F32).astype(BF16)
        hi = lax.bitcast_convert_type(jnp.bitwise_and(w, jnp.uint32(0xFFFF0000)), F32).astype(BF16)
        g = (jnp.dot(lo, wgb_ref[0:half, :], preferred_element_type=F32)
             + jnp.dot(hi, wgb_ref[half:, :], preferred_element_type=F32))
        u = (jnp.dot(lo, wub_ref[0:half, :], preferred_element_type=F32)
             + jnp.dot(hi, wub_ref[half:, :], preferred_element_type=F32))
        a_ref[0:n, :] = ((g / (1.0 + jnp.exp(-g))) * u).astype(BF16)
    _for_live_rows(nrows_ref[m], hs_ref.shape[0], up)


def _moe_down_kernel(te_ref, tix_ref, first_ref, nrows_ref, a_ref, wd_ref, ys_ref, wdb_ref):
    m = pl.program_id(1)

    @pl.when(first_ref[m] == 1)
    def _():
        wdb_ref[...] = wd_ref[...].astype(BF16)

    def down(n):
        ys_ref[0:n, :] = jnp.dot(a_ref[0:n, :], wdb_ref[...], preferred_element_type=F32)
    _for_live_rows(nrows_ref[m], a_ref.shape[0], down)


def _combine_kernel(pos_ref, ys_ref, rp_ref, x_ref, mod_ref, g3_ref, *rest, n_split):
    if n_split is None:
        out_ref, buf_ref, sem = rest
    else:
        out_a_ref, out_b_ref, buf_ref, sem = rest
    m = pl.program_id(0)
    n = pl.num_programs(0)
    tm = x_ref.shape[0]

    def issue_tile(tile, slot):
        def issue(r, carry):
            t = tile * tm + r
            _row_copy(ys_ref, pos_ref[2 * t], buf_ref.at[slot], r, sem.at[slot]).start()
            _row_copy(ys_ref, pos_ref[2 * t + 1], buf_ref.at[slot], tm + r, sem.at[slot]).start()
            return carry
        lax.fori_loop(0, tm, issue, 0, unroll=8)

    @pl.when(m == 0)
    def _():
        issue_tile(0, 0)

    @pl.when(m + 1 < n)
    def _():
        issue_tile(m + 1, (m + 1) % 2)

    slot = m % 2

    _rows_wait(ys_ref, 2 * tm, sem.at[slot])

    def result(rows, second):
        y = rp_ref[rows, 0:1] * buf_ref[slot, rows, :] + rp_ref[rows, 1:2] * buf_ref[slot, second, :]
        return x_ref[rows, :] + mod_ref[5:6, :] * (_rms(y) * g3_ref[...])

    def chunked(dst_ref):
        def body(c, carry):
            r0 = pl.multiple_of(c * ROW_CHUNK, ROW_CHUNK)
            dst_ref[pl.ds(r0, ROW_CHUNK), :] = result(pl.ds(r0, ROW_CHUNK), pl.ds(tm + r0, ROW_CHUNK))
            return carry
        lax.fori_loop(0, tm // ROW_CHUNK, body, 0)

    if n_split is None:
        chunked(out_ref)
    else:
        @pl.when(m < n_split)
        def _():
            chunked(out_a_ref)

        @pl.when(m >= n_split)
        def _():
            chunked(out_b_ref)


def _moe_layer(x, mods, g4, wr, w_e_gate, w_e_up, w_e_down, *, l, i, tm, tf, mrow, n_split=None):
    T, D = x.shape
    NE, EFF = w_e_gate.shape[1], w_e_gate.shape[3]
    n_tiles = T // tm
    tmu = 2 * tm
    n_mt = (TOP_K * T) // tmu + NE
    Tp = n_mt * tmu
    tn = 512
    assert D % tn == 0 and EFF % tf == 0 and (TOP_K * T) % tmu == 0
    sds = jax.ShapeDtypeStruct

    ri, rp, cnt = pl.pallas_call(
        functools.partial(_route_kernel, n_experts=NE),
        grid=(n_tiles,),
        in_specs=[
            pl.BlockSpec((tm, D), lambda m: (m, 0)),
            pl.BlockSpec((None, None, N_MOD, D), lambda m: (l, mrow(m), 0, 0)),
            pl.BlockSpec((None, 1, D), lambda m: (4 * l + 2, 0, 0)),
            pl.BlockSpec((None, D, LANES), lambda m: (i, 0, 0)),
        ],
        out_specs=[
            pl.BlockSpec((tm, LANES), lambda m: (m, 0)),
            pl.BlockSpec((tm, LANES), lambda m: (m, 0)),
            pl.BlockSpec((SUBLANES, LANES), lambda m: (0, 0)),
        ],
        out_shape=[sds((T, LANES), jnp.int32), sds((T, LANES), F32), sds((SUBLANES, LANES), F32)],
        scratch_shapes=[pltpu.VMEM((SUBLANES, LANES), F32)],
        compiler_params=_params(("arbitrary",)),
        name="moe_route",
    )(x, mods, g4, wr)

    cnt = cnt[0, :NE].astype(jnp.int32)
    pc = (cnt + tmu - 1) // tmu * tmu
    starts = jnp.cumsum(pc) - pc
    pos = jnp.stack([starts[ri[:, 0]] + ri[:, 2], starts[ri[:, 1]] + ri[:, 3]], axis=1).reshape(-1)
    zero_fill = (-cnt) % MOE_SUB

    def tile_list(t, n_max):
        per = (cnt + t - 1) // t
        upto = jnp.cumsum(per)
        ar = jnp.arange(n_max, dtype=jnp.int32)
        k = jnp.minimum(ar, upto[-1] - 1)
        te = jnp.minimum(jnp.sum(k[:, None] >= upto[None, :], axis=1), NE - 1).astype(jnp.int32)
        off = k - (upto[te] - per[te])
        tix = (starts[te] // t + off).astype(jnp.int32)
        first = jnp.logical_or(te != jnp.roll(te, 1), ar == 0).astype(jnp.int32)
        live = jnp.clip(cnt[te] - off * t, 0, t)
        nrows = jnp.where(ar < upto[-1], (live + MOE_SUB - 1) // MOE_SUB * MOE_SUB, 0).astype(jnp.int32)
        return te, tix, first, nrows

    te_u, tix_u, first_u, nrows_u = tile_list(tmu, n_mt)
    n_mt_d = (TOP_K * T) // tm + NE
    te_d, tix_d, first_d, nrows_d = tile_list(tm, n_mt_d)

    hs = pl.pallas_call(
        functools.partial(_dispatch_kernel, n_experts=NE),
        grid_spec=pltpu.PrefetchScalarGridSpec(
            num_scalar_prefetch=3, grid=(n_tiles,),
            in_specs=[
                pl.BlockSpec((tm, D), lambda m, pos, ps, pn: (m, 0)),
                pl.BlockSpec((None, None, N_MOD, D), lambda m, pos, ps, pn: (l, mrow(m), 0, 0)),
                pl.BlockSpec((None, 1, D), lambda m, pos, ps, pn: (4 * l + 2, 0, 0)),
            ],
            out_specs=pl.BlockSpec(memory_space=pl.ANY),
            scratch_shapes=[pltpu.VMEM((2, tm, D // 2), jnp.uint32),
                            pltpu.VMEM((SUBLANES, D // 2), jnp.uint32),
                            pltpu.SemaphoreType.DMA((2,)), pltpu.SemaphoreType.DMA(())]),
        out_shape=sds((Tp, D // 2), jnp.uint32),
        compiler_params=_params(("arbitrary",)),
        name="moe_dispatch",
    )(pos, starts + cnt, zero_fill, x, mods, g4)

    a = pl.pallas_call(
        _moe_up_kernel,
        grid_spec=pltpu.PrefetchScalarGridSpec(
            num_scalar_prefetch=4, grid=(EFF // tf, n_mt),
            in_specs=[
                pl.BlockSpec((tmu, D // 2), lambda f, m, te, tix, fi, nu: (tix[m], 0)),
                pl.BlockSpec((None, None, D, tf), lambda f, m, te, tix, fi, nu: (i, te[m], 0, f)),
                pl.BlockSpec((None, None, D, tf), lambda f, m, te, tix, fi, nu: (i, te[m], 0, f)),
            ],
            out_specs=pl.BlockSpec((tmu, tf), lambda f, m, te, tix, fi, nu: (tix[m], f)),
            scratch_shapes=[pltpu.VMEM((D, tf), BF16), pltpu.VMEM((D, tf), BF16)]),
        out_shape=sds((Tp, EFF), BF16),
        compiler_params=_params(("arbitrary", "arbitrary")),
        name="moe_up",
    )(te_u, tix_u, first_u, nrows_u, hs, w_e_gate, w_e_up)

    ys = pl.pallas_call(
        _moe_down_kernel,
        grid_spec=pltpu.PrefetchScalarGridSpec(
            num_scalar_prefetch=4, grid=(D // tn, n_mt_d),
            in_specs=[
                pl.BlockSpec((tm, EFF), lambda n, m, te, tix, fi, nu: (tix[m], 0)),
                pl.BlockSpec((None, None, EFF, tn), lambda n, m, te, tix, fi, nu: (i, te[m], 0, n)),
            ],
            out_specs=pl.BlockSpec((tm, tn), lambda n, m, te, tix, fi, nu: (tix[m], n)),
            scratch_shapes=[pltpu.VMEM((EFF, tn), BF16)]),
        out_shape=sds((Tp, D), F32),
        compiler_params=_params(("arbitrary", "arbitrary")),
        name="moe_down",
    )(te_d, tix_d, first_d, nrows_d, a, w_e_down)

    if n_split is None:
        out_specs = pl.BlockSpec((tm, D), lambda m, pos: (m, 0))
        out_shape = sds((T, D), F32)
    else:
        out_specs = [pl.BlockSpec((tm, D), lambda m, pos: (jnp.minimum(m, n_split - 1), 0)),
                     pl.BlockSpec((tm, D), lambda m, pos: (jnp.maximum(m - n_split, 0), 0))]
        out_shape = [sds((n_split * tm, D), F32), sds((T - n_split * tm, D), F32)]
    return pl.pallas_call(
        functools.partial(_combine_kernel, n_split=n_split),
        grid_spec=pltpu.PrefetchScalarGridSpec(
            num_scalar_prefetch=1, grid=(n_tiles,),
            in_specs=[
                pl.BlockSpec(memory_space=pl.ANY),
                pl.BlockSpec((tm, LANES), lambda m, pos: (m, 0)),
                pl.BlockSpec((tm, D), lambda m, pos: (m, 0)),
                pl.BlockSpec((None, None, N_MOD, D), lambda m, pos: (l, mrow(m), 0, 0)),
                pl.BlockSpec((None, 1, D), lambda m, pos: (4 * l + 3, 0, 0)),
            ],
            out_specs=out_specs,
            scratch_shapes=[pltpu.VMEM((2, 2 * tm, D), F32), pltpu.SemaphoreType.DMA((2,))]),
        out_shape=out_shape,
        compiler_params=_params(("arbitrary",)),
        name="moe_combine",
    )(pos, ys, rp, x, mods, g4)


def kernel(x_prompt, x_sample, cache_ckv, cache_krope, c, c_ctx, w_in, g_q, g_kv, w_uq, w_uk, w_uv,
           w_conv, w_o, w_ada, b_ada, g_sand, w_ff_gate, w_ff_up, w_ff_down, w_router,
           w_e_gate, w_e_up, w_e_down):
    B, N, D = x_prompt.shape
    DB, DN, _ = x_sample.shape
    L = w_in.shape[0]
    QL, KV = g_q.shape[1], g_kv.shape[1]
    H, NOPE = w_uk.shape[2], w_uk.shape[3]
    ROPE = w_uq.shape[3] - NOPE
    VD = w_uv.shape[3]
    CONV = w_conv.shape[2]
    AW = H * VD
    PAST = cache_ckv.shape[2]
    FF = w_ff_gate.shape[2]
    NE, EFF = w_e_gate.shape[1], w_e_gate.shape[3]
    T_P, T_S = B * N, DB * DN
    T = T_P + T_S
    DK = KV + LANES
    NA = 1024
    tm = TM
    assert ROPE == 64 and 2 * ROPE == LANES and NOPE % LANES == 0 and KV % LANES == 0 and QL % LANES == 0
    assert QL + KV + LANES <= NA and CONV % LANES == 0 and AW + CONV == w_o.shape[1]
    assert T_P % tm == 0 and DN % tm == 0 and tm % N == 0 and T_P % DN == 0
    assert N & (N - 1) == 0 and DN & (DN - 1) == 0 and DN % GRID_W == 0
    assert NE <= LANES and DB + 1 <= SUBLANES
    n_p_tiles = T_P // tm
    tiles_per_s = DN // tm
    n_tiles = T // tm
    scale = float(NOPE + ROPE) ** -0.5
    tf = 512
    assert FF % tf == 0 and EFF % tf == 0

    def mrow(m):
        return jnp.where(m < n_p_tiles, 0, 1 + (m - n_p_tiles) // tiles_per_s)

    def trow(m):
        return jnp.where(m < n_p_tiles, 0, 1 + (m - n_p_tiles) % tiles_per_s)

    t = jnp.arange(DN)
    inv = 1.0 / (ROPE_THETA ** (jnp.arange(0, ROPE // 2, 2, dtype=F32) / (ROPE // 2)))
    ar = (t // GRID_W).astype(F32)[:, None] * inv
    ac = (t % GRID_W).astype(F32)[:, None] * inv
    ang = jnp.concatenate([ar, ar, ac, ac] * 2, axis=-1)
    first = (jnp.arange(LANES) % (ROPE // 2)) < (ROPE // 4)
    cos_t = jnp.concatenate([jnp.ones((tm, LANES), F32), jnp.cos(ang)], axis=0)
    sin = jnp.sin(ang)
    sa_t = jnp.concatenate([jnp.zeros((tm, LANES), F32), jnp.where(first, -sin, 0.0)], axis=0)
    sb_t = jnp.concatenate([jnp.zeros((tm, LANES), F32), jnp.where(first, 0.0, sin)], axis=0)

    s1, s2 = QL + KV + ROPE, QL + KV + ROPE + 3 * CONV
    w_in_b = w_in.astype(BF16)
    w_a = jnp.pad(w_in_b[:, :, :s1], ((0, 0), (0, 0), (0, NA - s1)))
    w_b = w_in_b[:, :, s1:s2]
    wq = jnp.pad(w_uq, ((0, 0), (0, 0), (0, 0), (0, LANES - ROPE))).astype(BF16)
    wq = wq.reshape(L, QL, H * (NOPE + LANES))
    wk = w_uk.transpose(0, 2, 3, 1).astype(BF16)
    wuv = w_uv.transpose(0, 2, 1, 3).astype(BF16)
    wo = w_o.astype(BF16)
    wfg, wfu, wfd = w_ff_gate.astype(BF16), w_ff_up.astype(BF16), w_ff_down.astype(BF16)
    wr = jnp.pad(w_router, ((0, 0), (0, 0), (0, LANES - NE)))
    kcache = jnp.concatenate(
        [cache_ckv, cache_krope, jnp.zeros(cache_krope.shape[:-1] + (LANES - ROPE,), F32)],
        axis=-1).astype(BF16)

    cond8 = jnp.concatenate([c_ctx[None, :], c, jnp.zeros((SUBLANES - 1 - DB, D), F32)], axis=0)
    mods = _ada(cond8, w_ada, b_ada)[:, :1 + DB].reshape(L, 1 + DB, N_MOD, D)

    x = jnp.concatenate([x_prompt.reshape(T_P, D), x_sample.reshape(T_S, D)], axis=0)

    mod_spec2 = lambda l: pl.BlockSpec((None, None, N_MOD, D), lambda m, n: (l, mrow(m), 0, 0))
    row_spec2 = lambda w: pl.BlockSpec((tm, w), lambda m, n: (m, 0))
    tab_spec2 = pl.BlockSpec((tm, LANES), lambda m, n: (trow(m), 0))
    tab_spec1 = pl.BlockSpec((tm, LANES), lambda m: (trow(m), 0))

    ckv_list, kr_list = [], []
    for l in range(L):
        cqn, ckvn, kr, kcat, gates = pl.pallas_call(
            functools.partial(_inproj_kernel, QL=QL, KV=KV, ROPE=ROPE),
            grid=(n_tiles, 1 + 3),
            in_specs=[
                row_spec2(D),
                mod_spec2(l),
                pl.BlockSpec((None, 1, D), lambda m, n: (4 * l, 0, 0)),
                pl.BlockSpec((None, 1, QL), lambda m, n: (l, 0, 0)),
                pl.BlockSpec((None, 1, KV), lambda m, n: (l, 0, 0)),
                tab_spec2, tab_spec2, tab_spec2,
                pl.BlockSpec((None, D, NA), lambda m, n: (l, 0, 0)),
                pl.BlockSpec((None, D, CONV), lambda m, n: (l, 0, jnp.maximum(n - 1, 0))),
            ],
            out_specs=[
                row_spec2(QL), row_spec2(KV), row_spec2(ROPE), row_spec2(DK),
                pl.BlockSpec((tm, CONV), lambda m, n: (m, jnp.maximum(n - 1, 0))),
            ],
            out_shape=[
                jax.ShapeDtypeStruct((T, QL), BF16),
                jax.ShapeDtypeStruct((T, KV), F32),
                jax.ShapeDtypeStruct((T, ROPE), F32),
                jax.ShapeDtypeStruct((T, DK), BF16),
                jax.ShapeDtypeStruct((T, 3 * CONV), BF16),
            ],
            scratch_shapes=[pltpu.VMEM((tm, D), BF16)],
            compiler_params=_params(("arbitrary", "arbitrary")),
            name="in_proj",
        )(x, mods, g_sand.reshape(4 * L, 1, D), g_q.reshape(L, 1, QL), g_kv.reshape(L, 1, KV),
          cos_t, sa_t, sb_t, w_a, w_b)
        ckv_list.append(ckvn[:T_P].reshape(B, N, KV))
        kr_list.append(kr[:T_P].reshape(B, N, ROPE))

        qcat = pl.pallas_call(
            functools.partial(_q_kernel, H=H, NOPE=NOPE, KV=KV, scale=scale),
            grid=(n_tiles,),
            in_specs=[
                pl.BlockSpec((tm, QL), lambda m: (m, 0)),
                pl.BlockSpec((None, QL, H * (NOPE + LANES)), lambda m: (l, 0, 0)),
                pl.BlockSpec((None, H, NOPE, KV), lambda m: (l, 0, 0, 0)),
                tab_spec1, tab_spec1, tab_spec1,
            ],
            out_specs=pl.BlockSpec((tm, H * DK), lambda m: (m, 0)),
            out_shape=jax.ShapeDtypeStruct((T, H * DK), BF16),
            compiler_params=_params(("arbitrary",)),
            name="q_proj",
        )(cqn, wq, wk, cos_t, sa_t, sb_t)

        o_p = pl.pallas_call(
            functools.partial(_attn_kernel, H=H, KV=KV, has_cache=False),
            grid=(B,),
            in_specs=[
                pl.BlockSpec((N, H * DK), lambda b: (b, 0)),
                pl.BlockSpec((N, DK), lambda b: (b, 0)),
            ],
            out_specs=pl.BlockSpec((N, H * KV), lambda b: (b, 0)),
            out_shape=jax.ShapeDtypeStruct((T_P, H * KV), BF16),
            compiler_params=_params(("arbitrary",)),
            name="attn_prompt",
        )(qcat, kcat)
        tq = 256
        qs_per = DN // tq
        o_s = pl.pallas_call(
            functools.partial(_attn_kernel, H=H, KV=KV, has_cache=True),
            grid=(DB, qs_per),
            in_specs=[
                pl.BlockSpec((tq, H * DK), lambda b, i: (T_P // tq + b * qs_per + i, 0)),
                pl.BlockSpec((DN, DK), lambda b, i: (T_P // DN + b, 0)),
                pl.BlockSpec((None, None, PAST, DK), lambda b, i: (b, l, 0, 0)),
            ],
            out_specs=pl.BlockSpec((tq, H * KV), lambda b, i: (b * qs_per + i, 0)),
            out_shape=jax.ShapeDtypeStruct((T_S, H * KV), BF16),
            compiler_params=_params(("arbitrary", "arbitrary")),
            name="attn_sample",
        )(qcat, kcat, kcache)

        r8 = tm // HALO_ROWS
        prev8 = lambda m: jnp.maximum(m * r8 - 1, 0)
        next8 = lambda m: jnp.minimum((m + 1) * r8, T // HALO_ROWS - 1)
        x = pl.pallas_call(
            functools.partial(_mix_kernel, H=H, KV=KV, VD=VD, n_p_tiles=n_p_tiles,
                              seq_p=N, seq_s=DN, t_p=T_P),
            grid=(n_tiles,),
            in_specs=[
                pl.BlockSpec((tm, H * KV), lambda m: (jnp.minimum(m, n_p_tiles - 1), 0)),
                pl.BlockSpec((tm, H * KV), lambda m: (jnp.maximum(m - n_p_tiles, 0), 0)),
                pl.BlockSpec((tm, CONV), lambda m: (m, 0)),
                pl.BlockSpec((tm, CONV), lambda m: (m, 1)),
                pl.BlockSpec((tm, CONV), lambda m: (m, 2)),
                pl.BlockSpec((HALO_ROWS, CONV), lambda m: (prev8(m), 1)),
                pl.BlockSpec((HALO_ROWS, CONV), lambda m: (prev8(m), 2)),
                pl.BlockSpec((HALO_ROWS, CONV), lambda m: (next8(m), 1)),
                pl.BlockSpec((HALO_ROWS, CONV), lambda m: (next8(m), 2)),
                pl.BlockSpec((None, H, KV, VD), lambda m: (l, 0, 0, 0)),
                pl.BlockSpec((None, 3, CONV), lambda m: (l, 0, 0)),
                pl.BlockSpec((None, AW + CONV, D), lambda m: (l, 0, 0)),
                pl.BlockSpec((None, 1, D), lambda m: (4 * l + 1, 0, 0)),
                pl.BlockSpec((None, None, N_MOD, D), lambda m: (l, mrow(m), 0, 0)),
                pl.BlockSpec((tm, D), lambda m: (m, 0)),
            ],
            out_specs=pl.BlockSpec((tm, D), lambda m: (m, 0)),
            out_shape=jax.ShapeDtypeStruct((T, D), F32),
            scratch_shapes=[pltpu.VMEM((tm, AW + CONV), BF16),
                            pltpu.VMEM((tm + 2 * SUBLANES, CONV), F32)],
            compiler_params=_params(("arbitrary",)),
            name="mixer_out",
        )(o_p, o_s, gates, gates, gates, gates, gates, gates, gates, wuv, w_conv, wo,
          g_sand.reshape(4 * L, 1, D), mods, x)

        i = l // 2
        if l % 2 == 0:
            tmf = TM_FFN if (T_P % TM_FFN == 0 and DN % TM_FFN == 0) else tm
            mrow_f = lambda m: jnp.where(m < T_P // tmf, 0, 1 + (m - T_P // tmf) // (DN // tmf))
            x = pl.pallas_call(
                _ffn_kernel,
                grid=(T // tmf, FF // tf),
                in_specs=[
                    pl.BlockSpec((tmf, D), lambda m, f: (m, 0), pipeline_mode=pl.Buffered(1)),
                    pl.BlockSpec((None, None, N_MOD, D), lambda m, f: (l, mrow_f(m), 0, 0)),
                    pl.BlockSpec((None, 1, D), lambda m, f: (4 * l + 2, 0, 0)),
                    pl.BlockSpec((None, 1, D), lambda m, f: (4 * l + 3, 0, 0)),
                    pl.BlockSpec((None, D, tf), lambda m, f: (i, 0, f)),
                    pl.BlockSpec((None, D, tf), lambda m, f: (i, 0, f)),
                    pl.BlockSpec((None, tf, D), lambda m, f: (i, f, 0)),
                ],
                out_specs=pl.BlockSpec((tmf, D), lambda m, f: (m, 0)),
                out_shape=jax.ShapeDtypeStruct((T, D), F32),
                scratch_shapes=[pltpu.VMEM((tmf, D), BF16)],
                compiler_params=_params(("arbitrary", "arbitrary")),
                name="ffn_dense",
            )(x, mods, g_sand.reshape(4 * L, 1, D), g_sand.reshape(4 * L, 1, D), wfg, wfu, wfd)
        else:
            x = _moe_layer(x, mods, g_sand.reshape(4 * L, 1, D), wr, w_e_gate, w_e_up, w_e_down,
                           l=l, i=i, tm=tm, tf=tf, mrow=mrow,
                           n_split=n_p_tiles if l == L - 1 else None)

    x_p, x_s = x if isinstance(x, (list, tuple)) else (x[:T_P], x[T_P:])
    y_prompt = x_p.reshape(B, N, D)
    y_sample = x_s.reshape(DB, DN, D)
    new_ckv = jnp.stack(ckv_list, axis=1)
    new_krope = jnp.stack(kr_list, axis=1)
    return (y_prompt, y_sample, new_ckv, new_krope)
```

```python
import functools

import jax
import jax.numpy as jnp
from jax import lax
from jax.experimental import pallas as pl
from jax.experimental.pallas import tpu as pltpu

GRID_W = 64
ROPE_THETA = 10000.0
EPS = 1e-6
N_MOD = 6
TOP_K = 2

LANES = 128
SUBLANES = 8
TM = 512
TM_FFN = 1024
ROW_CHUNK = 256
HALO_ROWS = 16
MOE_SUB = 256
VMEM_LIMIT = 56 * 1024 * 1024

BF16 = jnp.bfloat16
F32 = jnp.float32


def _params(sem, vmem=VMEM_LIMIT):
    return pltpu.CompilerParams(dimension_semantics=sem, vmem_limit_bytes=vmem)


def _rms(x):
    return x * lax.rsqrt(jnp.mean(x * x, axis=-1, keepdims=True) + EPS)


def _rope(x, cos, sa, sb):
    return x * cos + pltpu.roll(x, LANES - 16, 1) * sa + pltpu.roll(x, 16, 1) * sb


def _ada_kernel(c_ref, w_ref, b_ref, o_ref):
    c = c_ref[...]
    s = c / (1.0 + jnp.exp(-c))
    o_ref[...] = jnp.dot(s.astype(BF16), w_ref[...].astype(BF16),
                         preferred_element_type=F32) + b_ref[...]


def _ada(cond8, w_ada, b_ada):
    L, D, N6 = w_ada.shape
    tn = 1024 if N6 % 1024 == 0 else N6
    return pl.pallas_call(
        _ada_kernel,
        grid=(L, N6 // tn),
        in_specs=[
            pl.BlockSpec((SUBLANES, D), lambda l, n: (0, 0)),
            pl.BlockSpec((None, D, tn), lambda l, n: (l, 0, n)),
            pl.BlockSpec((None, 1, tn), lambda l, n: (l, 0, n)),
        ],
        out_specs=pl.BlockSpec((None, SUBLANES, tn), lambda l, n: (l, 0, n)),
        out_shape=jax.ShapeDtypeStruct((L, SUBLANES, N6), F32),
        compiler_params=_params(("arbitrary", "arbitrary")),
        name="ada_mod",
    )(cond8, w_ada, b_ada.reshape(L, 1, N6))


def _tile_rows(refs, is_first, rows):
    if len(refs) == 1:
        return refs[0][rows, :]
    return jnp.where(is_first, refs[0][rows, :], refs[1][rows, :])


def _inproj_kernel(*refs, QL, KV, ROPE, n_x, n_p_tiles):
    x_refs, refs = refs[:n_x], refs[n_x:]
    (mod_ref, g_ref, gq_ref, gkv_ref, cos_ref, sa_ref, sb_ref, wa_ref, wb_ref,
     cqn_ref, ckv_ref, kr_ref, kcat_ref, gates_ref, hb_ref) = refs
    n = pl.program_id(1)
    is_p = pl.program_id(0) < n_p_tiles
    tm = hb_ref.shape[0]

    @pl.when(n == 0)
    def _():
        for r0 in range(0, tm, ROW_CHUNK):
            rows = slice(r0, r0 + ROW_CHUNK)
            h = _rms(_tile_rows(x_refs, is_p, rows)) * g_ref[...]
            h = h * (1.0 + mod_ref[1:2, :]) + mod_ref[0:1, :]
            hb_ref[rows, :] = h.astype(BF16)
            acc = jnp.dot(hb_ref[rows, :], wa_ref[...], preferred_element_type=F32)
            cqn_ref[rows, :] = (_rms(acc[:, :QL]) * gq_ref[...]).astype(BF16)
            ckv = _rms(acc[:, QL:QL + KV]) * gkv_ref[...]
            ckv_ref[rows, :] = ckv
            kr = _rope(acc[:, QL + KV:QL + KV + LANES], cos_ref[rows, :], sa_ref[rows, :], sb_ref[rows, :])
            kr_ref[rows, :] = kr[:, :ROPE]
            kcat_ref[rows, :KV] = ckv.astype(BF16)
            kcat_ref[rows, KV:] = kr.astype(BF16)

    @pl.when(n > 0)
    def _():
        gates_ref[...] = jnp.dot(hb_ref[...], wb_ref[...], preferred_element_type=F32).astype(BF16)


def _q_kernel(cqn_ref, wq_ref, wk_ref, cos_ref, sa_ref, sb_ref, o_ref, *, H, NOPE, KV, scale):
    QW = NOPE + LANES
    DK = KV + LANES
    cqn = cqn_ref[...]
    for h in range(H):
        q = jnp.dot(cqn, wq_ref[:, h * QW:(h + 1) * QW], preferred_element_type=F32)
        q_lat = jnp.dot(q[:, :NOPE].astype(BF16), wk_ref[h], preferred_element_type=F32)
        q_rope = _rope(q[:, NOPE:], cos_ref[...], sa_ref[...], sb_ref[...])
        o_ref[:, h * DK:h * DK + KV] = (q_lat * scale).astype(BF16)
        o_ref[:, h * DK + KV:(h + 1) * DK] = (q_rope * scale).astype(BF16)


def _attn_kernel(*refs, H, KV, has_cache):
    if has_cache:
        q_ref, kn_ref, kc_ref, o_ref = refs
    else:
        q_ref, kn_ref, o_ref = refs
    DK = KV + LANES
    dn = (((1,), (1,)), ((), ()))
    kn = kn_ref[...]
    vn = kn[:, :KV]
    if has_cache:
        kc = kc_ref[...]
        vc = kc[:, :KV]
    for h in range(H):
        q = q_ref[:, h * DK:(h + 1) * DK]
        s = lax.dot_general(q, kn, dn, preferred_element_type=F32)
        m = jnp.max(s, axis=-1, keepdims=True)
        if has_cache:
            sc = lax.dot_general(q, kc, dn, preferred_element_type=F32)
            m = jnp.maximum(m, jnp.max(sc, axis=-1, keepdims=True))
        p = jnp.exp(s - m)
        l = jnp.sum(p, axis=-1, keepdims=True)
        o = jnp.dot(p.astype(BF16), vn, preferred_element_type=F32)
        if has_cache:
            pc = jnp.exp(sc - m)
            l = l + jnp.sum(pc, axis=-1, keepdims=True)
            o = o + jnp.dot(pc.astype(BF16), vc, preferred_element_type=F32)
        o_ref[:, h * KV:(h + 1) * KV] = (o / l).astype(BF16)


def _mix_kernel(*refs, H, KV, VD, n_p_tiles, seq_p, seq_s, t_p, n_x):
    x_refs, refs = refs[:n_x], refs[n_x:]
    (op_ref, os_ref, gb_ref, gc_ref, u_ref, gcp_ref, up_ref, gcn_ref, un_ref,
     wuv_ref, wc_ref, wo_ref, g_ref, mod_ref, out_ref, cat_ref, vs_ref) = refs
    m = pl.program_id(0)
    tm = out_ref.shape[0]
    AW = H * VD
    is_p = m < n_p_tiles
    seq = jnp.where(is_p, seq_p, seq_s)
    row0 = m * tm - jnp.where(is_p, 0, t_p)

    hr = HALO_ROWS
    vs_ref[SUBLANES:SUBLANES + tm, :] = gc_ref[...].astype(F32) * u_ref[...].astype(F32)
    vs_ref[SUBLANES - 1:SUBLANES, :] = gcp_ref[hr - 1:hr, :].astype(F32) * up_ref[hr - 1:hr, :].astype(F32)
    vs_ref[SUBLANES + tm:SUBLANES + tm + 1, :] = gcn_ref[0:1, :].astype(F32) * un_ref[0:1, :].astype(F32)

    for r0 in range(0, tm, ROW_CHUNK):
        rows = slice(r0, r0 + ROW_CHUNK)
        for h in range(H):
            cols = slice(h * KV, (h + 1) * KV)
            o = jnp.where(is_p, op_ref[rows, cols], os_ref[rows, cols])
            a = jnp.dot(o, wuv_ref[h], preferred_element_type=F32)
            cat_ref[rows, h * VD:(h + 1) * VD] = a.astype(BF16)
        pos = jnp.bitwise_and(lax.broadcasted_iota(jnp.int32, (ROW_CHUNK, 1), 0) + (row0 + r0), seq - 1)
        v = vs_ref[SUBLANES + r0:SUBLANES + r0 + ROW_CHUNK, :]
        vp = jnp.where(pos == 0, 0.0, vs_ref[SUBLANES - 1 + r0:SUBLANES - 1 + r0 + ROW_CHUNK, :])
        vn = jnp.where(pos == seq - 1, 0.0, vs_ref[SUBLANES + 1 + r0:SUBLANES + 1 + r0 + ROW_CHUNK, :])
        conv = vp * wc_ref[0:1, :] + v * wc_ref[1:2, :] + vn * wc_ref[2:3, :]
        cat_ref[rows, AW:] = (gb_ref[rows, :].astype(F32) * conv).astype(BF16)
        y = jnp.dot(cat_ref[rows, :], wo_ref[...], preferred_element_type=F32)
        out_ref[rows, :] = _tile_rows(x_refs, is_p, rows) + mod_ref[2:3, :] * (_rms(y) * g_ref[...])


def _modulate_ffn(x_ref, mod_ref, g_ref):
    h = _rms(x_ref[...]) * g_ref[...]
    return h * (1.0 + mod_ref[4:5, :]) + mod_ref[3:4, :]


def _swiglu_step(hb, wg_ref, wu_ref, wd_ref):
    g = jnp.dot(hb, wg_ref[...], preferred_element_type=F32)
    u = jnp.dot(hb, wu_ref[...], preferred_element_type=F32)
    a = (g / (1.0 + jnp.exp(-g))) * u
    return jnp.dot(a.astype(BF16), wd_ref[...], preferred_element_type=F32)


def _for_row_chunks(n_rows, chunk, fn):
    def body(c, carry):
        fn(pl.ds(pl.multiple_of(c * chunk, chunk), chunk))
        return carry
    lax.fori_loop(0, n_rows // chunk, body, 0)


def _ffn_kernel(x_ref, mod_ref, g2_ref, g3_ref, wg_ref, wu_ref, wd_ref, out_ref, hb_ref):
    f = pl.program_id(1)
    nf = pl.num_programs(1)
    tm = x_ref.shape[0]

    @pl.when(f == 0)
    def _():
        def prologue(rows):
            h = _rms(x_ref[rows, :]) * g2_ref[...]
            hb_ref[rows, :] = (h * (1.0 + mod_ref[4:5, :]) + mod_ref[3:4, :]).astype(BF16)
            out_ref[rows, :] = jnp.zeros((ROW_CHUNK, out_ref.shape[1]), F32)
        _for_row_chunks(tm, ROW_CHUNK, prologue)

    out_ref[...] += _swiglu_step(hb_ref[...], wg_ref, wu_ref, wd_ref)

    @pl.when(f == nf - 1)
    def _():
        def epilogue(rows):
            y = _rms(out_ref[rows, :]) * g3_ref[...]
            out_ref[rows, :] = x_ref[rows, :] + mod_ref[5:6, :] * y
        _for_row_chunks(tm, ROW_CHUNK, epilogue)


def _top2(logits, n_experts):
    lane = lax.broadcasted_iota(jnp.int32, logits.shape, 1)
    neg = jnp.float32(-jnp.inf)
    l1 = jnp.where(lane < n_experts, logits, neg)
    m1 = jnp.max(l1, axis=-1, keepdims=True)
    i1 = jnp.min(jnp.where(l1 == m1, lane, LANES), axis=-1, keepdims=True)
    l2 = jnp.where(lane == i1, neg, l1)
    m2 = jnp.max(l2, axis=-1, keepdims=True)
    i2 = jnp.min(jnp.where(l2 == m2, lane, LANES), axis=-1, keepdims=True)
    e2 = jnp.exp(m2 - m1)
    return lane, i1, i2, 1.0 / (1.0 + e2), e2 / (1.0 + e2)


def _route_kernel(x_ref, mod_ref, g2_ref, wr_ref, ri_ref, rp_ref, cnt_ref, carry_ref,
                  *, n_experts):
    m = pl.program_id(0)
    tm = x_ref.shape[0]

    @pl.when(m == 0)
    def _():
        carry_ref[...] = jnp.zeros_like(carry_ref)

    h = _modulate_ffn(x_ref, mod_ref, g2_ref)
    h_hi = h.astype(BF16)
    h_lo = (h - h_hi.astype(F32)).astype(BF16)
    w = wr_ref[...]
    w_hi = w.astype(BF16)
    w_lo = (w - w_hi.astype(F32)).astype(BF16)
    logits = (jnp.dot(h_hi, w_hi, preferred_element_type=F32)
              + jnp.dot(h_hi, w_lo, preferred_element_type=F32)
              + jnp.dot(h_lo, w_hi, preferred_element_type=F32))
    lane, i1, i2, p1, p2 = _top2(logits, n_experts)
    sel = jnp.logical_or(lane == i1, lane == i2)
    r = lax.broadcasted_iota(jnp.int32, (tm, tm), 0)
    c = lax.broadcasted_iota(jnp.int32, (tm, tm), 1)
    before = jnp.where(r > c, 1.0, 0.0).astype(BF16)
    rank = jnp.dot(before, jnp.where(sel, 1.0, 0.0).astype(BF16), preferred_element_type=F32)
    rank = rank + carry_ref[0:1, :]
    r1 = jnp.sum(jnp.where(lane == i1, rank, 0.0), axis=-1, keepdims=True).astype(jnp.int32)
    r2 = jnp.sum(jnp.where(lane == i2, rank, 0.0), axis=-1, keepdims=True).astype(jnp.int32)
    carry_ref[0:1, :] = carry_ref[0:1, :] + jnp.sum(jnp.where(sel, 1.0, 0.0), axis=0, keepdims=True)
    ri_ref[...] = jnp.where(lane == 0, i1, jnp.where(lane == 1, i2,
                            jnp.where(lane == 2, r1, jnp.where(lane == 3, r2, 0))))
    rp_ref[...] = jnp.where(lane == 0, p1, jnp.where(lane == 1, p2, 0.0))
    cnt_ref[...] = carry_ref[...]


def _row_copy(src_ref, src_row, dst_ref, dst_row, sem):
    return pltpu.make_async_copy(src_ref.at[pl.ds(src_row, 1), :], dst_ref.at[pl.ds(dst_row, 1), :], sem)


def _rows_wait(any_ref, n_rows, sem):
    pltpu.make_async_copy(any_ref.at[pl.ds(0, n_rows), :], any_ref.at[pl.ds(0, n_rows), :], sem).wait()


def _dispatch_kernel(pos_ref, pad_start_ref, pad_cnt_ref, x_ref, mod_ref, g2_ref, hs_ref,
                     pk_ref, zero_ref, sem, pad_sem, *, n_experts):
    m = pl.program_id(0)
    n = pl.num_programs(0)
    tm, half = x_ref.shape[0], x_ref.shape[1] // 2
    slot = m % 2

    @pl.when(m >= 2)
    def _():
        _rows_wait(hs_ref, 2 * tm, sem.at[slot])

    def pack(rows):
        h = _rms(x_ref[rows, :]) * g2_ref[...]
        h = (h * (1.0 + mod_ref[4:5, :]) + mod_ref[3:4, :]).astype(BF16).astype(F32)
        bits = lax.bitcast_convert_type(h, jnp.uint32)
        pk_ref[slot, rows, :] = jnp.bitwise_or(bits[:, half:], jnp.right_shift(bits[:, :half], 16))
    _for_row_chunks(tm, ROW_CHUNK, pack)

    def issue(r, carry):
        t = m * tm + r
        _row_copy(pk_ref.at[slot], r, hs_ref, pos_ref[2 * t], sem.at[slot]).start()
        _row_copy(pk_ref.at[slot], r, hs_ref, pos_ref[2 * t + 1], sem.at[slot]).start()
        return carry

    lax.fori_loop(0, tm, issue, 0, unroll=8)

    @pl.when(m == 0)
    def _():
        zero_ref[...] = jnp.zeros_like(zero_ref)
        for e in range(n_experts):
            start, cnt = pad_start_ref[e], pad_cnt_ref[e]

            def pad_issue(i, carry):
                _row_copy(zero_ref, 0, hs_ref, start + i, pad_sem).start()
                return carry

            def pad_wait(i, carry):
                _row_copy(zero_ref, 0, hs_ref, 0, pad_sem).wait()
                return carry

            lax.fori_loop(0, cnt, pad_issue, 0)
            lax.fori_loop(0, cnt, pad_wait, 0)

    @pl.when(m == n - 1)
    def _():
        _rows_wait(hs_ref, 2 * tm, sem.at[slot])

        @pl.when(n >= 2)
        def _():
            _rows_wait(hs_ref, 2 * tm, sem.at[1 - slot])


def _for_live_rows(nrows, tm, fn):
    for n in range(MOE_SUB, tm + 1, MOE_SUB):
        pl.when(nrows == n)(functools.partial(fn, n))


def _moe_up_kernel(te_ref, tix_ref, first_ref, nrows_ref, hs_ref, wg_ref, wu_ref, a_ref,
                   wgb_ref, wub_ref):
    m = pl.program_id(1)

    @pl.when(first_ref[m] == 1)
    def _():
        wgb_ref[...] = wg_ref[...].astype(BF16)
        wub_ref[...] = wu_ref[...].astype(BF16)

    half = hs_ref.shape[1]

    def up(n):
        w = hs_ref[0:n, :]
        lo = lax.bitcast_convert_type(jnp.left_shift(w, 16), F32).astype(BF16)
        hi = lax.bitcast_convert_type(jnp.bitwise_and(w, jnp.uint32(0xFFFF0000)), F32).astype(BF16)
        g = (jnp.dot(lo, wgb_ref[0:half, :], preferred_element_type=F32)
             + jnp.dot(hi, wgb_ref[half:, :], preferred_element_type=F32))
        u = (jnp.dot(lo, wub_ref[0:half, :], preferred_element_type=F32)
             + jnp.dot(hi, wub_ref[half:, :], preferred_element_type=F32))
        a_ref[0:n, :] = ((g / (1.0 + jnp.exp(-g))) * u).astype(BF16)
    _for_live_rows(nrows_ref[m], hs_ref.shape[0], up)


def _moe_down_kernel(te_ref, tix_ref, first_ref, nrows_ref, a_ref, wd_ref, ys_ref, wdb_ref):
    m = pl.program_id(1)

    @pl.when(first_ref[m] == 1)
    def _():
        wdb_ref[...] = wd_ref[...].astype(BF16)

    def down(n):
        ys_ref[0:n, :] = jnp.dot(a_ref[0:n, :], wdb_ref[...], preferred_element_type=F32)
    _for_live_rows(nrows_ref[m], a_ref.shape[0], down)


def _combine_kernel(pos_ref, ys_ref, rp_ref, x_ref, mod_ref, g3_ref, *rest, n_split):
    if n_split is None:
        out_ref, buf_ref, sem = rest
    else:
        out_a_ref, out_b_ref, buf_ref, sem = rest
    m = pl.program_id(0)
    n = pl.num_programs(0)
    tm = x_ref.shape[0]

    def issue_tile(tile, slot):
        def issue(r, carry):
            t = tile * tm + r
            _row_copy(ys_ref, pos_ref[2 * t], buf_ref.at[slot], r, sem.at[slot]).start()
            _row_copy(ys_ref, pos_ref[2 * t + 1], buf_ref.at[slot], tm + r, sem.at[slot]).start()
            return carry
        lax.fori_loop(0, tm, issue, 0, unroll=8)

    @pl.when(m == 0)
    def _():
        issue_tile(0, 0)

    @pl.when(m + 1 < n)
    def _():
        issue_tile(m + 1, (m + 1) % 2)

    slot = m % 2

    _rows_wait(ys_ref, 2 * tm, sem.at[slot])

    def result(rows, second):
        y = rp_ref[rows, 0:1] * buf_ref[slot, rows, :] + rp_ref[rows, 1:2] * buf_ref[slot, second, :]
        return x_ref[rows, :] + mod_ref[5:6, :] * (_rms(y) * g3_ref[...])

    def chunked(dst_ref):
        def body(c, carry):
            r0 = pl.multiple_of(c * ROW_CHUNK, ROW_CHUNK)
            dst_ref[pl.ds(r0, ROW_CHUNK), :] = result(pl.ds(r0, ROW_CHUNK), pl.ds(tm + r0, ROW_CHUNK))
            return carry
        lax.fori_loop(0, tm // ROW_CHUNK, body, 0)

    if n_split is None:
        chunked(out_ref)
    else:
        @pl.when(m < n_split)
        def _():
            chunked(out_a_ref)

        @pl.when(m >= n_split)
        def _():
            chunked(out_b_ref)


def _moe_layer(x, mods, g4, wr, w_e_gate, w_e_up, w_e_down, *, l, i, tm, tf, mrow, n_split=None):
    T, D = x.shape
    NE, EFF = w_e_gate.shape[1], w_e_gate.shape[3]
    n_tiles = T // tm
    tmu = 2 * tm
    n_mt = (TOP_K * T) // tmu + NE
    Tp = n_mt * tmu
    tn = 512
    assert D % tn == 0 and EFF % tf == 0 and (TOP_K * T) % tmu == 0
    sds = jax.ShapeDtypeStruct

    ri, rp, cnt = pl.pallas_call(
        functools.partial(_route_kernel, n_experts=NE),
        grid=(n_tiles,),
        in_specs=[
            pl.BlockSpec((tm, D), lambda m: (m, 0)),
            pl.BlockSpec((None, None, N_MOD, D), lambda m: (l, mrow(m), 0, 0)),
            pl.BlockSpec((None, 1, D), lambda m: (4 * l + 2, 0, 0)),
            pl.BlockSpec((None, D, LANES), lambda m: (i, 0, 0)),
        ],
        out_specs=[
            pl.BlockSpec((tm, LANES), lambda m: (m, 0)),
            pl.BlockSpec((tm, LANES), lambda m: (m, 0)),
            pl.BlockSpec((SUBLANES, LANES), lambda m: (0, 0)),
        ],
        out_shape=[sds((T, LANES), jnp.int32), sds((T, LANES), F32), sds((SUBLANES, LANES), F32)],
        scratch_shapes=[pltpu.VMEM((SUBLANES, LANES), F32)],
        compiler_params=_params(("arbitrary",)),
        name="moe_route",
    )(x, mods, g4, wr)

    cnt = cnt[0, :NE].astype(jnp.int32)
    pc = (cnt + tmu - 1) // tmu * tmu
    starts = jnp.cumsum(pc) - pc
    pos = jnp.stack([starts[ri[:, 0]] + ri[:, 2], starts[ri[:, 1]] + ri[:, 3]], axis=1).reshape(-1)
    zero_fill = (-cnt) % MOE_SUB

    def tile_list(t, n_max):
        per = (cnt + t - 1) // t
        upto = jnp.cumsum(per)
        ar = jnp.arange(n_max, dtype=jnp.int32)
        k = jnp.minimum(ar, upto[-1] - 1)
        te = jnp.minimum(jnp.sum(k[:, None] >= upto[None, :], axis=1), NE - 1).astype(jnp.int32)
        off = k - (upto[te] - per[te])
        tix = (starts[te] // t + off).astype(jnp.int32)
        first = jnp.logical_or(te != jnp.roll(te, 1), ar == 0).astype(jnp.int32)
        live = jnp.clip(cnt[te] - off * t, 0, t)
        nrows = jnp.where(ar < upto[-1], (live + MOE_SUB - 1) // MOE_SUB * MOE_SUB, 0).astype(jnp.int32)
        return te, tix, first, nrows

    te_u, tix_u, first_u, nrows_u = tile_list(tmu, n_mt)
    n_mt_d = (TOP_K * T) // tm + NE
    te_d, tix_d, first_d, nrows_d = tile_list(tm, n_mt_d)

    hs = pl.pallas_call(
        functools.partial(_dispatch_kernel, n_experts=NE),
        grid_spec=pltpu.PrefetchScalarGridSpec(
            num_scalar_prefetch=3, grid=(n_tiles,),
            in_specs=[
                pl.BlockSpec((tm, D), lambda m, pos, ps, pn: (m, 0)),
                pl.BlockSpec((None, None, N_MOD, D), lambda m, pos, ps, pn: (l, mrow(m), 0, 0)),
                pl.BlockSpec((None, 1, D), lambda m, pos, ps, pn: (4 * l + 2, 0, 0)),
            ],
            out_specs=pl.BlockSpec(memory_space=pl.ANY),
            scratch_shapes=[pltpu.VMEM((2, tm, D // 2), jnp.uint32),
                            pltpu.VMEM((SUBLANES, D // 2), jnp.uint32),
                            pltpu.SemaphoreType.DMA((2,)), pltpu.SemaphoreType.DMA(())]),
        out_shape=sds((Tp, D // 2), jnp.uint32),
        compiler_params=_params(("arbitrary",)),
        name="moe_dispatch",
    )(pos, starts + cnt, zero_fill, x, mods, g4)

    a = pl.pallas_call(
        _moe_up_kernel,
        grid_spec=pltpu.PrefetchScalarGridSpec(
            num_scalar_prefetch=4, grid=(EFF // tf, n_mt),
            in_specs=[
                pl.BlockSpec((tmu, D // 2), lambda f, m, te, tix, fi, nu: (tix[m], 0)),
                pl.BlockSpec((None, None, D, tf), lambda f, m, te, tix, fi, nu: (i, te[m], 0, f)),
                pl.BlockSpec((None, None, D, tf), lambda f, m, te, tix, fi, nu: (i, te[m], 0, f)),
            ],
            out_specs=pl.BlockSpec((tmu, tf), lambda f, m, te, tix, fi, nu: (tix[m], f)),
            scratch_shapes=[pltpu.VMEM((D, tf), BF16), pltpu.VMEM((D, tf), BF16)]),
        out_shape=sds((Tp, EFF), BF16),
        compiler_params=_params(("arbitrary", "arbitrary")),
        name="moe_up",
    )(te_u, tix_u, first_u, nrows_u, hs, w_e_gate, w_e_up)

    ys = pl.pallas_call(
        _moe_down_kernel,
        grid_spec=pltpu.PrefetchScalarGridSpec(
            num_scalar_prefetch=4, grid=(D // tn, n_mt_d),
            in_specs=[
                pl.BlockSpec((tm, EFF), lambda n, m, te, tix, fi, nu: (tix[m], 0)),
                pl.BlockSpec((None, None, EFF, tn), lambda n, m, te, tix, fi, nu: (i, te[m], 0, n)),
            ],
            out_specs=pl.BlockSpec((tm, tn), lambda n, m, te, tix, fi, nu: (tix[m], n)),
            scratch_shapes=[pltpu.VMEM((EFF, tn), BF16)]),
        out_shape=sds((Tp, D), F32),
        compiler_params=_params(("arbitrary", "arbitrary")),
        name="moe_down",
    )(te_d, tix_d, first_d, nrows_d, a, w_e_down)

    if n_split is None:
        out_specs = pl.BlockSpec((tm, D), lambda m, pos: (m, 0))
        out_shape = sds((T, D), F32)
    else:
        out_specs = [pl.BlockSpec((tm, D), lambda m, pos: (jnp.minimum(m, n_split - 1), 0)),
                     pl.BlockSpec((tm, D), lambda m, pos: (jnp.maximum(m - n_split, 0), 0))]
        out_shape = [sds((n_split * tm, D), F32), sds((T - n_split * tm, D), F32)]
    return pl.pallas_call(
        functools.partial(_combine_kernel, n_split=n_split),
        grid_spec=pltpu.PrefetchScalarGridSpec(
            num_scalar_prefetch=1, grid=(n_tiles,),
            in_specs=[
                pl.BlockSpec(memory_space=pl.ANY),
                pl.BlockSpec((tm, LANES), lambda m, pos: (m, 0)),
                pl.BlockSpec((tm, D), lambda m, pos: (m, 0)),
                pl.BlockSpec((None, None, N_MOD, D), lambda m, pos: (l, mrow(m), 0, 0)),
                pl.BlockSpec((None, 1, D), lambda m, pos: (4 * l + 3, 0, 0)),
            ],
            out_specs=out_specs,
            scratch_shapes=[pltpu.VMEM((2, 2 * tm, D), F32), pltpu.SemaphoreType.DMA((2,))]),
        out_shape=out_shape,
        compiler_params=_params(("arbitrary",)),
        name="moe_combine",
    )(pos, ys, rp, x, mods, g4)


def kernel(x_prompt, x_sample, cache_ckv, cache_krope, c, c_ctx, w_in, g_q, g_kv, w_uq, w_uk, w_uv,
           w_conv, w_o, w_ada, b_ada, g_sand, w_ff_gate, w_ff_up, w_ff_down, w_router,
           w_e_gate, w_e_up, w_e_down):
    B, N, D = x_prompt.shape
    DB, DN, _ = x_sample.shape
    L = w_in.shape[0]
    QL, KV = g_q.shape[1], g_kv.shape[1]
    H, NOPE = w_uk.shape[2], w_uk.shape[3]
    ROPE = w_uq.shape[3] - NOPE
    VD = w_uv.shape[3]
    CONV = w_conv.shape[2]
    AW = H * VD
    PAST = cache_ckv.shape[2]
    FF = w_ff_gate.shape[2]
    NE, EFF = w_e_gate.shape[1], w_e_gate.shape[3]
    T_P, T_S = B * N, DB * DN
    T = T_P + T_S
    DK = KV + LANES
    NA = 1024
    tm = TM
    assert ROPE == 64 and 2 * ROPE == LANES and NOPE % LANES == 0 and KV % LANES == 0 and QL % LANES == 0
    assert QL + KV + LANES <= NA and CONV % LANES == 0 and AW + CONV == w_o.shape[1]
    assert T_P % tm == 0 and DN % tm == 0 and tm % N == 0 and T_P % DN == 0
    assert N & (N - 1) == 0 and DN & (DN - 1) == 0 and DN % GRID_W == 0
    assert NE <= LANES and DB + 1 <= SUBLANES
    n_p_tiles = T_P // tm
    tiles_per_s = DN // tm
    n_tiles = T // tm
    scale = float(NOPE + ROPE) ** -0.5
    tf = 512
    assert FF % tf == 0 and EFF % tf == 0

    def mrow(m):
        return jnp.where(m < n_p_tiles, 0, 1 + (m - n_p_tiles) // tiles_per_s)

    def trow(m):
        return jnp.where(m < n_p_tiles, 0, 1 + (m - n_p_tiles) % tiles_per_s)

    t = jnp.arange(DN)
    inv = 1.0 / (ROPE_THETA ** (jnp.arange(0, ROPE // 2, 2, dtype=F32) / (ROPE // 2)))
    ar = (t // GRID_W).astype(F32)[:, None] * inv
    ac = (t % GRID_W).astype(F32)[:, None] * inv
    ang = jnp.concatenate([ar, ar, ac, ac] * 2, axis=-1)
    first = (jnp.arange(LANES) % (ROPE // 2)) < (ROPE // 4)
    cos_t = jnp.concatenate([jnp.ones((tm, LANES), F32), jnp.cos(ang)], axis=0)
    sin = jnp.sin(ang)
    sa_t = jnp.concatenate([jnp.zeros((tm, LANES), F32), jnp.where(first, -sin, 0.0)], axis=0)
    sb_t = jnp.concatenate([jnp.zeros((tm, LANES), F32), jnp.where(first, 0.0, sin)], axis=0)

    s1, s2 = QL + KV + ROPE, QL + KV + ROPE + 3 * CONV
    w_in_b = w_in.astype(BF16)
    w_a = jnp.pad(w_in_b[:, :, :s1], ((0, 0), (0, 0), (0, NA - s1)))
    w_b = w_in_b[:, :, s1:s2]
    wq = jnp.pad(w_uq, ((0, 0), (0, 0), (0, 0), (0, LANES - ROPE))).astype(BF16)
    wq = wq.reshape(L, QL, H * (NOPE + LANES))
    wk = w_uk.transpose(0, 2, 3, 1).astype(BF16)
    wuv = w_uv.transpose(0, 2, 1, 3).astype(BF16)
    wo = w_o.astype(BF16)
    wfg, wfu, wfd = w_ff_gate.astype(BF16), w_ff_up.astype(BF16), w_ff_down.astype(BF16)
    wr = jnp.pad(w_router, ((0, 0), (0, 0), (0, LANES - NE)))
    kcache = jnp.concatenate(
        [cache_ckv, cache_krope, jnp.zeros(cache_krope.shape[:-1] + (LANES - ROPE,), F32)],
        axis=-1).astype(BF16)

    cond8 = jnp.concatenate([c_ctx[None, :], c, jnp.zeros((SUBLANES - 1 - DB, D), F32)], axis=0)
    mods = _ada(cond8, w_ada, b_ada)[:, :1 + DB].reshape(L, 1 + DB, N_MOD, D)

    xs = (x_prompt.reshape(T_P, D), x_sample.reshape(T_S, D))
    p_idx = lambda m: jnp.minimum(m, n_p_tiles - 1)
    s_idx = lambda m: jnp.maximum(m - n_p_tiles, 0)

    mod_spec2 = lambda l: pl.BlockSpec((None, None, N_MOD, D), lambda m, n: (l, mrow(m), 0, 0))
    row_spec2 = lambda w: pl.BlockSpec((tm, w), lambda m, n: (m, 0))
    tab_spec2 = pl.BlockSpec((tm, LANES), lambda m, n: (trow(m), 0))
    tab_spec1 = pl.BlockSpec((tm, LANES), lambda m: (trow(m), 0))

    ckv_list, kr_list = [], []
    for l in range(L):
        if l > 0:
            xs = (x,)
        cqn, ckvn, kr, kcat, gates = pl.pallas_call(
            functools.partial(_inproj_kernel, QL=QL, KV=KV, ROPE=ROPE, n_x=len(xs), n_p_tiles=n_p_tiles),
            grid=(n_tiles, 1 + 3),
            in_specs=([row_spec2(D)] if len(xs) == 1 else
                      [pl.BlockSpec((tm, D), lambda m, n: (p_idx(m), 0)),
                       pl.BlockSpec((tm, D), lambda m, n: (s_idx(m), 0))]) + [
                mod_spec2(l),
                pl.BlockSpec((None, 1, D), lambda m, n: (4 * l, 0, 0)),
                pl.BlockSpec((None, 1, QL), lambda m, n: (l, 0, 0)),
                pl.BlockSpec((None, 1, KV), lambda m, n: (l, 0, 0)),
                tab_spec2, tab_spec2, tab_spec2,
                pl.BlockSpec((None, D, NA), lambda m, n: (l, 0, 0)),
                pl.BlockSpec((None, D, CONV), lambda m, n: (l, 0, jnp.maximum(n - 1, 0))),
            ],
            out_specs=[
                row_spec2(QL), row_spec2(KV), row_spec2(ROPE), row_spec2(DK),
                pl.BlockSpec((tm, CONV), lambda m, n: (m, jnp.maximum(n - 1, 0))),
            ],
            out_shape=[
                jax.ShapeDtypeStruct((T, QL), BF16),
                jax.ShapeDtypeStruct((T, KV), F32),
                jax.ShapeDtypeStruct((T, ROPE), F32),
                jax.ShapeDtypeStruct((T, DK), BF16),
                jax.ShapeDtypeStruct((T, 3 * CONV), BF16),
            ],
            scratch_shapes=[pltpu.VMEM((tm, D), BF16)],
            compiler_params=_params(("arbitrary", "arbitrary")),
            name="in_proj",
        )(*xs, mods, g_sand.reshape(4 * L, 1, D), g_q.reshape(L, 1, QL), g_kv.reshape(L, 1, KV),
          cos_t, sa_t, sb_t, w_a, w_b)
        ckv_list.append(ckvn[:T_P].reshape(B, N, KV))
        kr_list.append(kr[:T_P].reshape(B, N, ROPE))

        qcat = pl.pallas_call(
            functools.partial(_q_kernel, H=H, NOPE=NOPE, KV=KV, scale=scale),
            grid=(n_tiles,),
            in_specs=[
                pl.BlockSpec((tm, QL), lambda m: (m, 0)),
                pl.BlockSpec((None, QL, H * (NOPE + LANES)), lambda m: (l, 0, 0)),
                pl.BlockSpec((None, H, NOPE, KV), lambda m: (l, 0, 0, 0)),
                tab_spec1, tab_spec1, tab_spec1,
            ],
            out_specs=pl.BlockSpec((tm, H * DK), lambda m: (m, 0)),
            out_shape=jax.ShapeDtypeStruct((T, H * DK), BF16),
            compiler_params=_params(("arbitrary",)),
            name="q_proj",
        )(cqn, wq, wk, cos_t, sa_t, sb_t)

        o_p = pl.pallas_call(
            functools.partial(_attn_kernel, H=H, KV=KV, has_cache=False),
            grid=(B,),
            in_specs=[
                pl.BlockSpec((N, H * DK), lambda b: (b, 0)),
                pl.BlockSpec((N, DK), lambda b: (b, 0)),
            ],
            out_specs=pl.BlockSpec((N, H * KV), lambda b: (b, 0)),
            out_shape=jax.ShapeDtypeStruct((T_P, H * KV), BF16),
            compiler_params=_params(("arbitrary",)),
            name="attn_prompt",
        )(qcat, kcat)
        tq = 256
        qs_per = DN // tq
        o_s = pl.pallas_call(
            functools.partial(_attn_kernel, H=H, KV=KV, has_cache=True),
            grid=(DB, qs_per),
            in_specs=[
                pl.BlockSpec((tq, H * DK), lambda b, i: (T_P // tq + b * qs_per + i, 0)),
                pl.BlockSpec((DN, DK), lambda b, i: (T_P // DN + b, 0)),
                pl.BlockSpec((None, None, PAST, DK), lambda b, i: (b, l, 0, 0)),
            ],
            out_specs=pl.BlockSpec((tq, H * KV), lambda b, i: (b * qs_per + i, 0)),
            out_shape=jax.ShapeDtypeStruct((T_S, H * KV), BF16),
            compiler_params=_params(("arbitrary", "arbitrary")),
            name="attn_sample",
        )(qcat, kcat, kcache)

        r8 = tm // HALO_ROWS
        prev8 = lambda m: jnp.maximum(m * r8 - 1, 0)
        next8 = lambda m: jnp.minimum((m + 1) * r8, T // HALO_ROWS - 1)
        x = pl.pallas_call(
            functools.partial(_mix_kernel, H=H, KV=KV, VD=VD, n_p_tiles=n_p_tiles,
                              seq_p=N, seq_s=DN, t_p=T_P, n_x=len(xs)),
            grid=(n_tiles,),
            in_specs=([pl.BlockSpec((tm, D), lambda m: (m, 0))] if len(xs) == 1 else
                      [pl.BlockSpec((tm, D), lambda m: (p_idx(m), 0)),
                       pl.BlockSpec((tm, D), lambda m: (s_idx(m), 0))]) + [
                pl.BlockSpec((tm, H * KV), lambda m: (p_idx(m), 0)),
                pl.BlockSpec((tm, H * KV), lambda m: (s_idx(m), 0)),
                pl.BlockSpec((tm, CONV), lambda m: (m, 0)),
                pl.BlockSpec((tm, CONV), lambda m: (m, 1)),
                pl.BlockSpec((tm, CONV), lambda m: (m, 2)),
                pl.BlockSpec((HALO_ROWS, CONV), lambda m: (prev8(m), 1)),
                pl.BlockSpec((HALO_ROWS, CONV), lambda m: (prev8(m), 2)),
                pl.BlockSpec((HALO_ROWS, CONV), lambda m: (next8(m), 1)),
                pl.BlockSpec((HALO_ROWS, CONV), lambda m: (next8(m), 2)),
                pl.BlockSpec((None, H, KV, VD), lambda m: (l, 0, 0, 0)),
                pl.BlockSpec((None, 3, CONV), lambda m: (l, 0, 0)),
                pl.BlockSpec((None, AW + CONV, D), lambda m: (l, 0, 0)),
                pl.BlockSpec((None, 1, D), lambda m: (4 * l + 1, 0, 0)),
                pl.BlockSpec((None, None, N_MOD, D), lambda m: (l, mrow(m), 0, 0)),
            ],
            out_specs=pl.BlockSpec((tm, D), lambda m: (m, 0)),
            out_shape=jax.ShapeDtypeStruct((T, D), F32),
            scratch_shapes=[pltpu.VMEM((tm, AW + CONV), BF16),
                            pltpu.VMEM((tm + 2 * SUBLANES, CONV), F32)],
            compiler_params=_params(("arbitrary",)),
            name="mixer_out",
        )(*xs, o_p, o_s, gates, gates, gates, gates, gates, gates, gates, wuv, w_conv, wo,
          g_sand.reshape(4 * L, 1, D), mods)

        i = l // 2
        if l % 2 == 0:
            tmf = TM_FFN if (T_P % TM_FFN == 0 and DN % TM_FFN == 0) else tm
            mrow_f = lambda m: jnp.where(m < T_P // tmf, 0, 1 + (m - T_P // tmf) // (DN // tmf))
            x = pl.pallas_call(
                _ffn_kernel,
                grid=(T // tmf, FF // tf),
                in_specs=[
                    pl.BlockSpec((tmf, D), lambda m, f: (m, 0), pipeline_mode=pl.Buffered(1)),
                    pl.BlockSpec((None, None, N_MOD, D), lambda m, f: (l, mrow_f(m), 0, 0)),
                    pl.BlockSpec((None, 1, D), lambda m, f: (4 * l + 2, 0, 0)),
                    pl.BlockSpec((None, 1, D), lambda m, f: (4 * l + 3, 0, 0)),
                    pl.BlockSpec((None, D, tf), lambda m, f: (i, 0, f)),
                    pl.BlockSpec((None, D, tf), lambda m, f: (i, 0, f)),
                    pl.BlockSpec((None, tf, D), lambda m, f: (i, f, 0)),
                ],
                out_specs=pl.BlockSpec((tmf, D), lambda m, f: (m, 0)),
                out_shape=jax.ShapeDtypeStruct((T, D), F32),
                scratch_shapes=[pltpu.VMEM((tmf, D), BF16)],
                compiler_params=_params(("arbitrary", "arbitrary")),
                name="ffn_dense",
            )(x, mods, g_sand.reshape(4 * L, 1, D), g_sand.reshape(4 * L, 1, D), wfg, wfu, wfd)
        else:
            x = _moe_layer(x, mods, g_sand.reshape(4 * L, 1, D), wr, w_e_gate, w_e_up, w_e_down,
                           l=l, i=i, tm=tm, tf=tf, mrow=mrow,
                           n_split=n_p_tiles if l == L - 1 else None)

    x_p, x_s = x if isinstance(x, (list, tuple)) else (x[:T_P], x[T_P:])
    y_prompt = x_p.reshape(B, N, D)
    y_sample = x_s.reshape(DB, DN, D)
    new_ckv = jnp.stack(ckv_list, axis=1)
    new_krope = jnp.stack(kr_list, axis=1)
    return (y_prompt, y_sample, new_ckv, new_krope)
```

```python
import functools

import jax
import jax.numpy as jnp
from jax import lax
from jax.experimental import pallas as pl
from jax.experimental.pallas import tpu as pltpu

GRID_W = 64
ROPE_THETA = 10000.0
EPS = 1e-6
N_MOD = 6
TOP_K = 2

LANES = 128
SUBLANES = 8
TM = 512
TM_FFN = 1024
ROW_CHUNK = 256
HALO_ROWS = 16
MOE_SUB = 256
VMEM_LIMIT = 56 * 1024 * 1024

BF16 = jnp.bfloat16
F32 = jnp.float32


def _params(sem, vmem=VMEM_LIMIT):
    return pltpu.CompilerParams(dimension_semantics=sem, vmem_limit_bytes=vmem)


def _rms(x):
    return x * lax.rsqrt(jnp.mean(x * x, axis=-1, keepdims=True) + EPS)


def _rope(x, cos, sa, sb):
    return x * cos + pltpu.roll(x, LANES - 16, 1) * sa + pltpu.roll(x, 16, 1) * sb


def _ada_kernel(c_ref, w_ref, b_ref, o_ref):
    c = c_ref[...]
    s = c / (1.0 + jnp.exp(-c))
    o_ref[...] = jnp.dot(s.astype(BF16), w_ref[...].astype(BF16),
                         preferred_element_type=F32) + b_ref[...]


def _ada(cond8, w_ada, b_ada):
    L, D, N6 = w_ada.shape
    tn = 1024 if N6 % 1024 == 0 else N6
    return pl.pallas_call(
        _ada_kernel,
        grid=(L, N6 // tn),
        in_specs=[
            pl.BlockSpec((SUBLANES, D), lambda l, n: (0, 0)),
            pl.BlockSpec((None, D, tn), lambda l, n: (l, 0, n)),
            pl.BlockSpec((None, 1, tn), lambda l, n: (l, 0, n)),
        ],
        out_specs=pl.BlockSpec((None, SUBLANES, tn), lambda l, n: (l, 0, n)),
        out_shape=jax.ShapeDtypeStruct((L, SUBLANES, N6), F32),
        compiler_params=_params(("arbitrary", "arbitrary")),
        name="ada_mod",
    )(cond8, w_ada, b_ada.reshape(L, 1, N6))


def _tile_rows(refs, is_first, rows):
    if len(refs) == 1:
        return refs[0][rows, :]
    return jnp.where(is_first, refs[0][rows, :], refs[1][rows, :])


def _inproj_kernel(*refs, QL, KV, ROPE, n_x, n_p_tiles, n_alias):
    x_refs, refs = refs[:n_x], refs[n_x + n_alias:]
    (mod_ref, g_ref, gq_ref, gkv_ref, cos_ref, sa_ref, sb_ref, wa_ref, wb_ref,
     cqn_ref, ckv_ref, kr_ref, kcat_ref, gates_ref, hb_ref) = refs
    n = pl.program_id(1)
    is_p = pl.program_id(0) < n_p_tiles
    tm = hb_ref.shape[0]
    seq = ckv_ref.shape[1]

    def first_step(store_cache):
        for r0 in range(0, tm, ROW_CHUNK):
            rows = slice(r0, r0 + ROW_CHUNK)
            h = _rms(_tile_rows(x_refs, is_p, rows)) * g_ref[...]
            h = h * (1.0 + mod_ref[1:2, :]) + mod_ref[0:1, :]
            hb_ref[rows, :] = h.astype(BF16)
            acc = jnp.dot(hb_ref[rows, :], wa_ref[...], preferred_element_type=F32)
            cqn_ref[rows, :] = (_rms(acc[:, :QL]) * gq_ref[...]).astype(BF16)
            ckv = _rms(acc[:, QL:QL + KV]) * gkv_ref[...]
            kr = _rope(acc[:, QL + KV:QL + KV + LANES], cos_ref[rows, :], sa_ref[rows, :], sb_ref[rows, :])
            if store_cache:
                for j in range(ROW_CHUNK // seq):
                    ckv_ref[r0 // seq + j] = ckv[j * seq:(j + 1) * seq, :]
                    kr_ref[r0 // seq + j] = kr[j * seq:(j + 1) * seq, :ROPE]
            kcat_ref[rows, :KV] = ckv.astype(BF16)
            kcat_ref[rows, KV:] = kr.astype(BF16)

    pl.when(jnp.logical_and(n == 0, is_p))(functools.partial(first_step, True))
    pl.when(jnp.logical_and(n == 0, jnp.logical_not(is_p)))(functools.partial(first_step, False))

    @pl.when(n > 0)
    def _():
        gates_ref[...] = jnp.dot(hb_ref[...], wb_ref[...], preferred_element_type=F32).astype(BF16)


def _q_kernel(cqn_ref, wq_ref, wk_ref, cos_ref, sa_ref, sb_ref, o_ref, *, H, NOPE, KV, scale):
    QW = NOPE + LANES
    DK = KV + LANES
    cqn = cqn_ref[...]
    for h in range(H):
        q = jnp.dot(cqn, wq_ref[:, h * QW:(h + 1) * QW], preferred_element_type=F32)
        q_lat = jnp.dot(q[:, :NOPE].astype(BF16), wk_ref[h], preferred_element_type=F32)
        q_rope = _rope(q[:, NOPE:], cos_ref[...], sa_ref[...], sb_ref[...])
        o_ref[:, h * DK:h * DK + KV] = (q_lat * scale).astype(BF16)
        o_ref[:, h * DK + KV:(h + 1) * DK] = (q_rope * scale).astype(BF16)


def _attn_kernel(*refs, H, KV, has_cache):
    if has_cache:
        q_ref, kn_ref, kc_ref, o_ref = refs
    else:
        q_ref, kn_ref, o_ref = refs
    DK = KV + LANES
    dn = (((1,), (1,)), ((), ()))
    kn = kn_ref[...]
    vn = kn[:, :KV]
    if has_cache:
        kc = kc_ref[...]
        vc = kc[:, :KV]
    for h in range(H):
        q = q_ref[:, h * DK:(h + 1) * DK]
        s = lax.dot_general(q, kn, dn, preferred_element_type=F32)
        m = jnp.max(s, axis=-1, keepdims=True)
        if has_cache:
            sc = lax.dot_general(q, kc, dn, preferred_element_type=F32)
            m = jnp.maximum(m, jnp.max(sc, axis=-1, keepdims=True))
        p = jnp.exp(s - m)
        l = jnp.sum(p, axis=-1, keepdims=True)
        o = jnp.dot(p.astype(BF16), vn, preferred_element_type=F32)
        if has_cache:
            pc = jnp.exp(sc - m)
            l = l + jnp.sum(pc, axis=-1, keepdims=True)
            o = o + jnp.dot(pc.astype(BF16), vc, preferred_element_type=F32)
        o_ref[:, h * KV:(h + 1) * KV] = (o / l).astype(BF16)


def _mix_kernel(*refs, H, KV, VD, n_p_tiles, seq_p, seq_s, t_p, n_x):
    x_refs, refs = refs[:n_x], refs[n_x:]
    (op_ref, os_ref, gb_ref, gc_ref, u_ref, gcp_ref, up_ref, gcn_ref, un_ref,
     wuv_ref, wc_ref, wo_ref, g_ref, mod_ref, out_ref, cat_ref, vs_ref) = refs
    m = pl.program_id(0)
    tm = out_ref.shape[0]
    AW = H * VD
    is_p = m < n_p_tiles
    seq = jnp.where(is_p, seq_p, seq_s)
    row0 = m * tm - jnp.where(is_p, 0, t_p)

    hr = HALO_ROWS
    vs_ref[SUBLANES:SUBLANES + tm, :] = gc_ref[...].astype(F32) * u_ref[...].astype(F32)
    vs_ref[SUBLANES - 1:SUBLANES, :] = gcp_ref[hr - 1:hr, :].astype(F32) * up_ref[hr - 1:hr, :].astype(F32)
    vs_ref[SUBLANES + tm:SUBLANES + tm + 1, :] = gcn_ref[0:1, :].astype(F32) * un_ref[0:1, :].astype(F32)

    for r0 in range(0, tm, ROW_CHUNK):
        rows = slice(r0, r0 + ROW_CHUNK)
        for h in range(H):
            cols = slice(h * KV, (h + 1) * KV)
            o = jnp.where(is_p, op_ref[rows, cols], os_ref[rows, cols])
            a = jnp.dot(o, wuv_ref[h], preferred_element_type=F32)
            cat_ref[rows, h * VD:(h + 1) * VD] = a.astype(BF16)
        pos = jnp.bitwise_and(lax.broadcasted_iota(jnp.int32, (ROW_CHUNK, 1), 0) + (row0 + r0), seq - 1)
        v = vs_ref[SUBLANES + r0:SUBLANES + r0 + ROW_CHUNK, :]
        vp = jnp.where(pos == 0, 0.0, vs_ref[SUBLANES - 1 + r0:SUBLANES - 1 + r0 + ROW_CHUNK, :])
        vn = jnp.where(pos == seq - 1, 0.0, vs_ref[SUBLANES + 1 + r0:SUBLANES + 1 + r0 + ROW_CHUNK, :])
        conv = vp * wc_ref[0:1, :] + v * wc_ref[1:2, :] + vn * wc_ref[2:3, :]
        cat_ref[rows, AW:] = (gb_ref[rows, :].astype(F32) * conv).astype(BF16)
        y = jnp.dot(cat_ref[rows, :], wo_ref[...], preferred_element_type=F32)
        out_ref[rows, :] = _tile_rows(x_refs, is_p, rows) + mod_ref[2:3, :] * (_rms(y) * g_ref[...])


def _modulate_ffn(x_ref, mod_ref, g_ref):
    h = _rms(x_ref[...]) * g_ref[...]
    return h * (1.0 + mod_ref[4:5, :]) + mod_ref[3:4, :]


def _swiglu_step(hb, wg_ref, wu_ref, wd_ref):
    g = jnp.dot(hb, wg_ref[...], preferred_element_type=F32)
    u = jnp.dot(hb, wu_ref[...], preferred_element_type=F32)
    a = (g / (1.0 + jnp.exp(-g))) * u
    return jnp.dot(a.astype(BF16), wd_ref[...], preferred_element_type=F32)


def _for_row_chunks(n_rows, chunk, fn):
    def body(c, carry):
        fn(pl.ds(pl.multiple_of(c * chunk, chunk), chunk))
        return carry
    lax.fori_loop(0, n_rows // chunk, body, 0)


def _ffn_kernel(x_ref, mod_ref, g2_ref, g3_ref, wg_ref, wu_ref, wd_ref, out_ref, hb_ref):
    f = pl.program_id(1)
    nf = pl.num_programs(1)
    tm = x_ref.shape[0]

    @pl.when(f == 0)
    def _():
        def prologue(rows):
            h = _rms(x_ref[rows, :]) * g2_ref[...]
            hb_ref[rows, :] = (h * (1.0 + mod_ref[4:5, :]) + mod_ref[3:4, :]).astype(BF16)
            out_ref[rows, :] = jnp.zeros((ROW_CHUNK, out_ref.shape[1]), F32)
        _for_row_chunks(tm, ROW_CHUNK, prologue)

    out_ref[...] += _swiglu_step(hb_ref[...], wg_ref, wu_ref, wd_ref)

    @pl.when(f == nf - 1)
    def _():
        def epilogue(rows):
            y = _rms(out_ref[rows, :]) * g3_ref[...]
            out_ref[rows, :] = x_ref[rows, :] + mod_ref[5:6, :] * y
        _for_row_chunks(tm, ROW_CHUNK, epilogue)


def _top2(logits, n_experts):
    lane = lax.broadcasted_iota(jnp.int32, logits.shape, 1)
    neg = jnp.float32(-jnp.inf)
    l1 = jnp.where(lane < n_experts, logits, neg)
    m1 = jnp.max(l1, axis=-1, keepdims=True)
    i1 = jnp.min(jnp.where(l1 == m1, lane, LANES), axis=-1, keepdims=True)
    l2 = jnp.where(lane == i1, neg, l1)
    m2 = jnp.max(l2, axis=-1, keepdims=True)
    i2 = jnp.min(jnp.where(l2 == m2, lane, LANES), axis=-1, keepdims=True)
    e2 = jnp.exp(m2 - m1)
    return lane, i1, i2, 1.0 / (1.0 + e2), e2 / (1.0 + e2)


def _route_kernel(x_ref, mod_ref, g2_ref, wr_ref, ri_ref, rp_ref, cnt_ref, carry_ref,
                  *, n_experts):
    m = pl.program_id(0)
    tm = x_ref.shape[0]

    @pl.when(m == 0)
    def _():
        carry_ref[...] = jnp.zeros_like(carry_ref)

    h = _modulate_ffn(x_ref, mod_ref, g2_ref)
    h_hi = h.astype(BF16)
    h_lo = (h - h_hi.astype(F32)).astype(BF16)
    w = wr_ref[...]
    w_hi = w.astype(BF16)
    w_lo = (w - w_hi.astype(F32)).astype(BF16)
    logits = (jnp.dot(h_hi, w_hi, preferred_element_type=F32)
              + jnp.dot(h_hi, w_lo, preferred_element_type=F32)
              + jnp.dot(h_lo, w_hi, preferred_element_type=F32))
    lane, i1, i2, p1, p2 = _top2(logits, n_experts)
    sel = jnp.logical_or(lane == i1, lane == i2)
    r = lax.broadcasted_iota(jnp.int32, (tm, tm), 0)
    c = lax.broadcasted_iota(jnp.int32, (tm, tm), 1)
    before = jnp.where(r > c, 1.0, 0.0).astype(BF16)
    rank = jnp.dot(before, jnp.where(sel, 1.0, 0.0).astype(BF16), preferred_element_type=F32)
    rank = rank + carry_ref[0:1, :]
    r1 = jnp.sum(jnp.where(lane == i1, rank, 0.0), axis=-1, keepdims=True).astype(jnp.int32)
    r2 = jnp.sum(jnp.where(lane == i2, rank, 0.0), axis=-1, keepdims=True).astype(jnp.int32)
    carry_ref[0:1, :] = carry_ref[0:1, :] + jnp.sum(jnp.where(sel, 1.0, 0.0), axis=0, keepdims=True)
    ri_ref[...] = jnp.where(lane == 0, i1, jnp.where(lane == 1, i2,
                            jnp.where(lane == 2, r1, jnp.where(lane == 3, r2, 0))))
    rp_ref[...] = jnp.where(lane == 0, p1, jnp.where(lane == 1, p2, 0.0))
    cnt_ref[...] = carry_ref[...]


def _row_copy(src_ref, src_row, dst_ref, dst_row, sem):
    return pltpu.make_async_copy(src_ref.at[pl.ds(src_row, 1), :], dst_ref.at[pl.ds(dst_row, 1), :], sem)


def _rows_wait(any_ref, n_rows, sem):
    pltpu.make_async_copy(any_ref.at[pl.ds(0, n_rows), :], any_ref.at[pl.ds(0, n_rows), :], sem).wait()


def _dispatch_kernel(pos_ref, pad_start_ref, pad_cnt_ref, x_ref, mod_ref, g2_ref, hs_ref,
                     pk_ref, zero_ref, sem, pad_sem, *, n_experts):
    m = pl.program_id(0)
    n = pl.num_programs(0)
    tm, half = x_ref.shape[0], x_ref.shape[1] // 2
    slot = m % 2

    @pl.when(m >= 2)
    def _():
        _rows_wait(hs_ref, 2 * tm, sem.at[slot])

    def pack(rows):
        h = _rms(x_ref[rows, :]) * g2_ref[...]
        h = (h * (1.0 + mod_ref[4:5, :]) + mod_ref[3:4, :]).astype(BF16).astype(F32)
        bits = lax.bitcast_convert_type(h, jnp.uint32)
        pk_ref[slot, rows, :] = jnp.bitwise_or(bits[:, half:], jnp.right_shift(bits[:, :half], 16))
    _for_row_chunks(tm, ROW_CHUNK, pack)

    def issue(r, carry):
        t = m * tm + r
        _row_copy(pk_ref.at[slot], r, hs_ref, pos_ref[2 * t], sem.at[slot]).start()
        _row_copy(pk_ref.at[slot], r, hs_ref, pos_ref[2 * t + 1], sem.at[slot]).start()
        return carry

    lax.fori_loop(0, tm, issue, 0, unroll=8)

    @pl.when(m == 0)
    def _():
        zero_ref[...] = jnp.zeros_like(zero_ref)
        for e in range(n_experts):
            start, cnt = pad_start_ref[e], pad_cnt_ref[e]

            def pad_issue(i, carry):
                _row_copy(zero_ref, 0, hs_ref, start + i, pad_sem).start()
                return carry

            def pad_wait(i, carry):
                _row_copy(zero_ref, 0, hs_ref, 0, pad_sem).wait()
                return carry

            lax.fori_loop(0, cnt, pad_issue, 0)
            lax.fori_loop(0, cnt, pad_wait, 0)

    @pl.when(m == n - 1)
    def _():
        _rows_wait(hs_ref, 2 * tm, sem.at[slot])

        @pl.when(n >= 2)
        def _():
            _rows_wait(hs_ref, 2 * tm, sem.at[1 - slot])


def _for_live_rows(nrows, tm, fn):
    for n in range(MOE_SUB, tm + 1, MOE_SUB):
        pl.when(nrows == n)(functools.partial(fn, n))


def _moe_up_kernel(te_ref, tix_ref, first_ref, nrows_ref, hs_ref, wg_ref, wu_ref, a_ref,
                   wgb_ref, wub_ref):
    m = pl.program_id(1)

    @pl.when(first_ref[m] == 1)
    def _():
        wgb_ref[...] = wg_ref[...].astype(BF16)
        wub_ref[...] = wu_ref[...].astype(BF16)

    half = hs_ref.shape[1]

    def up(n):
        w = hs_ref[0:n, :]
        lo = lax.bitcast_convert_type(jnp.left_shift(w, 16), F32).astype(BF16)
        hi = lax.bitcast_convert_type(jnp.bitwise_and(w, jnp.uint32(0xFFFF0000)), F32).astype(BF16)
        g = (jnp.dot(lo, wgb_ref[0:half, :], preferred_element_type=F32)
             + jnp.dot(hi, wgb_ref[half:, :], preferred_element_type=F32))
        u = (jnp.dot(lo, wub_ref[0:half, :], preferred_element_type=F32)
             + jnp.dot(hi, wub_ref[half:, :], preferred_element_type=F32))
        a_ref[0:n, :] = ((g / (1.0 + jnp.exp(-g))) * u).astype(BF16)
    _for_live_rows(nrows_ref[m], hs_ref.shape[0], up)


def _moe_down_kernel(te_ref, tix_ref, first_ref, nrows_ref, a_ref, wd_ref, ys_ref, wdb_ref):
    m = pl.program_id(1)

    @pl.when(first_ref[m] == 1)
    def _():
        wdb_ref[...] = wd_ref[...].astype(BF16)

    def down(n):
        ys_ref[0:n, :] = jnp.dot(a_ref[0:n, :], wdb_ref[...], preferred_element_type=F32)
    _for_live_rows(nrows_ref[m], a_ref.shape[0], down)


def _combine_kernel(pos_ref, ys_ref, rp_ref, x_ref, mod_ref, g3_ref, *rest, n_split):
    if n_split is None:
        out_ref, buf_ref, sem = rest
    else:
        out_a_ref, out_b_ref, buf_ref, sem = rest
    m = pl.program_id(0)
    n = pl.num_programs(0)
    tm = x_ref.shape[0]

    def issue_tile(tile, slot):
        def issue(r, carry):
            t = tile * tm + r
            _row_copy(ys_ref, pos_ref[2 * t], buf_ref.at[slot], r, sem.at[slot]).start()
            _row_copy(ys_ref, pos_ref[2 * t + 1], buf_ref.at[slot], tm + r, sem.at[slot]).start()
            return carry
        lax.fori_loop(0, tm, issue, 0, unroll=8)

    @pl.when(m == 0)
    def _():
        issue_tile(0, 0)

    @pl.when(m + 1 < n)
    def _():
        issue_tile(m + 1, (m + 1) % 2)

    slot = m % 2

    _rows_wait(ys_ref, 2 * tm, sem.at[slot])

    def result(rows, second):
        y = rp_ref[rows, 0:1] * buf_ref[slot, rows, :] + rp_ref[rows, 1:2] * buf_ref[slot, second, :]
        return x_ref[rows, :] + mod_ref[5:6, :] * (_rms(y) * g3_ref[...])

    def chunked(dst_ref):
        def body(c, carry):
            r0 = pl.multiple_of(c * ROW_CHUNK, ROW_CHUNK)
            dst_ref[pl.ds(r0, ROW_CHUNK), :] = result(pl.ds(r0, ROW_CHUNK), pl.ds(tm + r0, ROW_CHUNK))
            return carry
        lax.fori_loop(0, tm // ROW_CHUNK, body, 0)

    if n_split is None:
        chunked(out_ref)
    else:
        @pl.when(m < n_split)
        def _():
            chunked(out_a_ref)

        @pl.when(m >= n_split)
        def _():
            chunked(out_b_ref)


def _moe_layer(x, mods, g4, wr, w_e_gate, w_e_up, w_e_down, *, l, i, tm, tf, mrow, n_split=None):
    T, D = x.shape
    NE, EFF = w_e_gate.shape[1], w_e_gate.shape[3]
    n_tiles = T // tm
    tmu = 2 * tm
    n_mt = (TOP_K * T) // tmu + NE
    Tp = n_mt * tmu
    tn = 512
    assert D % tn == 0 and EFF % tf == 0 and (TOP_K * T) % tmu == 0
    sds = jax.ShapeDtypeStruct

    ri, rp, cnt = pl.pallas_call(
        functools.partial(_route_kernel, n_experts=NE),
        grid=(n_tiles,),
        in_specs=[
            pl.BlockSpec((tm, D), lambda m: (m, 0)),
            pl.BlockSpec((None, None, N_MOD, D), lambda m: (l, mrow(m), 0, 0)),
            pl.BlockSpec((None, 1, D), lambda m: (4 * l + 2, 0, 0)),
            pl.BlockSpec((None, D, LANES), lambda m: (i, 0, 0)),
        ],
        out_specs=[
            pl.BlockSpec((tm, LANES), lambda m: (m, 0)),
            pl.BlockSpec((tm, LANES), lambda m: (m, 0)),
            pl.BlockSpec((SUBLANES, LANES), lambda m: (0, 0)),
        ],
        out_shape=[sds((T, LANES), jnp.int32), sds((T, LANES), F32), sds((SUBLANES, LANES), F32)],
        scratch_shapes=[pltpu.VMEM((SUBLANES, LANES), F32)],
        compiler_params=_params(("arbitrary",)),
        name="moe_route",
    )(x, mods, g4, wr)

    cnt = cnt[0, :NE].astype(jnp.int32)
    pc = (cnt + tmu - 1) // tmu * tmu
    starts = jnp.cumsum(pc) - pc
    pos = jnp.stack([starts[ri[:, 0]] + ri[:, 2], starts[ri[:, 1]] + ri[:, 3]], axis=1).reshape(-1)
    zero_fill = (-cnt) % MOE_SUB

    def tile_list(t, n_max):
        per = (cnt + t - 1) // t
        upto = jnp.cumsum(per)
        ar = jnp.arange(n_max, dtype=jnp.int32)
        k = jnp.minimum(ar, upto[-1] - 1)
        te = jnp.minimum(jnp.sum(k[:, None] >= upto[None, :], axis=1), NE - 1).astype(jnp.int32)
        off = k - (upto[te] - per[te])
        tix = (starts[te] // t + off).astype(jnp.int32)
        first = jnp.logical_or(te != jnp.roll(te, 1), ar == 0).astype(jnp.int32)
        live = jnp.clip(cnt[te] - off * t, 0, t)
        nrows = jnp.where(ar < upto[-1], (live + MOE_SUB - 1) // MOE_SUB * MOE_SUB, 0).astype(jnp.int32)
        return te, tix, first, nrows

    te_u, tix_u, first_u, nrows_u = tile_list(tmu, n_mt)
    n_mt_d = (TOP_K * T) // tm + NE
    te_d, tix_d, first_d, nrows_d = tile_list(tm, n_mt_d)

    hs = pl.pallas_call(
        functools.partial(_dispatch_kernel, n_experts=NE),
        grid_spec=pltpu.PrefetchScalarGridSpec(
            num_scalar_prefetch=3, grid=(n_tiles,),
            in_specs=[
                pl.BlockSpec((tm, D), lambda m, pos, ps, pn: (m, 0)),
                pl.BlockSpec((None, None, N_MOD, D), lambda m, pos, ps, pn: (l, mrow(m), 0, 0)),
                pl.BlockSpec((None, 1, D), lambda m, pos, ps, pn: (4 * l + 2, 0, 0)),
            ],
            out_specs=pl.BlockSpec(memory_space=pl.ANY),
            scratch_shapes=[pltpu.VMEM((2, tm, D // 2), jnp.uint32),
                            pltpu.VMEM((SUBLANES, D // 2), jnp.uint32),
                            pltpu.SemaphoreType.DMA((2,)), pltpu.SemaphoreType.DMA(())]),
        out_shape=sds((Tp, D // 2), jnp.uint32),
        compiler_params=_params(("arbitrary",)),
        name="moe_dispatch",
    )(pos, starts + cnt, zero_fill, x, mods, g4)

    a = pl.pallas_call(
        _moe_up_kernel,
        grid_spec=pltpu.PrefetchScalarGridSpec(
            num_scalar_prefetch=4, grid=(EFF // tf, n_mt),
            in_specs=[
                pl.BlockSpec((tmu, D // 2), lambda f, m, te, tix, fi, nu: (tix[m], 0)),
                pl.BlockSpec((None, None, D, tf), lambda f, m, te, tix, fi, nu: (i, te[m], 0, f)),
                pl.BlockSpec((None, None, D, tf), lambda f, m, te, tix, fi, nu: (i, te[m], 0, f)),
            ],
            out_specs=pl.BlockSpec((tmu, tf), lambda f, m, te, tix, fi, nu: (tix[m], f)),
            scratch_shapes=[pltpu.VMEM((D, tf), BF16), pltpu.VMEM((D, tf), BF16)]),
        out_shape=sds((Tp, EFF), BF16),
        compiler_params=_params(("arbitrary", "arbitrary")),
        name="moe_up",
    )(te_u, tix_u, first_u, nrows_u, hs, w_e_gate, w_e_up)

    ys = pl.pallas_call(
        _moe_down_kernel,
        grid_spec=pltpu.PrefetchScalarGridSpec(
            num_scalar_prefetch=4, grid=(D // tn, n_mt_d),
            in_specs=[
                pl.BlockSpec((tm, EFF), lambda n, m, te, tix, fi, nu: (tix[m], 0)),
                pl.BlockSpec((None, None, EFF, tn), lambda n, m, te, tix, fi, nu: (i, te[m], 0, n)),
            ],
            out_specs=pl.BlockSpec((tm, tn), lambda n, m, te, tix, fi, nu: (tix[m], n)),
            scratch_shapes=[pltpu.VMEM((EFF, tn), BF16)]),
        out_shape=sds((Tp, D), F32),
        compiler_params=_params(("arbitrary", "arbitrary")),
        name="moe_down",
    )(te_d, tix_d, first_d, nrows_d, a, w_e_down)

    if n_split is None:
        out_specs = pl.BlockSpec((tm, D), lambda m, pos: (m, 0))
        out_shape = sds((T, D), F32)
    else:
        out_specs = [pl.BlockSpec((tm, D), lambda m, pos: (jnp.minimum(m, n_split - 1), 0)),
                     pl.BlockSpec((tm, D), lambda m, pos: (jnp.maximum(m - n_split, 0), 0))]
        out_shape = [sds((n_split * tm, D), F32), sds((T - n_split * tm, D), F32)]
    return pl.pallas_call(
        functools.partial(_combine_kernel, n_split=n_split),
        grid_spec=pltpu.PrefetchScalarGridSpec(
            num_scalar_prefetch=1, grid=(n_tiles,),
            in_specs=[
                pl.BlockSpec(memory_space=pl.ANY),
                pl.BlockSpec((tm, LANES), lambda m, pos: (m, 0)),
                pl.BlockSpec((tm, D), lambda m, pos: (m, 0)),
                pl.BlockSpec((None, None, N_MOD, D), lambda m, pos: (l, mrow(m), 0, 0)),
                pl.BlockSpec((None, 1, D), lambda m, pos: (4 * l + 3, 0, 0)),
            ],
            out_specs=out_specs,
            scratch_shapes=[pltpu.VMEM((2, 2 * tm, D), F32), pltpu.SemaphoreType.DMA((2,))]),
        out_shape=out_shape,
        compiler_params=_params(("arbitrary",)),
        name="moe_combine",
    )(pos, ys, rp, x, mods, g4)


def kernel(x_prompt, x_sample, cache_ckv, cache_krope, c, c_ctx, w_in, g_q, g_kv, w_uq, w_uk, w_uv,
           w_conv, w_o, w_ada, b_ada, g_sand, w_ff_gate, w_ff_up, w_ff_down, w_router,
           w_e_gate, w_e_up, w_e_down):
    B, N, D = x_prompt.shape
    DB, DN, _ = x_sample.shape
    L = w_in.shape[0]
    QL, KV = g_q.shape[1], g_kv.shape[1]
    H, NOPE = w_uk.shape[2], w_uk.shape[3]
    ROPE = w_uq.shape[3] - NOPE
    VD = w_uv.shape[3]
    CONV = w_conv.shape[2]
    AW = H * VD
    PAST = cache_ckv.shape[2]
    FF = w_ff_gate.shape[2]
    NE, EFF = w_e_gate.shape[1], w_e_gate.shape[3]
    T_P, T_S = B * N, DB * DN
    T = T_P + T_S
    DK = KV + LANES
    NA = 1024
    tm = TM
    assert ROPE == 64 and 2 * ROPE == LANES and NOPE % LANES == 0 and KV % LANES == 0 and QL % LANES == 0
    assert QL + KV + LANES <= NA and CONV % LANES == 0 and AW + CONV == w_o.shape[1]
    assert T_P % tm == 0 and DN % tm == 0 and tm % N == 0 and T_P % DN == 0
    assert N & (N - 1) == 0 and DN & (DN - 1) == 0 and DN % GRID_W == 0
    assert NE <= LANES and DB + 1 <= SUBLANES
    n_p_tiles = T_P // tm
    tiles_per_s = DN // tm
    n_tiles = T // tm
    scale = float(NOPE + ROPE) ** -0.5
    tf = 512
    assert FF % tf == 0 and EFF % tf == 0

    def mrow(m):
        return jnp.where(m < n_p_tiles, 0, 1 + (m - n_p_tiles) // tiles_per_s)

    def trow(m):
        return jnp.where(m < n_p_tiles, 0, 1 + (m - n_p_tiles) % tiles_per_s)

    t = jnp.arange(DN)
    inv = 1.0 / (ROPE_THETA ** (jnp.arange(0, ROPE // 2, 2, dtype=F32) / (ROPE // 2)))
    ar = (t // GRID_W).astype(F32)[:, None] * inv
    ac = (t % GRID_W).astype(F32)[:, None] * inv
    ang = jnp.concatenate([ar, ar, ac, ac] * 2, axis=-1)
    first = (jnp.arange(LANES) % (ROPE // 2)) < (ROPE // 4)
    cos_t = jnp.concatenate([jnp.ones((tm, LANES), F32), jnp.cos(ang)], axis=0)
    sin = jnp.sin(ang)
    sa_t = jnp.concatenate([jnp.zeros((tm, LANES), F32), jnp.where(first, -sin, 0.0)], axis=0)
    sb_t = jnp.concatenate([jnp.zeros((tm, LANES), F32), jnp.where(first, 0.0, sin)], axis=0)

    s1, s2 = QL + KV + ROPE, QL + KV + ROPE + 3 * CONV
    w_in_b = w_in.astype(BF16)
    w_a = jnp.pad(w_in_b[:, :, :s1], ((0, 0), (0, 0), (0, NA - s1)))
    w_b = w_in_b[:, :, s1:s2]
    wq = jnp.pad(w_uq, ((0, 0), (0, 0), (0, 0), (0, LANES - ROPE))).astype(BF16)
    wq = wq.reshape(L, QL, H * (NOPE + LANES))
    wk = w_uk.transpose(0, 2, 3, 1).astype(BF16)
    wuv = w_uv.transpose(0, 2, 1, 3).astype(BF16)
    wo = w_o.astype(BF16)
    wfg, wfu, wfd = w_ff_gate.astype(BF16), w_ff_up.astype(BF16), w_ff_down.astype(BF16)
    wr = jnp.pad(w_router, ((0, 0), (0, 0), (0, LANES - NE)))
    kcache = jnp.concatenate(
        [cache_ckv, cache_krope, jnp.zeros(cache_krope.shape[:-1] + (LANES - ROPE,), F32)],
        axis=-1).astype(BF16)

    cond8 = jnp.concatenate([c_ctx[None, :], c, jnp.zeros((SUBLANES - 1 - DB, D), F32)], axis=0)
    mods = _ada(cond8, w_ada, b_ada)[:, :1 + DB].reshape(L, 1 + DB, N_MOD, D)

    xs = (x_prompt.reshape(T_P, D), x_sample.reshape(T_S, D))
    p_idx = lambda m: jnp.minimum(m, n_p_tiles - 1)
    s_idx = lambda m: jnp.maximum(m - n_p_tiles, 0)

    mod_spec2 = lambda l: pl.BlockSpec((None, None, N_MOD, D), lambda m, n: (l, mrow(m), 0, 0))
    row_spec2 = lambda w: pl.BlockSpec((tm, w), lambda m, n: (m, 0))
    tab_spec2 = pl.BlockSpec((tm, LANES), lambda m, n: (trow(m), 0))
    tab_spec1 = pl.BlockSpec((tm, LANES), lambda m: (trow(m), 0))

    assert ROW_CHUNK % N == 0
    caches = ()
    for l in range(L):
        if l > 0:
            xs = (x,)
        cqn, new_ckv, new_krope, kcat, gates = pl.pallas_call(
            functools.partial(_inproj_kernel, QL=QL, KV=KV, ROPE=ROPE, n_x=len(xs), n_p_tiles=n_p_tiles,
                              n_alias=len(caches)),
            grid=(n_tiles, 1 + 3),
            in_specs=([row_spec2(D)] if len(xs) == 1 else
                      [pl.BlockSpec((tm, D), lambda m, n: (p_idx(m), 0)),
                       pl.BlockSpec((tm, D), lambda m, n: (s_idx(m), 0))]) + [
                pl.BlockSpec(memory_space=pl.ANY) for _ in caches] + [
                mod_spec2(l),
                pl.BlockSpec((None, 1, D), lambda m, n: (4 * l, 0, 0)),
                pl.BlockSpec((None, 1, QL), lambda m, n: (l, 0, 0)),
                pl.BlockSpec((None, 1, KV), lambda m, n: (l, 0, 0)),
                tab_spec2, tab_spec2, tab_spec2,
                pl.BlockSpec((None, D, NA), lambda m, n: (l, 0, 0)),
                pl.BlockSpec((None, D, CONV), lambda m, n: (l, 0, jnp.maximum(n - 1, 0))),
            ],
            out_specs=[
                row_spec2(QL),
                pl.BlockSpec((tm // N, None, N, KV), lambda m, n: (p_idx(m), l, 0, 0)),
                pl.BlockSpec((tm // N, None, N, ROPE), lambda m, n: (p_idx(m), l, 0, 0)),
                row_spec2(DK),
                pl.BlockSpec((tm, CONV), lambda m, n: (m, jnp.maximum(n - 1, 0))),
            ],
            out_shape=[
                jax.ShapeDtypeStruct((T, QL), BF16),
                jax.ShapeDtypeStruct((B, L, N, KV), F32),
                jax.ShapeDtypeStruct((B, L, N, ROPE), F32),
                jax.ShapeDtypeStruct((T, DK), BF16),
                jax.ShapeDtypeStruct((T, 3 * CONV), BF16),
            ],
            scratch_shapes=[pltpu.VMEM((tm, D), BF16)],
            input_output_aliases={len(xs) + j: 1 + j for j in range(len(caches))},
            compiler_params=_params(("arbitrary", "arbitrary")),
            name="in_proj",
        )(*xs, *caches, mods, g_sand.reshape(4 * L, 1, D), g_q.reshape(L, 1, QL), g_kv.reshape(L, 1, KV),
          cos_t, sa_t, sb_t, w_a, w_b)
        caches = (new_ckv, new_krope)

        qcat = pl.pallas_call(
            functools.partial(_q_kernel, H=H, NOPE=NOPE, KV=KV, scale=scale),
            grid=(n_tiles,),
            in_specs=[
                pl.BlockSpec((tm, QL), lambda m: (m, 0)),
                pl.BlockSpec((None, QL, H * (NOPE + LANES)), lambda m: (l, 0, 0)),
                pl.BlockSpec((None, H, NOPE, KV), lambda m: (l, 0, 0, 0)),
                tab_spec1, tab_spec1, tab_spec1,
            ],
            out_specs=pl.BlockSpec((tm, H * DK), lambda m: (m, 0)),
            out_shape=jax.ShapeDtypeStruct((T, H * DK), BF16),
            compiler_params=_params(("arbitrary",)),
            name="q_proj",
        )(cqn, wq, wk, cos_t, sa_t, sb_t)

        o_p = pl.pallas_call(
            functools.partial(_attn_kernel, H=H, KV=KV, has_cache=False),
            grid=(B,),
            in_specs=[
                pl.BlockSpec((N, H * DK), lambda b: (b, 0)),
                pl.BlockSpec((N, DK), lambda b: (b, 0)),
            ],
            out_specs=pl.BlockSpec((N, H * KV), lambda b: (b, 0)),
            out_shape=jax.ShapeDtypeStruct((T_P, H * KV), BF16),
            compiler_params=_params(("arbitrary",)),
            name="attn_prompt",
        )(qcat, kcat)
        tq = 256
        qs_per = DN // tq
        o_s = pl.pallas_call(
            functools.partial(_attn_kernel, H=H, KV=KV, has_cache=True),
            grid=(DB, qs_per),
            in_specs=[
                pl.BlockSpec((tq, H * DK), lambda b, i: (T_P // tq + b * qs_per + i, 0)),
                pl.BlockSpec((DN, DK), lambda b, i: (T_P // DN + b, 0)),
                pl.BlockSpec((None, None, PAST, DK), lambda b, i: (b, l, 0, 0)),
            ],
            out_specs=pl.BlockSpec((tq, H * KV), lambda b, i: (b * qs_per + i, 0)),
            out_shape=jax.ShapeDtypeStruct((T_S, H * KV), BF16),
            compiler_params=_params(("arbitrary", "arbitrary")),
            name="attn_sample",
        )(qcat, kcat, kcache)

        r8 = tm // HALO_ROWS
        prev8 = lambda m: jnp.maximum(m * r8 - 1, 0)
        next8 = lambda m: jnp.minimum((m + 1) * r8, T // HALO_ROWS - 1)
        x = pl.pallas_call(
            functools.partial(_mix_kernel, H=H, KV=KV, VD=VD, n_p_tiles=n_p_tiles,
                              seq_p=N, seq_s=DN, t_p=T_P, n_x=len(xs)),
            grid=(n_tiles,),
            in_specs=([pl.BlockSpec((tm, D), lambda m: (m, 0))] if len(xs) == 1 else
                      [pl.BlockSpec((tm, D), lambda m: (p_idx(m), 0)),
                       pl.BlockSpec((tm, D), lambda m: (s_idx(m), 0))]) + [
                pl.BlockSpec((tm, H * KV), lambda m: (p_idx(m), 0)),
                pl.BlockSpec((tm, H * KV), lambda m: (s_idx(m), 0)),
                pl.BlockSpec((tm, CONV), lambda m: (m, 0)),
                pl.BlockSpec((tm, CONV), lambda m: (m, 1)),
                pl.BlockSpec((tm, CONV), lambda m: (m, 2)),
                pl.BlockSpec((HALO_ROWS, CONV), lambda m: (prev8(m), 1)),
                pl.BlockSpec((HALO_ROWS, CONV), lambda m: (prev8(m), 2)),
                pl.BlockSpec((HALO_ROWS, CONV), lambda m: (next8(m), 1)),
                pl.BlockSpec((HALO_ROWS, CONV), lambda m: (next8(m), 2)),
                pl.BlockSpec((None, H, KV, VD), lambda m: (l, 0, 0, 0)),
                pl.BlockSpec((None, 3, CONV), lambda m: (l, 0, 0)),
                pl.BlockSpec((None, AW + CONV, D), lambda m: (l, 0, 0)),
                pl.BlockSpec((None, 1, D), lambda m: (4 * l + 1, 0, 0)),
                pl.BlockSpec((None, None, N_MOD, D), lambda m: (l, mrow(m), 0, 0)),
            ],
            out_specs=pl.BlockSpec((tm, D), lambda m: (m, 0)),
            out_shape=jax.ShapeDtypeStruct((T, D), F32),
            scratch_shapes=[pltpu.VMEM((tm, AW + CONV), BF16),
                            pltpu.VMEM((tm + 2 * SUBLANES, CONV), F32)],
            compiler_params=_params(("arbitrary",)),
            name="mixer_out",
        )(*xs, o_p, o_s, gates, gates, gates, gates, gates, gates, gates, wuv, w_conv, wo,
          g_sand.reshape(4 * L, 1, D), mods)

        i = l // 2
        if l % 2 == 0:
            tmf = TM_FFN if (T_P % TM_FFN == 0 and DN % TM_FFN == 0) else tm
            mrow_f = lambda m: jnp.where(m < T_P // tmf, 0, 1 + (m - T_P // tmf) // (DN // tmf))
            x = pl.pallas_call(
                _ffn_kernel,
                grid=(T // tmf, FF // tf),
                in_specs=[
                    pl.BlockSpec((tmf, D), lambda m, f: (m, 0), pipeline_mode=pl.Buffered(1)),
                    pl.BlockSpec((None, None, N_MOD, D), lambda m, f: (l, mrow_f(m), 0, 0)),
                    pl.BlockSpec((None, 1, D), lambda m, f: (4 * l + 2, 0, 0)),
                    pl.BlockSpec((None, 1, D), lambda m, f: (4 * l + 3, 0, 0)),
                    pl.BlockSpec((None, D, tf), lambda m, f: (i, 0, f)),
                    pl.BlockSpec((None, D, tf), lambda m, f: (i, 0, f)),
                    pl.BlockSpec((None, tf, D), lambda m, f: (i, f, 0)),
                ],
                out_specs=pl.BlockSpec((tmf, D), lambda m, f: (m, 0)),
                out_shape=jax.ShapeDtypeStruct((T, D), F32),
                scratch_shapes=[pltpu.VMEM((tmf, D), BF16)],
                compiler_params=_params(("arbitrary", "arbitrary")),
                name="ffn_dense",
            )(x, mods, g_sand.reshape(4 * L, 1, D), g_sand.reshape(4 * L, 1, D), wfg, wfu, wfd)
        else:
            x = _moe_layer(x, mods, g_sand.reshape(4 * L, 1, D), wr, w_e_gate, w_e_up, w_e_down,
                           l=l, i=i, tm=tm, tf=tf, mrow=mrow,
                           n_split=n_p_tiles if l == L - 1 else None)

    x_p, x_s = x if isinstance(x, (list, tuple)) else (x[:T_P], x[T_P:])
    y_prompt = x_p.reshape(B, N, D)
    y_sample = x_s.reshape(DB, DN, D)
    return (y_prompt, y_sample, new_ckv, new_krope)
```

```python
import functools

import jax
import jax.numpy as jnp
from jax import lax
from jax.experimental import pallas as pl
from jax.experimental.pallas import tpu as pltpu

GRID_W = 64
ROPE_THETA = 10000.0
EPS = 1e-6
N_MOD = 6
TOP_K = 2

LANES = 128
SUBLANES = 8
TM = 512
TM_FFN = 1024
ROW_CHUNK = 256
HALO_ROWS = 16
MOE_SUB = 256
VMEM_LIMIT = 56 * 1024 * 1024

BF16 = jnp.bfloat16
F32 = jnp.float32


def _params(sem, vmem=VMEM_LIMIT):
    return pltpu.CompilerParams(dimension_semantics=sem, vmem_limit_bytes=vmem)


def _rms(x):
    return x * lax.rsqrt(jnp.mean(x * x, axis=-1, keepdims=True) + EPS)


def _rope(x, cos, sa, sb):
    return x * cos + pltpu.roll(x, LANES - 16, 1) * sa + pltpu.roll(x, 16, 1) * sb


def _ada_kernel(c_ref, w_ref, b_ref, o_ref):
    c = c_ref[...]
    s = c / (1.0 + jnp.exp(-c))
    o_ref[...] = jnp.dot(s.astype(BF16), w_ref[...].astype(BF16),
                         preferred_element_type=F32) + b_ref[...]


def _ada(cond8, w_ada, b_ada):
    L, D, N6 = w_ada.shape
    tn = 1024 if N6 % 1024 == 0 else N6
    return pl.pallas_call(
        _ada_kernel,
        grid=(L, N6 // tn),
        in_specs=[
            pl.BlockSpec((SUBLANES, D), lambda l, n: (0, 0)),
            pl.BlockSpec((None, D, tn), lambda l, n: (l, 0, n)),
            pl.BlockSpec((None, 1, tn), lambda l, n: (l, 0, n)),
        ],
        out_specs=pl.BlockSpec((None, SUBLANES, tn), lambda l, n: (l, 0, n)),
        out_shape=jax.ShapeDtypeStruct((L, SUBLANES, N6), F32),
        compiler_params=_params(("arbitrary", "arbitrary")),
        name="ada_mod",
    )(cond8, w_ada, b_ada.reshape(L, 1, N6))


def _tile_rows(refs, is_first, rows):
    if len(refs) == 1:
        return refs[0][rows, :]
    return jnp.where(is_first, refs[0][rows, :], refs[1][rows, :])


def _inproj_kernel(*refs, QL, KV, ROPE, n_x, n_p_tiles, n_alias):
    x_refs, refs = refs[:n_x], refs[n_x + n_alias:]
    (mod_ref, g_ref, gq_ref, gkv_ref, cos_ref, sa_ref, sb_ref, wa_ref, wb_ref,
     cqn_ref, ckv_ref, kr_ref, kcat_ref, gates_ref, hb_ref) = refs
    n = pl.program_id(1)
    is_p = pl.program_id(0) < n_p_tiles
    tm = hb_ref.shape[0]
    seq = ckv_ref.shape[1]

    def first_step(store_cache):
        for r0 in range(0, tm, ROW_CHUNK):
            rows = slice(r0, r0 + ROW_CHUNK)
            h = _rms(_tile_rows(x_refs, is_p, rows)) * g_ref[...]
            h = h * (1.0 + mod_ref[1:2, :]) + mod_ref[0:1, :]
            hb_ref[rows, :] = h.astype(BF16)
            acc = jnp.dot(hb_ref[rows, :], wa_ref[...], preferred_element_type=F32)
            cqn_ref[rows, :] = (_rms(acc[:, :QL]) * gq_ref[...]).astype(BF16)
            ckv = _rms(acc[:, QL:QL + KV]) * gkv_ref[...]
            kr = _rope(acc[:, QL + KV:QL + KV + LANES], cos_ref[rows, :], sa_ref[rows, :], sb_ref[rows, :])
            if store_cache:
                for j in range(ROW_CHUNK // seq):
                    ckv_ref[r0 // seq + j] = ckv[j * seq:(j + 1) * seq, :]
                    kr_ref[r0 // seq + j] = kr[j * seq:(j + 1) * seq, :ROPE]
            kcat_ref[rows, :KV] = ckv.astype(BF16)
            kcat_ref[rows, KV:] = kr.astype(BF16)

    pl.when(jnp.logical_and(n == 0, is_p))(functools.partial(first_step, True))
    pl.when(jnp.logical_and(n == 0, jnp.logical_not(is_p)))(functools.partial(first_step, False))

    @pl.when(n > 0)
    def _():
        gates_ref[...] = jnp.dot(hb_ref[...], wb_ref[...], preferred_element_type=F32).astype(BF16)


def _q_kernel(cqn_ref, wq_ref, wk_ref, cos_ref, sa_ref, sb_ref, o_ref, *, H, NOPE, KV, scale):
    QW = NOPE + LANES
    DK = KV + LANES
    cqn = cqn_ref[...]
    for h in range(H):
        q = jnp.dot(cqn, wq_ref[:, h * QW:(h + 1) * QW], preferred_element_type=F32)
        q_lat = jnp.dot(q[:, :NOPE].astype(BF16), wk_ref[h], preferred_element_type=F32)
        q_rope = _rope(q[:, NOPE:], cos_ref[...], sa_ref[...], sb_ref[...])
        o_ref[:, h * DK:h * DK + KV] = (q_lat * scale).astype(BF16)
        o_ref[:, h * DK + KV:(h + 1) * DK] = (q_rope * scale).astype(BF16)


def _attn_kernel(*refs, H, KV, has_cache):
    if has_cache:
        q_ref, kn_ref, kc_ref, o_ref = refs
    else:
        q_ref, kn_ref, o_ref = refs
    DK = KV + LANES
    dn = (((1,), (1,)), ((), ()))
    kn = kn_ref[...]
    vn = kn[:, :KV]
    if has_cache:
        kc = kc_ref[...]
        vc = kc[:, :KV]
    for h in range(H):
        q = q_ref[:, h * DK:(h + 1) * DK]
        s = lax.dot_general(q, kn, dn, preferred_element_type=F32)
        m = jnp.max(s, axis=-1, keepdims=True)
        if has_cache:
            sc = lax.dot_general(q, kc, dn, preferred_element_type=F32)
            m = jnp.maximum(m, jnp.max(sc, axis=-1, keepdims=True))
        p = jnp.exp(s - m)
        l = jnp.sum(p, axis=-1, keepdims=True)
        o = jnp.dot(p.astype(BF16), vn, preferred_element_type=F32)
        if has_cache:
            pc = jnp.exp(sc - m)
            l = l + jnp.sum(pc, axis=-1, keepdims=True)
            o = o + jnp.dot(pc.astype(BF16), vc, preferred_element_type=F32)
        o_ref[:, h * KV:(h + 1) * KV] = (o / l).astype(BF16)


def _mix_kernel(*refs, H, KV, VD, n_p_tiles, seq_p, seq_s, t_p, n_x):
    x_refs, refs = refs[:n_x], refs[n_x:]
    (op_ref, os_ref, gb_ref, gc_ref, u_ref, gcp_ref, up_ref, gcn_ref, un_ref,
     wuv_ref, wc_ref, wo_ref, g_ref, mod_ref, out_ref, cat_ref, vs_ref) = refs
    m = pl.program_id(0)
    tm = out_ref.shape[0]
    AW = H * VD
    is_p = m < n_p_tiles
    seq = jnp.where(is_p, seq_p, seq_s)
    row0 = m * tm - jnp.where(is_p, 0, t_p)

    hr = HALO_ROWS
    vs_ref[SUBLANES:SUBLANES + tm, :] = gc_ref[...].astype(F32) * u_ref[...].astype(F32)
    vs_ref[SUBLANES - 1:SUBLANES, :] = gcp_ref[hr - 1:hr, :].astype(F32) * up_ref[hr - 1:hr, :].astype(F32)
    vs_ref[SUBLANES + tm:SUBLANES + tm + 1, :] = gcn_ref[0:1, :].astype(F32) * un_ref[0:1, :].astype(F32)

    for r0 in range(0, tm, ROW_CHUNK):
        rows = slice(r0, r0 + ROW_CHUNK)
        for h in range(H):
            cols = slice(h * KV, (h + 1) * KV)
            o = jnp.where(is_p, op_ref[rows, cols], os_ref[rows, cols])
            a = jnp.dot(o, wuv_ref[h], preferred_element_type=F32)
            cat_ref[rows, h * VD:(h + 1) * VD] = a.astype(BF16)
        pos = jnp.bitwise_and(lax.broadcasted_iota(jnp.int32, (ROW_CHUNK, 1), 0) + (row0 + r0), seq - 1)
        v = vs_ref[SUBLANES + r0:SUBLANES + r0 + ROW_CHUNK, :]
        vp = jnp.where(pos == 0, 0.0, vs_ref[SUBLANES - 1 + r0:SUBLANES - 1 + r0 + ROW_CHUNK, :])
        vn = jnp.where(pos == seq - 1, 0.0, vs_ref[SUBLANES + 1 + r0:SUBLANES + 1 + r0 + ROW_CHUNK, :])
        conv = vp * wc_ref[0:1, :] + v * wc_ref[1:2, :] + vn * wc_ref[2:3, :]
        cat_ref[rows, AW:] = (gb_ref[rows, :].astype(F32) * conv).astype(BF16)
        y = jnp.dot(cat_ref[rows, :], wo_ref[...], preferred_element_type=F32)
        out_ref[rows, :] = _tile_rows(x_refs, is_p, rows) + mod_ref[2:3, :] * (_rms(y) * g_ref[...])


def _modulate_ffn(x_ref, mod_ref, g_ref):
    h = _rms(x_ref[...]) * g_ref[...]
    return h * (1.0 + mod_ref[4:5, :]) + mod_ref[3:4, :]


def _swiglu_step(hb, wg_ref, wu_ref, wd_ref):
    g = jnp.dot(hb, wg_ref[...], preferred_element_type=F32)
    u = jnp.dot(hb, wu_ref[...], preferred_element_type=F32)
    a = (g / (1.0 + jnp.exp(-g))) * u
    return jnp.dot(a.astype(BF16), wd_ref[...], preferred_element_type=F32)


def _for_row_chunks(n_rows, chunk, fn):
    def body(c, carry):
        fn(pl.ds(pl.multiple_of(c * chunk, chunk), chunk))
        return carry
    lax.fori_loop(0, n_rows // chunk, body, 0)


def _ffn_kernel(x_ref, mod_ref, g2_ref, g3_ref, wg_ref, wu_ref, wd_ref, out_ref, hb_ref):
    f = pl.program_id(1)
    nf = pl.num_programs(1)
    tm = x_ref.shape[0]

    @pl.when(f == 0)
    def _():
        def prologue(rows):
            h = _rms(x_ref[rows, :]) * g2_ref[...]
            hb_ref[rows, :] = (h * (1.0 + mod_ref[4:5, :]) + mod_ref[3:4, :]).astype(BF16)
            out_ref[rows, :] = jnp.zeros((ROW_CHUNK, out_ref.shape[1]), F32)
        _for_row_chunks(tm, ROW_CHUNK, prologue)

    out_ref[...] += _swiglu_step(hb_ref[...], wg_ref, wu_ref, wd_ref)

    @pl.when(f == nf - 1)
    def _():
        def epilogue(rows):
            y = _rms(out_ref[rows, :]) * g3_ref[...]
            out_ref[rows, :] = x_ref[rows, :] + mod_ref[5:6, :] * y
        _for_row_chunks(tm, ROW_CHUNK, epilogue)


def _top2(logits, n_experts):
    lane = lax.broadcasted_iota(jnp.int32, logits.shape, 1)
    neg = jnp.float32(-jnp.inf)
    l1 = jnp.where(lane < n_experts, logits, neg)
    m1 = jnp.max(l1, axis=-1, keepdims=True)
    i1 = jnp.min(jnp.where(l1 == m1, lane, LANES), axis=-1, keepdims=True)
    l2 = jnp.where(lane == i1, neg, l1)
    m2 = jnp.max(l2, axis=-1, keepdims=True)
    i2 = jnp.min(jnp.where(l2 == m2, lane, LANES), axis=-1, keepdims=True)
    e2 = jnp.exp(m2 - m1)
    return lane, i1, i2, 1.0 / (1.0 + e2), e2 / (1.0 + e2)


def _route_kernel(x_ref, mod_ref, g2_ref, wr_ref, ri_ref, rp_ref, cnt_ref, carry_ref,
                  *, n_experts):
    m = pl.program_id(0)
    tm = x_ref.shape[0]

    @pl.when(m == 0)
    def _():
        carry_ref[...] = jnp.zeros_like(carry_ref)

    h = _modulate_ffn(x_ref, mod_ref, g2_ref)
    h_hi = h.astype(BF16)
    h_lo = (h - h_hi.astype(F32)).astype(BF16)
    w = wr_ref[...]
    w_hi = w.astype(BF16)
    w_lo = (w - w_hi.astype(F32)).astype(BF16)
    logits = (jnp.dot(h_hi, w_hi, preferred_element_type=F32)
              + jnp.dot(h_hi, w_lo, preferred_element_type=F32)
              + jnp.dot(h_lo, w_hi, preferred_element_type=F32))
    lane, i1, i2, p1, p2 = _top2(logits, n_experts)
    sel = jnp.logical_or(lane == i1, lane == i2)
    r = lax.broadcasted_iota(jnp.int32, (tm, tm), 0)
    c = lax.broadcasted_iota(jnp.int32, (tm, tm), 1)
    before = jnp.where(r > c, 1.0, 0.0).astype(BF16)
    rank = jnp.dot(before, jnp.where(sel, 1.0, 0.0).astype(BF16), preferred_element_type=F32)
    rank = rank + carry_ref[0:1, :]
    r1 = jnp.sum(jnp.where(lane == i1, rank, 0.0), axis=-1, keepdims=True).astype(jnp.int32)
    r2 = jnp.sum(jnp.where(lane == i2, rank, 0.0), axis=-1, keepdims=True).astype(jnp.int32)
    carry_ref[0:1, :] = carry_ref[0:1, :] + jnp.sum(jnp.where(sel, 1.0, 0.0), axis=0, keepdims=True)
    ri_ref[...] = jnp.where(lane == 0, i1, jnp.where(lane == 1, i2,
                            jnp.where(lane == 2, r1, jnp.where(lane == 3, r2, 0))))
    rp_ref[...] = jnp.where(lane == 0, p1, jnp.where(lane == 1, p2, 0.0))
    cnt_ref[...] = carry_ref[...]


def _row_copy(src_ref, src_row, dst_ref, dst_row, sem):
    return pltpu.make_async_copy(src_ref.at[pl.ds(src_row, 1), :], dst_ref.at[pl.ds(dst_row, 1), :], sem)


def _rows_wait(any_ref, n_rows, sem):
    pltpu.make_async_copy(any_ref.at[pl.ds(0, n_rows), :], any_ref.at[pl.ds(0, n_rows), :], sem).wait()


def _dispatch_kernel(pos_ref, pad_start_ref, pad_cnt_ref, x_ref, mod_ref, g2_ref, hs_ref,
                     pk_ref, zero_ref, sem, pad_sem, *, n_experts):
    m = pl.program_id(0)
    n = pl.num_programs(0)
    tm, half = x_ref.shape[0], x_ref.shape[1] // 2
    slot = m % 2

    @pl.when(m >= 2)
    def _():
        _rows_wait(hs_ref, 2 * tm, sem.at[slot])

    def pack(rows):
        h = _rms(x_ref[rows, :]) * g2_ref[...]
        h = (h * (1.0 + mod_ref[4:5, :]) + mod_ref[3:4, :]).astype(BF16).astype(F32)
        bits = lax.bitcast_convert_type(h, jnp.uint32)
        pk_ref[slot, rows, :] = jnp.bitwise_or(bits[:, half:], jnp.right_shift(bits[:, :half], 16))
    _for_row_chunks(tm, ROW_CHUNK, pack)

    def issue(r, carry):
        t = m * tm + r
        _row_copy(pk_ref.at[slot], r, hs_ref, pos_ref[2 * t], sem.at[slot]).start()
        _row_copy(pk_ref.at[slot], r, hs_ref, pos_ref[2 * t + 1], sem.at[slot]).start()
        return carry

    lax.fori_loop(0, tm, issue, 0, unroll=8)

    @pl.when(m == 0)
    def _():
        zero_ref[...] = jnp.zeros_like(zero_ref)
        for e in range(n_experts):
            start, cnt = pad_start_ref[e], pad_cnt_ref[e]

            def pad_issue(i, carry):
                _row_copy(zero_ref, 0, hs_ref, start + i, pad_sem).start()
                return carry

            def pad_wait(i, carry):
                _row_copy(zero_ref, 0, hs_ref, 0, pad_sem).wait()
                return carry

            lax.fori_loop(0, cnt, pad_issue, 0)
            lax.fori_loop(0, cnt, pad_wait, 0)

    @pl.when(m == n - 1)
    def _():
        _rows_wait(hs_ref, 2 * tm, sem.at[slot])

        @pl.when(n >= 2)
        def _():
            _rows_wait(hs_ref, 2 * tm, sem.at[1 - slot])


def _for_live_rows(nrows, tm, fn):
    for n in range(MOE_SUB, tm + 1, MOE_SUB):
        pl.when(nrows == n)(functools.partial(fn, n))


def _moe_up_kernel(te_ref, tix_ref, first_ref, nrows_ref, hs_ref, wg_ref, wu_ref, a_ref,
                   wgb_ref, wub_ref):
    m = pl.program_id(1)

    @pl.when(first_ref[m] == 1)
    def _():
        wgb_ref[...] = wg_ref[...].astype(BF16)
        wub_ref[...] = wu_ref[...].astype(BF16)

    half = hs_ref.shape[1]

    def up(n):
        w = hs_ref[0:n, :]
        lo = lax.bitcast_convert_type(jnp.left_shift(w, 16), F32).astype(BF16)
        hi = lax.bitcast_convert_type(jnp.bitwise_and(w, jnp.uint32(0xFFFF0000)), F32).astype(BF16)
        g = (jnp.dot(lo, wgb_ref[0:half, :], preferred_element_type=F32)
             + jnp.dot(hi, wgb_ref[half:, :], preferred_element_type=F32))
        u = (jnp.dot(lo, wub_ref[0:half, :], preferred_element_type=F32)
             + jnp.dot(hi, wub_ref[half:, :], preferred_element_type=F32))
        a_ref[0:n, :] = ((g / (1.0 + jnp.exp(-g))) * u).astype(BF16)
    _for_live_rows(nrows_ref[m], hs_ref.shape[0], up)


def _moe_down_kernel(te_ref, tix_ref, first_ref, nrows_ref, a_ref, wd_ref, ys_ref, wdb_ref):
    m = pl.program_id(1)

    @pl.when(first_ref[m] == 1)
    def _():
        wdb_ref[...] = wd_ref[...].astype(BF16)

    def down(n):
        ys_ref[0:n, :] = jnp.dot(a_ref[0:n, :], wdb_ref[...], preferred_element_type=F32)
    _for_live_rows(nrows_ref[m], a_ref.shape[0], down)


def _combine_kernel(pos_ref, ys_ref, rp_ref, x_ref, mod_ref, g3_ref, *rest, n_split):
    if n_split is None:
        out_ref, buf_ref, sem = rest
    else:
        out_a_ref, out_b_ref, buf_ref, sem = rest
    m = pl.program_id(0)
    n = pl.num_programs(0)
    tm = x_ref.shape[0]

    def issue_tile(tile, slot):
        def issue(r, carry):
            t = tile * tm + r
            _row_copy(ys_ref, pos_ref[2 * t], buf_ref.at[slot], r, sem.at[slot]).start()
            _row_copy(ys_ref, pos_ref[2 * t + 1], buf_ref.at[slot], tm + r, sem.at[slot]).start()
            return carry
        lax.fori_loop(0, tm, issue, 0, unroll=8)

    @pl.when(m == 0)
    def _():
        issue_tile(0, 0)

    @pl.when(m + 1 < n)
    def _():
        issue_tile(m + 1, (m + 1) % 2)

    slot = m % 2

    _rows_wait(ys_ref, 2 * tm, sem.at[slot])

    def result(rows, second):
        y = rp_ref[rows, 0:1] * buf_ref[slot, rows, :] + rp_ref[rows, 1:2] * buf_ref[slot, second, :]
        return x_ref[rows, :] + mod_ref[5:6, :] * (_rms(y) * g3_ref[...])

    def chunked(dst_ref):
        def body(c, carry):
            r0 = pl.multiple_of(c * ROW_CHUNK, ROW_CHUNK)
            dst_ref[pl.ds(r0, ROW_CHUNK), :] = result(pl.ds(r0, ROW_CHUNK), pl.ds(tm + r0, ROW_CHUNK))
            return carry
        lax.fori_loop(0, tm // ROW_CHUNK, body, 0)

    if n_split is None:
        chunked(out_ref)
    else:
        @pl.when(m < n_split)
        def _():
            chunked(out_a_ref)

        @pl.when(m >= n_split)
        def _():
            chunked(out_b_ref)


def _moe_layer(x, mods, g4, wr, w_e_gate, w_e_up, w_e_down, *, l, i, tm, tf, mrow, n_split=None):
    T, D = x.shape
    NE, EFF = w_e_gate.shape[1], w_e_gate.shape[3]
    n_tiles = T // tm
    tmu = 2 * tm
    n_mt = (TOP_K * T) // tmu + NE
    Tp = n_mt * tmu
    tn = 512
    assert D % tn == 0 and EFF % tf == 0 and (TOP_K * T) % tmu == 0
    sds = jax.ShapeDtypeStruct

    ri, rp, cnt = pl.pallas_call(
        functools.partial(_route_kernel, n_experts=NE),
        grid=(n_tiles,),
        in_specs=[
            pl.BlockSpec((tm, D), lambda m: (m, 0)),
            pl.BlockSpec((None, None, N_MOD, D), lambda m: (l, mrow(m), 0, 0)),
            pl.BlockSpec((None, 1, D), lambda m: (4 * l + 2, 0, 0)),
            pl.BlockSpec((None, D, LANES), lambda m: (i, 0, 0)),
        ],
        out_specs=[
            pl.BlockSpec((tm, LANES), lambda m: (m, 0)),
            pl.BlockSpec((tm, LANES), lambda m: (m, 0)),
            pl.BlockSpec((SUBLANES, LANES), lambda m: (0, 0)),
        ],
        out_shape=[sds((T, LANES), jnp.int32), sds((T, LANES), F32), sds((SUBLANES, LANES), F32)],
        scratch_shapes=[pltpu.VMEM((SUBLANES, LANES), F32)],
        compiler_params=_params(("arbitrary",)),
        name="moe_route",
    )(x, mods, g4, wr)

    cnt = cnt[0, :NE].astype(jnp.int32)
    pc = (cnt + tmu - 1) // tmu * tmu
    starts = jnp.cumsum(pc) - pc
    pos = jnp.stack([starts[ri[:, 0]] + ri[:, 2], starts[ri[:, 1]] + ri[:, 3]], axis=1).reshape(-1)
    zero_fill = (-cnt) % MOE_SUB

    def tile_list(t, n_max):
        per = (cnt + t - 1) // t
        upto = jnp.cumsum(per)
        ar = jnp.arange(n_max, dtype=jnp.int32)
        k = jnp.minimum(ar, upto[-1] - 1)
        te = jnp.minimum(jnp.sum(k[:, None] >= upto[None, :], axis=1), NE - 1).astype(jnp.int32)
        off = k - (upto[te] - per[te])
        tix = (starts[te] // t + off).astype(jnp.int32)
        first = jnp.logical_or(te != jnp.roll(te, 1), ar == 0).astype(jnp.int32)
        live = jnp.clip(cnt[te] - off * t, 0, t)
        nrows = jnp.where(ar < upto[-1], (live + MOE_SUB - 1) // MOE_SUB * MOE_SUB, 0).astype(jnp.int32)
        return te, tix, first, nrows

    te_u, tix_u, first_u, nrows_u = tile_list(tmu, n_mt)
    n_mt_d = (TOP_K * T) // tm + NE
    te_d, tix_d, first_d, nrows_d = tile_list(tm, n_mt_d)

    hs = pl.pallas_call(
        functools.partial(_dispatch_kernel, n_experts=NE),
        grid_spec=pltpu.PrefetchScalarGridSpec(
            num_scalar_prefetch=3, grid=(n_tiles,),
            in_specs=[
                pl.BlockSpec((tm, D), lambda m, pos, ps, pn: (m, 0)),
                pl.BlockSpec((None, None, N_MOD, D), lambda m, pos, ps, pn: (l, mrow(m), 0, 0)),
                pl.BlockSpec((None, 1, D), lambda m, pos, ps, pn: (4 * l + 2, 0, 0)),
            ],
            out_specs=pl.BlockSpec(memory_space=pl.ANY),
            scratch_shapes=[pltpu.VMEM((2, tm, D // 2), jnp.uint32),
                            pltpu.VMEM((SUBLANES, D // 2), jnp.uint32),
                            pltpu.SemaphoreType.DMA((2,)), pltpu.SemaphoreType.DMA(())]),
        out_shape=sds((Tp, D // 2), jnp.uint32),
        compiler_params=_params(("arbitrary",)),
        name="moe_dispatch",
    )(pos, starts + cnt, zero_fill, x, mods, g4)

    a = pl.pallas_call(
        _moe_up_kernel,
        grid_spec=pltpu.PrefetchScalarGridSpec(
            num_scalar_prefetch=4, grid=(EFF // tf, n_mt),
            in_specs=[
                pl.BlockSpec((tmu, D // 2), lambda f, m, te, tix, fi, nu: (tix[m], 0)),
                pl.BlockSpec((None, None, D, tf), lambda f, m, te, tix, fi, nu: (i, te[m], 0, f)),
                pl.BlockSpec((None, None, D, tf), lambda f, m, te, tix, fi, nu: (i, te[m], 0, f)),
            ],
            out_specs=pl.BlockSpec((tmu, tf), lambda f, m, te, tix, fi, nu: (tix[m], f)),
            scratch_shapes=[pltpu.VMEM((D, tf), BF16), pltpu.VMEM((D, tf), BF16)]),
        out_shape=sds((Tp, EFF), BF16),
        compiler_params=_params(("arbitrary", "arbitrary")),
        name="moe_up",
    )(te_u, tix_u, first_u, nrows_u, hs, w_e_gate, w_e_up)

    ys = pl.pallas_call(
        _moe_down_kernel,
        grid_spec=pltpu.PrefetchScalarGridSpec(
            num_scalar_prefetch=4, grid=(D // tn, n_mt_d),
            in_specs=[
                pl.BlockSpec((tm, EFF), lambda n, m, te, tix, fi, nu: (tix[m], 0)),
                pl.BlockSpec((None, None, EFF, tn), lambda n, m, te, tix, fi, nu: (i, te[m], 0, n)),
            ],
            out_specs=pl.BlockSpec((tm, tn), lambda n, m, te, tix, fi, nu: (tix[m], n)),
            scratch_shapes=[pltpu.VMEM((EFF, tn), BF16)]),
        out_shape=sds((Tp, D), F32),
        compiler_params=_params(("arbitrary", "arbitrary")),
        name="moe_down",
    )(te_d, tix_d, first_d, nrows_d, a, w_e_down)

    if n_split is None:
        out_specs = pl.BlockSpec((tm, D), lambda m, pos: (m, 0))
        out_shape = sds((T, D), F32)
    else:
        out_specs = [pl.BlockSpec((tm, D), lambda m, pos: (jnp.minimum(m, n_split - 1), 0)),
                     pl.BlockSpec((tm, D), lambda m, pos: (jnp.maximum(m - n_split, 0), 0))]
        out_shape = [sds((n_split * tm, D), F32), sds((T - n_split * tm, D), F32)]
    return pl.pallas_call(
        functools.partial(_combine_kernel, n_split=n_split),
        grid_spec=pltpu.PrefetchScalarGridSpec(
            num_scalar_prefetch=1, grid=(n_tiles,),
            in_specs=[
                pl.BlockSpec(memory_space=pl.ANY),
                pl.BlockSpec((tm, LANES), lambda m, pos: (m, 0)),
                pl.BlockSpec((tm, D), lambda m, pos: (m, 0)),
                pl.BlockSpec((None, None, N_MOD, D), lambda m, pos: (l, mrow(m), 0, 0)),
                pl.BlockSpec((None, 1, D), lambda m, pos: (4 * l + 3, 0, 0)),
            ],
            out_specs=out_specs,
            scratch_shapes=[pltpu.VMEM((2, 2 * tm, D), F32), pltpu.SemaphoreType.DMA((2,))]),
        out_shape=out_shape,
        compiler_params=_params(("arbitrary",)),
        name="moe_combine",
    )(pos, ys, rp, x, mods, g4)


def kernel(x_prompt, x_sample, cache_ckv, cache_krope, c, c_ctx, w_in, g_q, g_kv, w_uq, w_uk, w_uv,
           w_conv, w_o, w_ada, b_ada, g_sand, w_ff_gate, w_ff_up, w_ff_down, w_router,
           w_e_gate, w_e_up, w_e_down):
    B, N, D = x_prompt.shape
    DB, DN, _ = x_sample.shape
    L = w_in.shape[0]
    QL, KV = g_q.shape[1], g_kv.shape[1]
    H, NOPE = w_uk.shape[2], w_uk.shape[3]
    ROPE = w_uq.shape[3] - NOPE
    VD = w_uv.shape[3]
    CONV = w_conv.shape[2]
    AW = H * VD
    PAST = cache_ckv.shape[2]
    FF = w_ff_gate.shape[2]
    NE, EFF = w_e_gate.shape[1], w_e_gate.shape[3]
    T_P, T_S = B * N, DB * DN
    T = T_P + T_S
    DK = KV + LANES
    NA = 1024
    tm = TM
    assert ROPE == 64 and 2 * ROPE == LANES and NOPE % LANES == 0 and KV % LANES == 0 and QL % LANES == 0
    assert QL + KV + LANES <= NA and CONV % LANES == 0 and AW + CONV == w_o.shape[1]
    assert T_P % tm == 0 and DN % tm == 0 and tm % N == 0 and T_P % DN == 0
    assert N & (N - 1) == 0 and DN & (DN - 1) == 0 and DN % GRID_W == 0
    assert NE <= LANES and DB + 1 <= SUBLANES
    n_p_tiles = T_P // tm
    tiles_per_s = DN // tm
    n_tiles = T // tm
    scale = float(NOPE + ROPE) ** -0.5
    tf = 512
    assert FF % tf == 0 and EFF % tf == 0

    def mrow(m):
        return jnp.where(m < n_p_tiles, 0, 1 + (m - n_p_tiles) // tiles_per_s)

    def trow(m):
        return jnp.where(m < n_p_tiles, 0, 1 + (m - n_p_tiles) % tiles_per_s)

    t = jnp.arange(DN)
    inv = 1.0 / (ROPE_THETA ** (jnp.arange(0, ROPE // 2, 2, dtype=F32) / (ROPE // 2)))
    ar = (t // GRID_W).astype(F32)[:, None] * inv
    ac = (t % GRID_W).astype(F32)[:, None] * inv
    ang = jnp.concatenate([ar, ar, ac, ac] * 2, axis=-1)
    first = (jnp.arange(LANES) % (ROPE // 2)) < (ROPE // 4)
    cos_t = jnp.concatenate([jnp.ones((tm, LANES), F32), jnp.cos(ang)], axis=0)
    sin = jnp.sin(ang)
    sa_t = jnp.concatenate([jnp.zeros((tm, LANES), F32), jnp.where(first, -sin, 0.0)], axis=0)
    sb_t = jnp.concatenate([jnp.zeros((tm, LANES), F32), jnp.where(first, 0.0, sin)], axis=0)

    s1, s2 = QL + KV + ROPE, QL + KV + ROPE + 3 * CONV
    assert NA % CONV == 0
    w_cat = jnp.concatenate(
        [w_in[:, :, :s1], jnp.zeros((L, D, NA - s1), F32), w_in[:, :, s1:s2]], axis=-1).astype(BF16)
    wq = jnp.pad(w_uq, ((0, 0), (0, 0), (0, 0), (0, LANES - ROPE))).astype(BF16)
    wq = wq.reshape(L, QL, H * (NOPE + LANES))
    wk = w_uk.transpose(0, 2, 3, 1).astype(BF16)
    wuv = w_uv.transpose(0, 2, 1, 3).astype(BF16)
    wo = w_o.astype(BF16)
    wfg, wfu, wfd = w_ff_gate.astype(BF16), w_ff_up.astype(BF16), w_ff_down.astype(BF16)
    wr = jnp.pad(w_router, ((0, 0), (0, 0), (0, LANES - NE)))
    kcache = jnp.concatenate(
        [cache_ckv, cache_krope, jnp.zeros(cache_krope.shape[:-1] + (LANES - ROPE,), F32)],
        axis=-1).astype(BF16)

    cond8 = jnp.concatenate([c_ctx[None, :], c, jnp.zeros((SUBLANES - 1 - DB, D), F32)], axis=0)
    mods = _ada(cond8, w_ada, b_ada)[:, :1 + DB].reshape(L, 1 + DB, N_MOD, D)

    xs = (x_prompt.reshape(T_P, D), x_sample.reshape(T_S, D))
    p_idx = lambda m: jnp.minimum(m, n_p_tiles - 1)
    s_idx = lambda m: jnp.maximum(m - n_p_tiles, 0)

    mod_spec2 = lambda l: pl.BlockSpec((None, None, N_MOD, D), lambda m, n: (l, mrow(m), 0, 0))
    row_spec2 = lambda w: pl.BlockSpec((tm, w), lambda m, n: (m, 0))
    tab_spec2 = pl.BlockSpec((tm, LANES), lambda m, n: (trow(m), 0))
    tab_spec1 = pl.BlockSpec((tm, LANES), lambda m: (trow(m), 0))

    assert ROW_CHUNK % N == 0
    caches = ()
    for l in range(L):
        if l > 0:
            xs = (x,)
        cqn, new_ckv, new_krope, kcat, gates = pl.pallas_call(
            functools.partial(_inproj_kernel, QL=QL, KV=KV, ROPE=ROPE, n_x=len(xs), n_p_tiles=n_p_tiles,
                              n_alias=len(caches)),
            grid=(n_tiles, 1 + 3),
            in_specs=([row_spec2(D)] if len(xs) == 1 else
                      [pl.BlockSpec((tm, D), lambda m, n: (p_idx(m), 0)),
                       pl.BlockSpec((tm, D), lambda m, n: (s_idx(m), 0))]) + [
                pl.BlockSpec(memory_space=pl.ANY) for _ in caches] + [
                mod_spec2(l),
                pl.BlockSpec((None, 1, D), lambda m, n: (4 * l, 0, 0)),
                pl.BlockSpec((None, 1, QL), lambda m, n: (l, 0, 0)),
                pl.BlockSpec((None, 1, KV), lambda m, n: (l, 0, 0)),
                tab_spec2, tab_spec2, tab_spec2,
                pl.BlockSpec((None, D, NA), lambda m, n: (l, 0, 0)),
                pl.BlockSpec((None, D, CONV), lambda m, n: (l, 0, NA // CONV + jnp.maximum(n - 1, 0))),
            ],
            out_specs=[
                row_spec2(QL),
                pl.BlockSpec((tm // N, None, N, KV), lambda m, n: (p_idx(m), l, 0, 0)),
                pl.BlockSpec((tm // N, None, N, ROPE), lambda m, n: (p_idx(m), l, 0, 0)),
                row_spec2(DK),
                pl.BlockSpec((tm, CONV), lambda m, n: (m, jnp.maximum(n - 1, 0))),
            ],
            out_shape=[
                jax.ShapeDtypeStruct((T, QL), BF16),
                jax.ShapeDtypeStruct((B, L, N, KV), F32),
                jax.ShapeDtypeStruct((B, L, N, ROPE), F32),
                jax.ShapeDtypeStruct((T, DK), BF16),
                jax.ShapeDtypeStruct((T, 3 * CONV), BF16),
            ],
            scratch_shapes=[pltpu.VMEM((tm, D), BF16)],
            input_output_aliases={len(xs) + j: 1 + j for j in range(len(caches))},
            compiler_params=_params(("arbitrary", "arbitrary")),
            name="in_proj",
        )(*xs, *caches, mods, g_sand.reshape(4 * L, 1, D), g_q.reshape(L, 1, QL), g_kv.reshape(L, 1, KV),
          cos_t, sa_t, sb_t, w_cat, w_cat)
        caches = (new_ckv, new_krope)

        qcat = pl.pallas_call(
            functools.partial(_q_kernel, H=H, NOPE=NOPE, KV=KV, scale=scale),
            grid=(n_tiles,),
            in_specs=[
                pl.BlockSpec((tm, QL), lambda m: (m, 0)),
                pl.BlockSpec((None, QL, H * (NOPE + LANES)), lambda m: (l, 0, 0)),
                pl.BlockSpec((None, H, NOPE, KV), lambda m: (l, 0, 0, 0)),
                tab_spec1, tab_spec1, tab_spec1,
            ],
            out_specs=pl.BlockSpec((tm, H * DK), lambda m: (m, 0)),
            out_shape=jax.ShapeDtypeStruct((T, H * DK), BF16),
            compiler_params=_params(("arbitrary",)),
            name="q_proj",
        )(cqn, wq, wk, cos_t, sa_t, sb_t)

        o_p = pl.pallas_call(
            functools.partial(_attn_kernel, H=H, KV=KV, has_cache=False),
            grid=(B,),
            in_specs=[
                pl.BlockSpec((N, H * DK), lambda b: (b, 0)),
                pl.BlockSpec((N, DK), lambda b: (b, 0)),
            ],
            out_specs=pl.BlockSpec((N, H * KV), lambda b: (b, 0)),
            out_shape=jax.ShapeDtypeStruct((T_P, H * KV), BF16),
            compiler_params=_params(("arbitrary",)),
            name="attn_prompt",
        )(qcat, kcat)
        tq = 256
        qs_per = DN // tq
        o_s = pl.pallas_call(
            functools.partial(_attn_kernel, H=H, KV=KV, has_cache=True),
            grid=(DB, qs_per),
            in_specs=[
                pl.BlockSpec((tq, H * DK), lambda b, i: (T_P // tq + b * qs_per + i, 0)),
                pl.BlockSpec((DN, DK), lambda b, i: (T_P // DN + b, 0)),
                pl.BlockSpec((None, None, PAST, DK), lambda b, i: (b, l, 0, 0)),
            ],
            out_specs=pl.BlockSpec((tq, H * KV), lambda b, i: (b * qs_per + i, 0)),
            out_shape=jax.ShapeDtypeStruct((T_S, H * KV), BF16),
            compiler_params=_params(("arbitrary", "arbitrary")),
            name="attn_sample",
        )(qcat, kcat, kcache)

        r8 = tm // HALO_ROWS
        prev8 = lambda m: jnp.maximum(m * r8 - 1, 0)
        next8 = lambda m: jnp.minimum((m + 1) * r8, T // HALO_ROWS - 1)
        x = pl.pallas_call(
            functools.partial(_mix_kernel, H=H, KV=KV, VD=VD, n_p_tiles=n_p_tiles,
                              seq_p=N, seq_s=DN, t_p=T_P, n_x=len(xs)),
            grid=(n_tiles,),
            in_specs=([pl.BlockSpec((tm, D), lambda m: (m, 0))] if len(xs) == 1 else
                      [pl.BlockSpec((tm, D), lambda m: (p_idx(m), 0)),
                       pl.BlockSpec((tm, D), lambda m: (s_idx(m), 0))]) + [
                pl.BlockSpec((tm, H * KV), lambda m: (p_idx(m), 0)),
                pl.BlockSpec((tm, H * KV), lambda m: (s_idx(m), 0)),
                pl.BlockSpec((tm, CONV), lambda m: (m, 0)),
                pl.BlockSpec((tm, CONV), lambda m: (m, 1)),
                pl.BlockSpec((tm, CONV), lambda m: (m, 2)),
                pl.BlockSpec((HALO_ROWS, CONV), lambda m: (prev8(m), 1)),
                pl.BlockSpec((HALO_ROWS, CONV), lambda m: (prev8(m), 2)),
                pl.BlockSpec((HALO_ROWS, CONV), lambda m: (next8(m), 1)),
                pl.BlockSpec((HALO_ROWS, CONV), lambda m: (next8(m), 2)),
                pl.BlockSpec((None, H, KV, VD), lambda m: (l, 0, 0, 0)),
                pl.BlockSpec((None, 3, CONV), lambda m: (l, 0, 0)),
                pl.BlockSpec((None, AW + CONV, D), lambda m: (l, 0, 0)),
                pl.BlockSpec((None, 1, D), lambda m: (4 * l + 1, 0, 0)),
                pl.BlockSpec((None, None, N_MOD, D), lambda m: (l, mrow(m), 0, 0)),
            ],
            out_specs=pl.BlockSpec((tm, D), lambda m: (m, 0)),
            out_shape=jax.ShapeDtypeStruct((T, D), F32),
            scratch_shapes=[pltpu.VMEM((tm, AW + CONV), BF16),
                            pltpu.VMEM((tm + 2 * SUBLANES, CONV), F32)],
            compiler_params=_params(("arbitrary",)),
            name="mixer_out",
        )(*xs, o_p, o_s, gates, gates, gates, gates, gates, gates, gates, wuv, w_conv, wo,
          g_sand.reshape(4 * L, 1, D), mods)

        i = l // 2
        if l % 2 == 0:
            tmf = TM_FFN if (T_P % TM_FFN == 0 and DN % TM_FFN == 0) else tm
            mrow_f = lambda m: jnp.where(m < T_P // tmf, 0, 1 + (m - T_P // tmf) // (DN // tmf))
            x = pl.pallas_call(
                _ffn_kernel,
                grid=(T // tmf, FF // tf),
                in_specs=[
                    pl.BlockSpec((tmf, D), lambda m, f: (m, 0), pipeline_mode=pl.Buffered(1)),
                    pl.BlockSpec((None, None, N_MOD, D), lambda m, f: (l, mrow_f(m), 0, 0)),
                    pl.BlockSpec((None, 1, D), lambda m, f: (4 * l + 2, 0, 0)),
                    pl.BlockSpec((None, 1, D), lambda m, f: (4 * l + 3, 0, 0)),
                    pl.BlockSpec((None, D, tf), lambda m, f: (i, 0, f)),
                    pl.BlockSpec((None, D, tf), lambda m, f: (i, 0, f)),
                    pl.BlockSpec((None, tf, D), lambda m, f: (i, f, 0)),
                ],
                out_specs=pl.BlockSpec((tmf, D), lambda m, f: (m, 0)),
                out_shape=jax.ShapeDtypeStruct((T, D), F32),
                scratch_shapes=[pltpu.VMEM((tmf, D), BF16)],
                compiler_params=_params(("arbitrary", "arbitrary")),
                name="ffn_dense",
            )(x, mods, g_sand.reshape(4 * L, 1, D), g_sand.reshape(4 * L, 1, D), wfg, wfu, wfd)
        else:
            x = _moe_layer(x, mods, g_sand.reshape(4 * L, 1, D), wr, w_e_gate, w_e_up, w_e_down,
                           l=l, i=i, tm=tm, tf=tf, mrow=mrow,
                           n_split=n_p_tiles if l == L - 1 else None)

    x_p, x_s = x if isinstance(x, (list, tuple)) else (x[:T_P], x[T_P:])
    y_prompt = x_p.reshape(B, N, D)
    y_sample = x_s.reshape(DB, DN, D)
    return (y_prompt, y_sample, new_ckv, new_krope)
```

```python
import functools

import jax
import jax.numpy as jnp
from jax import lax
from jax.experimental import pallas as pl
from jax.experimental.pallas import tpu as pltpu

GRID_W = 64
ROPE_THETA = 10000.0
EPS = 1e-6
N_MOD = 6
TOP_K = 2

LANES = 128
SUBLANES = 8
TM = 512
TM_FFN = 1024
ROW_CHUNK = 256
HALO_ROWS = 16
MOE_SUB = 256
VMEM_LIMIT = 56 * 1024 * 1024

BF16 = jnp.bfloat16
F32 = jnp.float32


def _params(sem, vmem=VMEM_LIMIT):
    return pltpu.CompilerParams(dimension_semantics=sem, vmem_limit_bytes=vmem)


def _rms(x):
    return x * lax.rsqrt(jnp.mean(x * x, axis=-1, keepdims=True) + EPS)


def _rope(x, cos, sa, sb):
    return x * cos + pltpu.roll(x, LANES - 16, 1) * sa + pltpu.roll(x, 16, 1) * sb


def _ada_kernel(c_ref, w_ref, b_ref, o_ref):
    c = c_ref[...]
    s = c / (1.0 + jnp.exp(-c))
    o_ref[...] = jnp.dot(s.astype(BF16), w_ref[...].astype(BF16),
                         preferred_element_type=F32) + b_ref[...]


def _ada(cond8, w_ada, b_ada):
    L, D, N6 = w_ada.shape
    tn = 1024 if N6 % 1024 == 0 else N6
    return pl.pallas_call(
        _ada_kernel,
        grid=(L, N6 // tn),
        in_specs=[
            pl.BlockSpec((SUBLANES, D), lambda l, n: (0, 0)),
            pl.BlockSpec((None, D, tn), lambda l, n: (l, 0, n)),
            pl.BlockSpec((None, 1, tn), lambda l, n: (l, 0, n)),
        ],
        out_specs=pl.BlockSpec((None, SUBLANES, tn), lambda l, n: (l, 0, n)),
        out_shape=jax.ShapeDtypeStruct((L, SUBLANES, N6), F32),
        compiler_params=_params(("arbitrary", "arbitrary")),
        name="ada_mod",
    )(cond8, w_ada, b_ada.reshape(L, 1, N6))


def _tile_rows(refs, is_first, rows):
    if len(refs) == 1:
        return refs[0][rows, :]
    return jnp.where(is_first, refs[0][rows, :], refs[1][rows, :])


def _inproj_kernel(*refs, QL, KV, ROPE, n_x, n_p_tiles, n_alias):
    x_refs, refs = refs[:n_x], refs[n_x + n_alias:]
    (mod_ref, g_ref, gq_ref, gkv_ref, cos_ref, sa_ref, sb_ref, wa_ref, wb_ref,
     cqn_ref, ckv_ref, kr_ref, kcat_ref, gates_ref, hb_ref) = refs
    n = pl.program_id(1)
    is_p = pl.program_id(0) < n_p_tiles
    tm = hb_ref.shape[0]
    seq = ckv_ref.shape[1]

    def first_step(store_cache):
        for r0 in range(0, tm, ROW_CHUNK):
            rows = slice(r0, r0 + ROW_CHUNK)
            h = _rms(_tile_rows(x_refs, is_p, rows)) * g_ref[...]
            h = h * (1.0 + mod_ref[1:2, :]) + mod_ref[0:1, :]
            hb_ref[rows, :] = h.astype(BF16)
            acc = jnp.dot(hb_ref[rows, :], wa_ref[...], preferred_element_type=F32)
            cqn_ref[rows, :] = (_rms(acc[:, :QL]) * gq_ref[...]).astype(BF16)
            ckv = _rms(acc[:, QL:QL + KV]) * gkv_ref[...]
            kr = _rope(acc[:, QL + KV:QL + KV + LANES], cos_ref[rows, :], sa_ref[rows, :], sb_ref[rows, :])
            if store_cache:
                for j in range(ROW_CHUNK // seq):
                    ckv_ref[r0 // seq + j] = ckv[j * seq:(j + 1) * seq, :]
                    kr_ref[r0 // seq + j] = kr[j * seq:(j + 1) * seq, :ROPE]
            kcat_ref[rows, :KV] = ckv.astype(BF16)
            ones_lane = lax.broadcasted_iota(jnp.int32, kr.shape, 1) >= ROPE
            kcat_ref[rows, KV:] = jnp.where(ones_lane, 1.0, kr).astype(BF16)

    pl.when(jnp.logical_and(n == 0, is_p))(functools.partial(first_step, True))
    pl.when(jnp.logical_and(n == 0, jnp.logical_not(is_p)))(functools.partial(first_step, False))

    @pl.when(n > 0)
    def _():
        gates_ref[...] = jnp.dot(hb_ref[...], wb_ref[...], preferred_element_type=F32).astype(BF16)


def _q_kernel(cqn_ref, wq_ref, wk_ref, cos_ref, sa_ref, sb_ref, o_ref, *, H, NOPE, KV, scale):
    QW = NOPE + LANES
    DK = KV + LANES
    cqn = cqn_ref[...]
    for h in range(H):
        q = jnp.dot(cqn, wq_ref[:, h * QW:(h + 1) * QW], preferred_element_type=F32)
        q_lat = jnp.dot(q[:, :NOPE].astype(BF16), wk_ref[h], preferred_element_type=F32)
        q_rope = _rope(q[:, NOPE:], cos_ref[...], sa_ref[...], sb_ref[...])
        o_ref[:, h * DK:h * DK + KV] = (q_lat * scale).astype(BF16)
        o_ref[:, h * DK + KV:(h + 1) * DK] = (q_rope * scale).astype(BF16)


def _attn_kernel(*refs, H, KV, has_cache):
    if has_cache:
        q_ref, kn_ref, kc_ref, o_ref = refs
    else:
        q_ref, kn_ref, o_ref = refs
    DK = KV + LANES
    dn = (((1,), (1,)), ((), ()))
    kn = kn_ref[...]
    if has_cache:
        kc = kc_ref[...]
    for h in range(H):
        q = q_ref[:, h * DK:(h + 1) * DK]
        s = lax.dot_general(q, kn, dn, preferred_element_type=F32)
        m = jnp.max(s, axis=-1, keepdims=True)
        if has_cache:
            sc = lax.dot_general(q, kc, dn, preferred_element_type=F32)
            m = jnp.maximum(m, jnp.max(sc, axis=-1, keepdims=True))
        o = jnp.dot(jnp.exp(s - m).astype(BF16), kn, preferred_element_type=F32)
        if has_cache:
            o = o + jnp.dot(jnp.exp(sc - m).astype(BF16), kc, preferred_element_type=F32)
        o_ref[:, h * KV:(h + 1) * KV] = (o[:, :KV] / o[:, DK - 1:DK]).astype(BF16)


def _mix_kernel(*refs, H, KV, VD, n_p_tiles, seq_p, seq_s, t_p, n_x):
    x_refs, refs = refs[:n_x], refs[n_x:]
    (op_ref, os_ref, gb_ref, gc_ref, u_ref, gcp_ref, up_ref, gcn_ref, un_ref,
     wuv_ref, wc_ref, wo_ref, g_ref, mod_ref, out_ref, cat_ref, vs_ref) = refs
    m = pl.program_id(0)
    tm = out_ref.shape[0]
    AW = H * VD
    is_p = m < n_p_tiles
    seq = jnp.where(is_p, seq_p, seq_s)
    row0 = m * tm - jnp.where(is_p, 0, t_p)

    hr = HALO_ROWS
    vs_ref[SUBLANES:SUBLANES + tm, :] = gc_ref[...].astype(F32) * u_ref[...].astype(F32)
    vs_ref[SUBLANES - 1:SUBLANES, :] = gcp_ref[hr - 1:hr, :].astype(F32) * up_ref[hr - 1:hr, :].astype(F32)
    vs_ref[SUBLANES + tm:SUBLANES + tm + 1, :] = gcn_ref[0:1, :].astype(F32) * un_ref[0:1, :].astype(F32)

    for r0 in range(0, tm, ROW_CHUNK):
        rows = slice(r0, r0 + ROW_CHUNK)
        for h in range(H):
            cols = slice(h * KV, (h + 1) * KV)
            o = jnp.where(is_p, op_ref[rows, cols], os_ref[rows, cols])
            a = jnp.dot(o, wuv_ref[h], preferred_element_type=F32)
            cat_ref[rows, h * VD:(h + 1) * VD] = a.astype(BF16)
        pos = jnp.bitwise_and(lax.broadcasted_iota(jnp.int32, (ROW_CHUNK, 1), 0) + (row0 + r0), seq - 1)
        v = vs_ref[SUBLANES + r0:SUBLANES + r0 + ROW_CHUNK, :]
        vp = jnp.where(pos == 0, 0.0, vs_ref[SUBLANES - 1 + r0:SUBLANES - 1 + r0 + ROW_CHUNK, :])
        vn = jnp.where(pos == seq - 1, 0.0, vs_ref[SUBLANES + 1 + r0:SUBLANES + 1 + r0 + ROW_CHUNK, :])
        conv = vp * wc_ref[0:1, :] + v * wc_ref[1:2, :] + vn * wc_ref[2:3, :]
        cat_ref[rows, AW:] = (gb_ref[rows, :].astype(F32) * conv).astype(BF16)
        y = jnp.dot(cat_ref[rows, :], wo_ref[...], preferred_element_type=F32)
        out_ref[rows, :] = _tile_rows(x_refs, is_p, rows) + mod_ref[2:3, :] * (_rms(y) * g_ref[...])


def _modulate_ffn(x_ref, mod_ref, g_ref):
    h = _rms(x_ref[...]) * g_ref[...]
    return h * (1.0 + mod_ref[4:5, :]) + mod_ref[3:4, :]


def _swiglu_step(hb, wg_ref, wu_ref, wd_ref):
    g = jnp.dot(hb, wg_ref[...], preferred_element_type=F32)
    u = jnp.dot(hb, wu_ref[...], preferred_element_type=F32)
    a = (g / (1.0 + jnp.exp(-g))) * u
    return jnp.dot(a.astype(BF16), wd_ref[...], preferred_element_type=F32)


def _for_row_chunks(n_rows, chunk, fn):
    def body(c, carry):
        fn(pl.ds(pl.multiple_of(c * chunk, chunk), chunk))
        return carry
    lax.fori_loop(0, n_rows // chunk, body, 0)


def _ffn_kernel(x_ref, mod_ref, g2_ref, g3_ref, wg_ref, wu_ref, wd_ref, out_ref, hb_ref):
    f = pl.program_id(1)
    nf = pl.num_programs(1)
    tm = x_ref.shape[0]

    @pl.when(f == 0)
    def _():
        def prologue(rows):
            h = _rms(x_ref[rows, :]) * g2_ref[...]
            hb_ref[rows, :] = (h * (1.0 + mod_ref[4:5, :]) + mod_ref[3:4, :]).astype(BF16)
            out_ref[rows, :] = jnp.zeros((ROW_CHUNK, out_ref.shape[1]), F32)
        _for_row_chunks(tm, ROW_CHUNK, prologue)

    out_ref[...] += _swiglu_step(hb_ref[...], wg_ref, wu_ref, wd_ref)

    @pl.when(f == nf - 1)
    def _():
        def epilogue(rows):
            y = _rms(out_ref[rows, :]) * g3_ref[...]
            out_ref[rows, :] = x_ref[rows, :] + mod_ref[5:6, :] * y
        _for_row_chunks(tm, ROW_CHUNK, epilogue)


def _top2(logits, n_experts):
    lane = lax.broadcasted_iota(jnp.int32, logits.shape, 1)
    neg = jnp.float32(-jnp.inf)
    l1 = jnp.where(lane < n_experts, logits, neg)
    m1 = jnp.max(l1, axis=-1, keepdims=True)
    i1 = jnp.min(jnp.where(l1 == m1, lane, LANES), axis=-1, keepdims=True)
    l2 = jnp.where(lane == i1, neg, l1)
    m2 = jnp.max(l2, axis=-1, keepdims=True)
    i2 = jnp.min(jnp.where(l2 == m2, lane, LANES), axis=-1, keepdims=True)
    e2 = jnp.exp(m2 - m1)
    return lane, i1, i2, 1.0 / (1.0 + e2), e2 / (1.0 + e2)


def _route_kernel(x_ref, mod_ref, g2_ref, wr_ref, ri_ref, rp_ref, cnt_ref, carry_ref,
                  *, n_experts):
    m = pl.program_id(0)
    tm = x_ref.shape[0]

    @pl.when(m == 0)
    def _():
        carry_ref[...] = jnp.zeros_like(carry_ref)

    h = _modulate_ffn(x_ref, mod_ref, g2_ref)
    h_hi = h.astype(BF16)
    h_lo = (h - h_hi.astype(F32)).astype(BF16)
    w = wr_ref[...]
    w_hi = w.astype(BF16)
    w_lo = (w - w_hi.astype(F32)).astype(BF16)
    logits = (jnp.dot(h_hi, w_hi, preferred_element_type=F32)
              + jnp.dot(h_hi, w_lo, preferred_element_type=F32)
              + jnp.dot(h_lo, w_hi, preferred_element_type=F32))
    lane, i1, i2, p1, p2 = _top2(logits, n_experts)
    sel = jnp.logical_or(lane == i1, lane == i2)
    r = lax.broadcasted_iota(jnp.int32, (tm, tm), 0)
    c = lax.broadcasted_iota(jnp.int32, (tm, tm), 1)
    before = jnp.where(r > c, 1.0, 0.0).astype(BF16)
    rank = jnp.dot(before, jnp.where(sel, 1.0, 0.0).astype(BF16), preferred_element_type=F32)
    rank = rank + carry_ref[0:1, :]
    r1 = jnp.sum(jnp.where(lane == i1, rank, 0.0), axis=-1, keepdims=True).astype(jnp.int32)
    r2 = jnp.sum(jnp.where(lane == i2, rank, 0.0), axis=-1, keepdims=True).astype(jnp.int32)
    carry_ref[0:1, :] = carry_ref[0:1, :] + jnp.sum(jnp.where(sel, 1.0, 0.0), axis=0, keepdims=True)
    ri_ref[...] = jnp.where(lane == 0, i1, jnp.where(lane == 1, i2,
                            jnp.where(lane == 2, r1, jnp.where(lane == 3, r2, 0))))
    rp_ref[...] = jnp.where(lane == 0, p1, jnp.where(lane == 1, p2, 0.0))
    cnt_ref[...] = carry_ref[...]


def _row_copy(src_ref, src_row, dst_ref, dst_row, sem):
    return pltpu.make_async_copy(src_ref.at[pl.ds(src_row, 1), :], dst_ref.at[pl.ds(dst_row, 1), :], sem)


def _rows_wait(any_ref, n_rows, sem):
    pltpu.make_async_copy(any_ref.at[pl.ds(0, n_rows), :], any_ref.at[pl.ds(0, n_rows), :], sem).wait()


def _dispatch_kernel(pos_ref, pad_start_ref, pad_cnt_ref, x_ref, mod_ref, g2_ref, hs_ref,
                     pk_ref, zero_ref, sem, pad_sem, *, n_experts):
    m = pl.program_id(0)
    n = pl.num_programs(0)
    tm, half = x_ref.shape[0], x_ref.shape[1] // 2
    slot = m % 2

    @pl.when(m >= 2)
    def _():
        _rows_wait(hs_ref, 2 * tm, sem.at[slot])

    def pack(rows):
        h = _rms(x_ref[rows, :]) * g2_ref[...]
        h = (h * (1.0 + mod_ref[4:5, :]) + mod_ref[3:4, :]).astype(BF16).astype(F32)
        bits = lax.bitcast_convert_type(h, jnp.uint32)
        pk_ref[slot, rows, :] = jnp.bitwise_or(bits[:, half:], jnp.right_shift(bits[:, :half], 16))
    _for_row_chunks(tm, ROW_CHUNK, pack)

    def issue(r, carry):
        t = m * tm + r
        _row_copy(pk_ref.at[slot], r, hs_ref, pos_ref[2 * t], sem.at[slot]).start()
        _row_copy(pk_ref.at[slot], r, hs_ref, pos_ref[2 * t + 1], sem.at[slot]).start()
        return carry

    lax.fori_loop(0, tm, issue, 0, unroll=8)

    @pl.when(m == 0)
    def _():
        zero_ref[...] = jnp.zeros_like(zero_ref)
        for e in range(n_experts):
            start, cnt = pad_start_ref[e], pad_cnt_ref[e]

            def pad_issue(i, carry):
                _row_copy(zero_ref, 0, hs_ref, start + i, pad_sem).start()
                return carry

            def pad_wait(i, carry):
                _row_copy(zero_ref, 0, hs_ref, 0, pad_sem).wait()
                return carry

            lax.fori_loop(0, cnt, pad_issue, 0)
            lax.fori_loop(0, cnt, pad_wait, 0)

    @pl.when(m == n - 1)
    def _():
        _rows_wait(hs_ref, 2 * tm, sem.at[slot])

        @pl.when(n >= 2)
        def _():
            _rows_wait(hs_ref, 2 * tm, sem.at[1 - slot])


def _for_live_rows(nrows, tm, fn):
    for n in range(MOE_SUB, tm + 1, MOE_SUB):
        pl.when(nrows == n)(functools.partial(fn, n))


def _moe_up_kernel(te_ref, tix_ref, first_ref, nrows_ref, hs_ref, wg_ref, wu_ref, a_ref,
                   wgb_ref, wub_ref):
    m = pl.program_id(1)

    @pl.when(first_ref[m] == 1)
    def _():
        wgb_ref[...] = wg_ref[...].astype(BF16)
        wub_ref[...] = wu_ref[...].astype(BF16)

    half = hs_ref.shape[1]

    def up(n):
        w = hs_ref[0:n, :]
        lo = lax.bitcast_convert_type(jnp.left_shift(w, 16), F32).astype(BF16)
        hi = lax.bitcast_convert_type(jnp.bitwise_and(w, jnp.uint32(0xFFFF0000)), F32).astype(BF16)
        g = (jnp.dot(lo, wgb_ref[0:half, :], preferred_element_type=F32)
             + jnp.dot(hi, wgb_ref[half:, :], preferred_element_type=F32))
        u = (jnp.dot(lo, wub_ref[0:half, :], preferred_element_type=F32)
             + jnp.dot(hi, wub_ref[half:, :], preferred_element_type=F32))
        a_ref[0:n, :] = ((g / (1.0 + jnp.exp(-g))) * u).astype(BF16)
    _for_live_rows(nrows_ref[m], hs_ref.shape[0], up)


def _moe_down_kernel(te_ref, tix_ref, first_ref, nrows_ref, a_ref, wd_ref, ys_ref, wdb_ref):
    m = pl.program_id(1)

    @pl.when(first_ref[m] == 1)
    def _():
        wdb_ref[...] = wd_ref[...].astype(BF16)

    def down(n):
        ys_ref[0:n, :] = jnp.dot(a_ref[0:n, :], wdb_ref[...], preferred_element_type=F32)
    _for_live_rows(nrows_ref[m], a_ref.shape[0], down)


def _combine_kernel(pos_ref, ys_ref, rp_ref, x_ref, mod_ref, g3_ref, *rest, n_split):
    if n_split is None:
        out_ref, buf_ref, sem = rest
    else:
        out_a_ref, out_b_ref, buf_ref, sem = rest
    m = pl.program_id(0)
    n = pl.num_programs(0)
    tm = x_ref.shape[0]

    def issue_tile(tile, slot):
        def issue(r, carry):
            t = tile * tm + r
            _row_copy(ys_ref, pos_ref[2 * t], buf_ref.at[slot], r, sem.at[slot]).start()
            _row_copy(ys_ref, pos_ref[2 * t + 1], buf_ref.at[slot], tm + r, sem.at[slot]).start()
            return carry
        lax.fori_loop(0, tm, issue, 0, unroll=8)

    @pl.when(m == 0)
    def _():
        issue_tile(0, 0)

    @pl.when(m + 1 < n)
    def _():
        issue_tile(m + 1, (m + 1) % 2)

    slot = m % 2

    _rows_wait(ys_ref, 2 * tm, sem.at[slot])

    def result(rows, second):
        y = rp_ref[rows, 0:1] * buf_ref[slot, rows, :] + rp_ref[rows, 1:2] * buf_ref[slot, second, :]
        return x_ref[rows, :] + mod_ref[5:6, :] * (_rms(y) * g3_ref[...])

    def chunked(dst_ref):
        def body(c, carry):
            r0 = pl.multiple_of(c * ROW_CHUNK, ROW_CHUNK)
            dst_ref[pl.ds(r0, ROW_CHUNK), :] = result(pl.ds(r0, ROW_CHUNK), pl.ds(tm + r0, ROW_CHUNK))
            return carry
        lax.fori_loop(0, tm // ROW_CHUNK, body, 0)

    if n_split is None:
        chunked(out_ref)
    else:
        @pl.when(m < n_split)
        def _():
            chunked(out_a_ref)

        @pl.when(m >= n_split)
        def _():
            chunked(out_b_ref)


def _moe_layer(x, mods, g4, wr, w_e_gate, w_e_up, w_e_down, *, l, i, tm, tf, mrow, n_split=None):
    T, D = x.shape
    NE, EFF = w_e_gate.shape[1], w_e_gate.shape[3]
    n_tiles = T // tm
    tmu = 2 * tm
    n_mt = (TOP_K * T) // tmu + NE
    Tp = n_mt * tmu
    tn = 512
    assert D % tn == 0 and EFF % tf == 0 and (TOP_K * T) % tmu == 0
    sds = jax.ShapeDtypeStruct

    ri, rp, cnt = pl.pallas_call(
        functools.partial(_route_kernel, n_experts=NE),
        grid=(n_tiles,),
        in_specs=[
            pl.BlockSpec((tm, D), lambda m: (m, 0)),
            pl.BlockSpec((None, None, N_MOD, D), lambda m: (l, mrow(m), 0, 0)),
            pl.BlockSpec((None, 1, D), lambda m: (4 * l + 2, 0, 0)),
            pl.BlockSpec((None, D, LANES), lambda m: (i, 0, 0)),
        ],
        out_specs=[
            pl.BlockSpec((tm, LANES), lambda m: (m, 0)),
            pl.BlockSpec((tm, LANES), lambda m: (m, 0)),
            pl.BlockSpec((SUBLANES, LANES), lambda m: (0, 0)),
        ],
        out_shape=[sds((T, LANES), jnp.int32), sds((T, LANES), F32), sds((SUBLANES, LANES), F32)],
        scratch_shapes=[pltpu.VMEM((SUBLANES, LANES), F32)],
        compiler_params=_params(("arbitrary",)),
        name="moe_route",
    )(x, mods, g4, wr)

    cnt = cnt[0, :NE].astype(jnp.int32)
    pc = (cnt + tmu - 1) // tmu * tmu
    starts = jnp.cumsum(pc) - pc
    pos = jnp.stack([starts[ri[:, 0]] + ri[:, 2], starts[ri[:, 1]] + ri[:, 3]], axis=1).reshape(-1)
    zero_fill = (-cnt) % MOE_SUB

    def tile_list(t, n_max):
        per = (cnt + t - 1) // t
        upto = jnp.cumsum(per)
        ar = jnp.arange(n_max, dtype=jnp.int32)
        k = jnp.minimum(ar, upto[-1] - 1)
        te = jnp.minimum(jnp.sum(k[:, None] >= upto[None, :], axis=1), NE - 1).astype(jnp.int32)
        off = k - (upto[te] - per[te])
        tix = (starts[te] // t + off).astype(jnp.int32)
        first = jnp.logical_or(te != jnp.roll(te, 1), ar == 0).astype(jnp.int32)
        live = jnp.clip(cnt[te] - off * t, 0, t)
        nrows = jnp.where(ar < upto[-1], (live + MOE_SUB - 1) // MOE_SUB * MOE_SUB, 0).astype(jnp.int32)
        return te, tix, first, nrows

    te_u, tix_u, first_u, nrows_u = tile_list(tmu, n_mt)
    n_mt_d = (TOP_K * T) // tm + NE
    te_d, tix_d, first_d, nrows_d = tile_list(tm, n_mt_d)

    hs = pl.pallas_call(
        functools.partial(_dispatch_kernel, n_experts=NE),
        grid_spec=pltpu.PrefetchScalarGridSpec(
            num_scalar_prefetch=3, grid=(n_tiles,),
            in_specs=[
                pl.BlockSpec((tm, D), lambda m, pos, ps, pn: (m, 0)),
                pl.BlockSpec((None, None, N_MOD, D), lambda m, pos, ps, pn: (l, mrow(m), 0, 0)),
                pl.BlockSpec((None, 1, D), lambda m, pos, ps, pn: (4 * l + 2, 0, 0)),
            ],
            out_specs=pl.BlockSpec(memory_space=pl.ANY),
            scratch_shapes=[pltpu.VMEM((2, tm, D // 2), jnp.uint32),
                            pltpu.VMEM((SUBLANES, D // 2), jnp.uint32),
                            pltpu.SemaphoreType.DMA((2,)), pltpu.SemaphoreType.DMA(())]),
        out_shape=sds((Tp, D // 2), jnp.uint32),
        compiler_params=_params(("arbitrary",)),
        name="moe_dispatch",
    )(pos, starts + cnt, zero_fill, x, mods, g4)

    a = pl.pallas_call(
        _moe_up_kernel,
        grid_spec=pltpu.PrefetchScalarGridSpec(
            num_scalar_prefetch=4, grid=(EFF // tf, n_mt),
            in_specs=[
                pl.BlockSpec((tmu, D // 2), lambda f, m, te, tix, fi, nu: (tix[m], 0)),
                pl.BlockSpec((None, None, D, tf), lambda f, m, te, tix, fi, nu: (i, te[m], 0, f)),
                pl.BlockSpec((None, None, D, tf), lambda f, m, te, tix, fi, nu: (i, te[m], 0, f)),
            ],
            out_specs=pl.BlockSpec((tmu, tf), lambda f, m, te, tix, fi, nu: (tix[m], f)),
            scratch_shapes=[pltpu.VMEM((D, tf), BF16), pltpu.VMEM((D, tf), BF16)]),
        out_shape=sds((Tp, EFF), BF16),
        compiler_params=_params(("arbitrary", "arbitrary")),
        name="moe_up",
    )(te_u, tix_u, first_u, nrows_u, hs, w_e_gate, w_e_up)

    ys = pl.pallas_call(
        _moe_down_kernel,
        grid_spec=pltpu.PrefetchScalarGridSpec(
            num_scalar_prefetch=4, grid=(D // tn, n_mt_d),
            in_specs=[
                pl.BlockSpec((tm, EFF), lambda n, m, te, tix, fi, nu: (tix[m], 0)),
                pl.BlockSpec((None, None, EFF, tn), lambda n, m, te, tix, fi, nu: (i, te[m], 0, n)),
            ],
            out_specs=pl.BlockSpec((tm, tn), lambda n, m, te, tix, fi, nu: (tix[m], n)),
            scratch_shapes=[pltpu.VMEM((EFF, tn), BF16)]),
        out_shape=sds((Tp, D), F32),
        compiler_params=_params(("arbitrary", "arbitrary")),
        name="moe_down",
    )(te_d, tix_d, first_d, nrows_d, a, w_e_down)

    if n_split is None:
        out_specs = pl.BlockSpec((tm, D), lambda m, pos: (m, 0))
        out_shape = sds((T, D), F32)
    else:
        out_specs = [pl.BlockSpec((tm, D), lambda m, pos: (jnp.minimum(m, n_split - 1), 0)),
                     pl.BlockSpec((tm, D), lambda m, pos: (jnp.maximum(m - n_split, 0), 0))]
        out_shape = [sds((n_split * tm, D), F32), sds((T - n_split * tm, D), F32)]
    return pl.pallas_call(
        functools.partial(_combine_kernel, n_split=n_split),
        grid_spec=pltpu.PrefetchScalarGridSpec(
            num_scalar_prefetch=1, grid=(n_tiles,),
            in_specs=[
                pl.BlockSpec(memory_space=pl.ANY),
                pl.BlockSpec((tm, LANES), lambda m, pos: (m, 0)),
                pl.BlockSpec((tm, D), lambda m, pos: (m, 0)),
                pl.BlockSpec((None, None, N_MOD, D), lambda m, pos: (l, mrow(m), 0, 0)),
                pl.BlockSpec((None, 1, D), lambda m, pos: (4 * l + 3, 0, 0)),
            ],
            out_specs=out_specs,
            scratch_shapes=[pltpu.VMEM((2, 2 * tm, D), F32), pltpu.SemaphoreType.DMA((2,))]),
        out_shape=out_shape,
        compiler_params=_params(("arbitrary",)),
        name="moe_combine",
    )(pos, ys, rp, x, mods, g4)


def kernel(x_prompt, x_sample, cache_ckv, cache_krope, c, c_ctx, w_in, g_q, g_kv, w_uq, w_uk, w_uv,
           w_conv, w_o, w_ada, b_ada, g_sand, w_ff_gate, w_ff_up, w_ff_down, w_router,
           w_e_gate, w_e_up, w_e_down):
    B, N, D = x_prompt.shape
    DB, DN, _ = x_sample.shape
    L = w_in.shape[0]
    QL, KV = g_q.shape[1], g_kv.shape[1]
    H, NOPE = w_uk.shape[2], w_uk.shape[3]
    ROPE = w_uq.shape[3] - NOPE
    VD = w_uv.shape[3]
    CONV = w_conv.shape[2]
    AW = H * VD
    PAST = cache_ckv.shape[2]
    FF = w_ff_gate.shape[2]
    NE, EFF = w_e_gate.shape[1], w_e_gate.shape[3]
    T_P, T_S = B * N, DB * DN
    T = T_P + T_S
    DK = KV + LANES
    NA = 1024
    tm = TM
    assert ROPE == 64 and 2 * ROPE == LANES and NOPE % LANES == 0 and KV % LANES == 0 and QL % LANES == 0
    assert QL + KV + LANES <= NA and CONV % LANES == 0 and AW + CONV == w_o.shape[1]
    assert T_P % tm == 0 and DN % tm == 0 and tm % N == 0 and T_P % DN == 0
    assert N & (N - 1) == 0 and DN & (DN - 1) == 0 and DN % GRID_W == 0
    assert NE <= LANES and DB + 1 <= SUBLANES
    n_p_tiles = T_P // tm
    tiles_per_s = DN // tm
    n_tiles = T // tm
    scale = float(NOPE + ROPE) ** -0.5
    tf = 512
    assert FF % tf == 0 and EFF % tf == 0

    def mrow(m):
        return jnp.where(m < n_p_tiles, 0, 1 + (m - n_p_tiles) // tiles_per_s)

    def trow(m):
        return jnp.where(m < n_p_tiles, 0, 1 + (m - n_p_tiles) % tiles_per_s)

    t = jnp.arange(DN)
    inv = 1.0 / (ROPE_THETA ** (jnp.arange(0, ROPE // 2, 2, dtype=F32) / (ROPE // 2)))
    ar = (t // GRID_W).astype(F32)[:, None] * inv
    ac = (t % GRID_W).astype(F32)[:, None] * inv
    ang = jnp.concatenate([ar, ar, ac, ac] * 2, axis=-1)
    first = (jnp.arange(LANES) % (ROPE // 2)) < (ROPE // 4)
    cos_t = jnp.concatenate([jnp.ones((tm, LANES), F32), jnp.cos(ang)], axis=0)
    sin = jnp.sin(ang)
    sa_t = jnp.concatenate([jnp.zeros((tm, LANES), F32), jnp.where(first, -sin, 0.0)], axis=0)
    sb_t = jnp.concatenate([jnp.zeros((tm, LANES), F32), jnp.where(first, 0.0, sin)], axis=0)

    s1, s2 = QL + KV + ROPE, QL + KV + ROPE + 3 * CONV
    w_in_b = w_in.astype(BF16)
    w_a = jnp.pad(w_in_b[:, :, :s1], ((0, 0), (0, 0), (0, NA - s1)))
    w_b = w_in_b[:, :, s1:s2]
    wq = jnp.pad(w_uq, ((0, 0), (0, 0), (0, 0), (0, LANES - ROPE))).astype(BF16)
    wq = wq.reshape(L, QL, H * (NOPE + LANES))
    wk = w_uk.transpose(0, 2, 3, 1).astype(BF16)
    wuv = w_uv.transpose(0, 2, 1, 3).astype(BF16)
    wo = w_o.astype(BF16)
    wfg, wfu, wfd = w_ff_gate.astype(BF16), w_ff_up.astype(BF16), w_ff_down.astype(BF16)
    wr = jnp.pad(w_router, ((0, 0), (0, 0), (0, LANES - NE)))
    kcache = jnp.concatenate(
        [cache_ckv, cache_krope, jnp.ones(cache_krope.shape[:-1] + (LANES - ROPE,), F32)],
        axis=-1).astype(BF16)

    cond8 = jnp.concatenate([c_ctx[None, :], c, jnp.zeros((SUBLANES - 1 - DB, D), F32)], axis=0)
    mods = _ada(cond8, w_ada, b_ada)[:, :1 + DB].reshape(L, 1 + DB, N_MOD, D)

    xs = (x_prompt.reshape(T_P, D), x_sample.reshape(T_S, D))
    p_idx = lambda m: jnp.minimum(m, n_p_tiles - 1)
    s_idx = lambda m: jnp.maximum(m - n_p_tiles, 0)

    mod_spec2 = lambda l: pl.BlockSpec((None, None, N_MOD, D), lambda m, n: (l, mrow(m), 0, 0))
    row_spec2 = lambda w: pl.BlockSpec((tm, w), lambda m, n: (m, 0))
    tab_spec2 = pl.BlockSpec((tm, LANES), lambda m, n: (trow(m), 0))
    tab_spec1 = pl.BlockSpec((tm, LANES), lambda m: (trow(m), 0))

    assert ROW_CHUNK % N == 0
    caches = ()
    for l in range(L):
        if l > 0:
            xs = (x,)
        cqn, new_ckv, new_krope, kcat, gates = pl.pallas_call(
            functools.partial(_inproj_kernel, QL=QL, KV=KV, ROPE=ROPE, n_x=len(xs), n_p_tiles=n_p_tiles,
                              n_alias=len(caches)),
            grid=(n_tiles, 1 + 3),
            in_specs=([row_spec2(D)] if len(xs) == 1 else
                      [pl.BlockSpec((tm, D), lambda m, n: (p_idx(m), 0)),
                       pl.BlockSpec((tm, D), lambda m, n: (s_idx(m), 0))]) + [
                pl.BlockSpec(memory_space=pl.ANY) for _ in caches] + [
                mod_spec2(l),
                pl.BlockSpec((None, 1, D), lambda m, n: (4 * l, 0, 0)),
                pl.BlockSpec((None, 1, QL), lambda m, n: (l, 0, 0)),
                pl.BlockSpec((None, 1, KV), lambda m, n: (l, 0, 0)),
                tab_spec2, tab_spec2, tab_spec2,
                pl.BlockSpec((None, D, NA), lambda m, n: (l, 0, 0)),
                pl.BlockSpec((None, D, CONV), lambda m, n: (l, 0, jnp.maximum(n - 1, 0))),
            ],
            out_specs=[
                row_spec2(QL),
                pl.BlockSpec((tm // N, None, N, KV), lambda m, n: (p_idx(m), l, 0, 0)),
                pl.BlockSpec((tm // N, None, N, ROPE), lambda m, n: (p_idx(m), l, 0, 0)),
                row_spec2(DK),
                pl.BlockSpec((tm, CONV), lambda m, n: (m, jnp.maximum(n - 1, 0))),
            ],
            out_shape=[
                jax.ShapeDtypeStruct((T, QL), BF16),
                jax.ShapeDtypeStruct((B, L, N, KV), F32),
                jax.ShapeDtypeStruct((B, L, N, ROPE), F32),
                jax.ShapeDtypeStruct((T, DK), BF16),
                jax.ShapeDtypeStruct((T, 3 * CONV), BF16),
            ],
            scratch_shapes=[pltpu.VMEM((tm, D), BF16)],
            input_output_aliases={len(xs) + j: 1 + j for j in range(len(caches))},
            compiler_params=_params(("arbitrary", "arbitrary")),
            name="in_proj",
        )(*xs, *caches, mods, g_sand.reshape(4 * L, 1, D), g_q.reshape(L, 1, QL), g_kv.reshape(L, 1, KV),
          cos_t, sa_t, sb_t, w_a, w_b)
        caches = (new_ckv, new_krope)

        qcat = pl.pallas_call(
            functools.partial(_q_kernel, H=H, NOPE=NOPE, KV=KV, scale=scale),
            grid=(n_tiles,),
            in_specs=[
                pl.BlockSpec((tm, QL), lambda m: (m, 0)),
                pl.BlockSpec((None, QL, H * (NOPE + LANES)), lambda m: (l, 0, 0)),
                pl.BlockSpec((None, H, NOPE, KV), lambda m: (l, 0, 0, 0)),
                tab_spec1, tab_spec1, tab_spec1,
            ],
            out_specs=pl.BlockSpec((tm, H * DK), lambda m: (m, 0)),
            out_shape=jax.ShapeDtypeStruct((T, H * DK), BF16),
            compiler_params=_params(("arbitrary",)),
            name="q_proj",
        )(cqn, wq, wk, cos_t, sa_t, sb_t)

        o_p = pl.pallas_call(
            functools.partial(_attn_kernel, H=H, KV=KV, has_cache=False),
            grid=(B,),
            in_specs=[
                pl.BlockSpec((N, H * DK), lambda b: (b, 0)),
                pl.BlockSpec((N, DK), lambda b: (b, 0)),
            ],
            out_specs=pl.BlockSpec((N, H * KV), lambda b: (b, 0)),
            out_shape=jax.ShapeDtypeStruct((T_P, H * KV), BF16),
            compiler_params=_params(("arbitrary",)),
            name="attn_prompt",
        )(qcat, kcat)
        tq = 256
        qs_per = DN // tq
        o_s = pl.pallas_call(
            functools.partial(_attn_kernel, H=H, KV=KV, has_cache=True),
            grid=(DB, qs_per),
            in_specs=[
                pl.BlockSpec((tq, H * DK), lambda b, i: (T_P // tq + b * qs_per + i, 0)),
                pl.BlockSpec((DN, DK), lambda b, i: (T_P // DN + b, 0)),
                pl.BlockSpec((None, None, PAST, DK), lambda b, i: (b, l, 0, 0)),
            ],
            out_specs=pl.BlockSpec((tq, H * KV), lambda b, i: (b * qs_per + i, 0)),
            out_shape=jax.ShapeDtypeStruct((T_S, H * KV), BF16),
            compiler_params=_params(("arbitrary", "arbitrary")),
            name="attn_sample",
        )(qcat, kcat, kcache)

        r8 = tm // HALO_ROWS
        prev8 = lambda m: jnp.maximum(m * r8 - 1, 0)
        next8 = lambda m: jnp.minimum((m + 1) * r8, T // HALO_ROWS - 1)
        x = pl.pallas_call(
            functools.partial(_mix_kernel, H=H, KV=KV, VD=VD, n_p_tiles=n_p_tiles,
                              seq_p=N, seq_s=DN, t_p=T_P, n_x=len(xs)),
            grid=(n_tiles,),
            in_specs=([pl.BlockSpec((tm, D), lambda m: (m, 0))] if len(xs) == 1 else
                      [pl.BlockSpec((tm, D), lambda m: (p_idx(m), 0)),
                       pl.BlockSpec((tm, D), lambda m: (s_idx(m), 0))]) + [
                pl.BlockSpec((tm, H * KV), lambda m: (p_idx(m), 0)),
                pl.BlockSpec((tm, H * KV), lambda m: (s_idx(m), 0)),
                pl.BlockSpec((tm, CONV), lambda m: (m, 0)),
                pl.BlockSpec((tm, CONV), lambda m: (m, 1)),
                pl.BlockSpec((tm, CONV), lambda m: (m, 2)),
                pl.BlockSpec((HALO_ROWS, CONV), lambda m: (prev8(m), 1)),
                pl.BlockSpec((HALO_ROWS, CONV), lambda m: (prev8(m), 2)),
                pl.BlockSpec((HALO_ROWS, CONV), lambda m: (next8(m), 1)),
                pl.BlockSpec((HALO_ROWS, CONV), lambda m: (next8(m), 2)),
                pl.BlockSpec((None, H, KV, VD), lambda m: (l, 0, 0, 0)),
                pl.BlockSpec((None, 3, CONV), lambda m: (l, 0, 0)),
                pl.BlockSpec((None, AW + CONV, D), lambda m: (l, 0, 0)),
                pl.BlockSpec((None, 1, D), lambda m: (4 * l + 1, 0, 0)),
                pl.BlockSpec((None, None, N_MOD, D), lambda m: (l, mrow(m), 0, 0)),
            ],
            out_specs=pl.BlockSpec((tm, D), lambda m: (m, 0)),
            out_shape=jax.ShapeDtypeStruct((T, D), F32),
            scratch_shapes=[pltpu.VMEM((tm, AW + CONV), BF16),
                            pltpu.VMEM((tm + 2 * SUBLANES, CONV), F32)],
            compiler_params=_params(("arbitrary",)),
            name="mixer_out",
        )(*xs, o_p, o_s, gates, gates, gates, gates, gates, gates, gates, wuv, w_conv, wo,
          g_sand.reshape(4 * L, 1, D), mods)

        i = l // 2
        if l % 2 == 0:
            tmf = TM_FFN if (T_P % TM_FFN == 0 and DN % TM_FFN == 0) else tm
            mrow_f = lambda m: jnp.where(m < T_P // tmf, 0, 1 + (m - T_P // tmf) // (DN // tmf))
            x = pl.pallas_call(
                _ffn_kernel,
                grid=(T // tmf, FF // tf),
                in_specs=[
                    pl.BlockSpec((tmf, D), lambda m, f: (m, 0), pipeline_mode=pl.Buffered(1)),
                    pl.BlockSpec((None, None, N_MOD, D), lambda m, f: (l, mrow_f(m), 0, 0)),
                    pl.BlockSpec((None, 1, D), lambda m, f: (4 * l + 2, 0, 0)),
                    pl.BlockSpec((None, 1, D), lambda m, f: (4 * l + 3, 0, 0)),
                    pl.BlockSpec((None, D, tf), lambda m, f: (i, 0, f)),
                    pl.BlockSpec((None, D, tf), lambda m, f: (i, 0, f)),
                    pl.BlockSpec((None, tf, D), lambda m, f: (i, f, 0)),
                ],
                out_specs=pl.BlockSpec((tmf, D), lambda m, f: (m, 0)),
                out_shape=jax.ShapeDtypeStruct((T, D), F32),
                scratch_shapes=[pltpu.VMEM((tmf, D), BF16)],
                compiler_params=_params(("arbitrary", "arbitrary")),
                name="ffn_dense",
            )(x, mods, g_sand.reshape(4 * L, 1, D), g_sand.reshape(4 * L, 1, D), wfg, wfu, wfd)
        else:
            x = _moe_layer(x, mods, g_sand.reshape(4 * L, 1, D), wr, w_e_gate, w_e_up, w_e_down,
                           l=l, i=i, tm=tm, tf=tf, mrow=mrow,
                           n_split=n_p_tiles if l == L - 1 else None)

    x_p, x_s = x if isinstance(x, (list, tuple)) else (x[:T_P], x[T_P:])
    y_prompt = x_p.reshape(B, N, D)
    y_sample = x_s.reshape(DB, DN, D)
    return (y_prompt, y_sample, new_ckv, new_krope)
```
